```python
import math, functools
import jax, jax.numpy as jnp
from jax import lax
import numpy as np

D_MODEL = 1024
BATCH = 4
SEQ = 4096
DEPTH = 1
DEC_BATCH = 128
DEC_SEQ = 8
PAST_LEN = 8192
PAGE_SIZE = 128

N_META = 16
MIX_W = D_MODEL
S5_W = MIX_W // 2
S5_GROUP = 16
S5_GROUPS = S5_W // S5_GROUP
S5_STATE = 64
MLA_HEADS = 4
QK_NOPE = 128
QK_ROPE = 64
V_HEAD = 128
Q_RANK = 384
KV_RANK = 256
D_FF = 256 * ((8 * D_MODEL // 3 + 255) // 256)
CONV_W = 3
ROPE_BASE = 10000.0
Q_BLOCK = 128
EPS = 1e-6
IN_W = S5_W + Q_RANK + KV_RANK + QK_ROPE
ATTN_SCALE = 1.0 / math.sqrt(QK_NOPE + QK_ROPE)
N_PAGES = PAST_LEN // PAGE_SIZE
N_POOL = (DEC_BATCH * N_PAGES * 5) // 4

kernel_name = "hymba_s5_mla_convffn_step"

F32 = jnp.float32


def rmsnorm(x, g):
    x32 = x.astype(F32)
    y = x32 * lax.rsqrt(jnp.mean(x32 * x32, axis=-1, keepdims=True) + EPS)
    return (y * g.astype(F32)).astype(x.dtype)


def rope(x, pos):
    half = QK_ROPE // 2
    inv = ROPE_BASE ** (-jnp.arange(half, dtype=F32) / half)
    ang = pos.astype(F32)[:, None] * inv[None, :]
    ang = ang.reshape(ang.shape[:1] + (1,) * (x.ndim - 3) + (half,))
    cos, sin = jnp.cos(ang), jnp.sin(ang)
    x32 = x.astype(F32)
    x1, x2 = x32[..., :half], x32[..., half:]
    return jnp.concatenate([x1 * cos - x2 * sin, x1 * sin + x2 * cos], axis=-1).astype(x.dtype)


def project_mixers(xn, pos, lw):
    z = xn @ lw["w_in"]
    u, c_q, c_kv, k_r = jnp.split(z, [S5_W, S5_W + Q_RANK, S5_W + Q_RANK + KV_RANK], axis=-1)
    b, t = xn.shape[:2]
    q = (rmsnorm(c_q, lw["g_q"]) @ lw["w_uq"]).reshape(b, t, MLA_HEADS, QK_NOPE + QK_ROPE)
    q_lat = jnp.einsum("bthn,rhn->bthr", q[..., :QK_NOPE], lw["w_uk"])
    q_rope = rope(q[..., QK_NOPE:], pos)
    ckv = rmsnorm(c_kv, lw["g_kv"])
    kr = rope(k_r, pos)
    return u, q_lat, q_rope, ckv, kr


def s5_discretise(a_re, a_im, log_dt, b_re, b_im):
    dt = jnp.exp(log_dt.astype(F32))[:, None]
    ar, ai = a_re.astype(F32), a_im.astype(F32)
    mag = jnp.exp(dt * ar)
    abar_re, abar_im = mag * jnp.cos(dt * ai), mag * jnp.sin(dt * ai)
    num_re, num_im = abar_re - 1.0, abar_im
    den = ar * ar + ai * ai
    f_re = (num_re * ar + num_im * ai) / den
    f_im = (num_im * ar - num_re * ai) / den
    br, bi = b_re.astype(F32), b_im.astype(F32)
    bbar_re = f_re[..., None] * br - f_im[..., None] * bi
    bbar_im = f_re[..., None] * bi + f_im[..., None] * br
    return abar_re, abar_im, bbar_re, bbar_im


def _ssm_combine(e1, e2):
    a1r, a1i, b1r, b1i = e1
    a2r, a2i, b2r, b2i = e2
    return (a2r * a1r - a2i * a1i,
            a2r * a1i + a2i * a1r,
            a2r * b1r - a2i * b1i + b2r,
            a2r * b1i + a2i * b1r + b2i)


def s5_mixer(u, h0_re, h0_im, lw):
    b, t, _ = u.shape
    u32 = u.astype(F32)
    ug = u32.reshape(b, t, S5_GROUPS, S5_GROUP)
    abr, abi, bbr, bbi = s5_discretise(lw["s5_a_re"], lw["s5_a_im"], lw["s5_log_dt"], lw["s5_b_re"], lw["s5_b_im"])
    bu_re = jnp.einsum("btgh,gph->tbgp", ug, bbr)
    bu_im = jnp.einsum("btgh,gph->tbgp", ug, bbi)
    h0r, h0i = h0_re.astype(F32), h0_im.astype(F32)
    bu_re = bu_re.at[0].add(abr * h0r - abi * h0i)
    bu_im = bu_im.at[0].add(abr * h0i + abi * h0r)
    a_re = jnp.broadcast_to(abr, (t, 1) + abr.shape)
    a_im = jnp.broadcast_to(abi, (t, 1) + abi.shape)
    _, _, h_re, h_im = lax.associative_scan(_ssm_combine, (a_re, a_im, bu_re, bu_im), axis=0)
    y = (jnp.einsum("tbgp,ghp->btgh", h_re, lw["s5_c_re"].astype(F32))
         - jnp.einsum("tbgp,ghp->btgh", h_im, lw["s5_c_im"].astype(F32)))
    y = y.reshape(b, t, S5_W) + lw["s5_d"].astype(F32) * u32
    y = jax.nn.gelu(y)
    y = y * jax.nn.sigmoid(y @ lw["w_glu"].astype(F32))
    return y.astype(u.dtype), h_re[-1], h_im[-1]


def attend(q_lat, q_rope, q_pos, ckv, kr, k_pos):
    s = (jnp.einsum("bqhr,bkr->bhqk", q_lat, ckv).astype(F32)
         + jnp.einsum("bqhe,bke->bhqk", q_rope, kr).astype(F32)) * ATTN_SCALE
    s = jnp.where(k_pos[None, :] <= q_pos[:, None], s, -jnp.inf)
    p = jax.nn.softmax(s, axis=-1).astype(ckv.dtype)
    return jnp.einsum("bhqk,bkr->bqhr", p, ckv)


def prompt_attention(q_lat, q_rope, ckv, kr, pos):
    b = q_lat.shape[0]
    o_meta = attend(q_lat[:, :N_META], q_rope[:, :N_META], pos[:N_META], ckv, kr, pos)
    nb = (q_lat.shape[1] - N_META) // Q_BLOCK

    def blocks(a):
        return a[:, N_META:].reshape((b, nb, Q_BLOCK) + a.shape[2:]).swapaxes(0, 1)

    def one(args):
        ql, qr, qp = args
        return attend(ql, qr, qp, ckv, kr, pos)

    o = lax.map(one, (blocks(q_lat), blocks(q_rope), pos[N_META:].reshape(nb, Q_BLOCK)))
    o = o.swapaxes(0, 1).reshape((b, nb * Q_BLOCK) + o.shape[3:])
    return jnp.concatenate([o_meta, o], axis=1)


def sample_attention(q_lat, q_rope, ckv, kr, pos, past_ckv, past_kr):
    n_past = past_ckv.shape[1]
    s_past = (jnp.einsum("bqhr,bkr->bhqk", q_lat, past_ckv).astype(F32)
              + jnp.einsum("bqhe,bke->bhqk", q_rope, past_kr).astype(F32)) * ATTN_SCALE
    s_new = (jnp.einsum("bqhr,bkr->bhqk", q_lat, ckv).astype(F32)
             + jnp.einsum("bqhe,bke->bhqk", q_rope, kr).astype(F32)) * ATTN_SCALE
    s_new = jnp.where(pos[None, :] <= pos[:, None], s_new, -jnp.inf)
    p = jax.nn.softmax(jnp.concatenate([s_past, s_new], axis=-1), axis=-1).astype(ckv.dtype)
    return (jnp.einsum("bhqk,bkr->bqhr", p[..., :n_past], past_ckv)
            + jnp.einsum("bhqk,bkr->bqhr", p[..., n_past:], ckv))


def conv_ffn(xn, conv_prev, lw):
    gate = xn @ lw["w_gate"]
    up = xn @ lw["w_up"]
    t = xn.shape[1]
    padded = jnp.concatenate([conv_prev.astype(gate.dtype), gate], axis=1)
    conv = lw["conv_b"] + sum(lw["conv_w"][k] * padded[:, k:k + t] for k in range(CONV_W))
    h = jax.nn.silu(conv) * up
    return h @ lw["w_down"], padded[:, t:]


def hybrid_layer(x, pos, h0_re, h0_im, conv_prev, attention, lw):
    xn = rmsnorm(x, lw["g_mix"])
    u, q_lat, q_rope, ckv, kr = project_mixers(xn, pos, lw)
    y_s5, h_re, h_im = s5_mixer(u, h0_re, h0_im, lw)
    o_lat = attention(q_lat, q_rope, ckv, kr, pos)
    y_mla = jnp.einsum("bthr,rhv->bthv", o_lat, lw["w_uv"]).reshape(x.shape[0], x.shape[1], MLA_HEADS * V_HEAD)
    x = x + jnp.concatenate([y_s5, y_mla], axis=-1) @ lw["w_out"]
    f, conv_new = conv_ffn(rmsnorm(x, lw["g_ffn"]), conv_prev, lw)
    x = x + f
    return x, ckv, kr, h_re, h_im, conv_new


def setup_inputs(seed: int = 0) -> dict:
    key = jax.random.key(seed)
    ks = jax.random.split(key, 40)

    def nrm(k, shape, scale=1.0):
        return jax.random.normal(k, shape, F32) * scale

    def gain(k, shape):
        return 1.0 + 0.02 * jax.random.normal(k, shape, F32)

    page_table = jax.random.permutation(ks[7], N_POOL)[:DEC_BATCH * N_PAGES].reshape(DEC_BATCH, N_PAGES).astype(jnp.int32)
    a_im = jnp.pi * jnp.arange(S5_STATE, dtype=F32)
    return {
        "x_prompt": nrm(ks[0], (BATCH, SEQ, D_MODEL)),
        "x_sample": nrm(ks[1], (DEC_BATCH, DEC_SEQ, D_MODEL)),
        "cache_ckv": nrm(ks[2], (DEPTH, N_POOL, PAGE_SIZE, KV_RANK)),
        "cache_kr": nrm(ks[3], (DEPTH, N_POOL, PAGE_SIZE, QK_ROPE)),
        "state_s5_re": nrm(ks[4], (DEPTH, DEC_BATCH, S5_GROUPS, S5_STATE), 0.1),
        "state_s5_im": nrm(ks[5], (DEPTH, DEC_BATCH, S5_GROUPS, S5_STATE), 0.1),
        "state_conv": nrm(ks[6], (DEPTH, DEC_BATCH, CONV_W - 1, D_FF)),
        "page_table": page_table,
        "meta_tokens": nrm(ks[8], (N_META, D_MODEL)),
        "g_mix": gain(ks[9], (DEPTH, D_MODEL)),
        "w_in": nrm(ks[10], (DEPTH, D_MODEL, IN_W), D_MODEL ** -0.5),
        "g_q": gain(ks[11], (DEPTH, Q_RANK)),
        "w_uq": nrm(ks[12], (DEPTH, Q_RANK, MLA_HEADS * (QK_NOPE + QK_ROPE)), Q_RANK ** -0.5),
        "g_kv": gain(ks[13], (DEPTH, KV_RANK)),
        "w_uk": nrm(ks[14], (DEPTH, KV_RANK, MLA_HEADS, QK_NOPE), KV_RANK ** -0.5),
        "w_uv": nrm(ks[15], (DEPTH, KV_RANK, MLA_HEADS, V_HEAD), KV_RANK ** -0.5),
        "s5_a_re": -0.5 + nrm(ks[16], (DEPTH, S5_GROUPS, S5_STATE), 0.01),
        "s5_a_im": a_im + nrm(ks[17], (DEPTH, S5_GROUPS, S5_STATE), 0.01),
        "s5_log_dt": jax.random.uniform(ks[18], (DEPTH, S5_GROUPS), F32, math.log(1e-3), math.log(1e-1)),
        "s5_b_re": nrm(ks[19], (DEPTH, S5_GROUPS, S5_STATE, S5_GROUP), (2 * S5_GROUP) ** -0.5),
        "s5_b_im": nrm(ks[20], (DEPTH, S5_GROUPS, S5_STATE, S5_GROUP), (2 * S5_GROUP) ** -0.5),
        "s5_c_re": nrm(ks[21], (DEPTH, S5_GROUPS, S5_GROUP, S5_STATE), S5_STATE ** -0.5),
        "s5_c_im": nrm(ks[22], (DEPTH, S5_GROUPS, S5_GROUP, S5_STATE), S5_STATE ** -0.5),
        "s5_d": nrm(ks[23], (DEPTH, S5_W)),
        "w_glu": nrm(ks[24], (DEPTH, S5_W, S5_W), S5_W ** -0.5),
        "w_out": nrm(ks[25], (DEPTH, MIX_W, D_MODEL), MIX_W ** -0.5),
        "g_ffn": gain(ks[26], (DEPTH, D_MODEL)),
        "w_gate": nrm(ks[27], (DEPTH, D_MODEL, D_FF), D_MODEL ** -0.5),
        "w_up": nrm(ks[28], (DEPTH, D_MODEL, D_FF), D_MODEL ** -0.5),
        "conv_w": nrm(ks[29], (DEPTH, CONV_W, D_FF), CONV_W ** -0.5),
        "conv_b": nrm(ks[30], (DEPTH, D_FF), 0.01),
        "w_down": nrm(ks[31], (DEPTH, D_FF, D_MODEL), D_FF ** -0.5),
        "g_final": gain(ks[32], (D_MODEL,)),
    }


def reference(x_prompt, x_sample, cache_ckv, cache_kr, state_s5_re, state_s5_im, state_conv, page_table,
              meta_tokens, g_mix, w_in, g_q, w_uq, g_kv, w_uk, w_uv, s5_a_re, s5_a_im, s5_log_dt,
              s5_b_re, s5_b_im, s5_c_re, s5_c_im, s5_d, w_glu, w_out, g_ffn, w_gate, w_up, conv_w, conv_b,
              w_down, g_final):
    b = x_prompt.shape[0]
    db = x_sample.shape[0]
    t_p = N_META + x_prompt.shape[1]
    t_s = x_sample.shape[1]
    n_past = page_table.shape[1] * PAGE_SIZE

    hp = jnp.concatenate([jnp.broadcast_to(meta_tokens.astype(x_prompt.dtype)[None], (b, N_META, D_MODEL)), x_prompt], axis=1)
    hs = x_sample
    pos_p = jnp.arange(t_p, dtype=jnp.int32)
    pos_s = n_past + jnp.arange(t_s, dtype=jnp.int32)
    h0_zero = jnp.zeros((b, S5_GROUPS, S5_STATE), F32)
    conv_zero = jnp.zeros((b, CONV_W - 1, D_FF), x_prompt.dtype)

    ckv_p_l, kr_p_l, sre_p_l, sim_p_l, conv_p_l = [], [], [], [], []
    ckv_s_l, kr_s_l, sre_s_l, sim_s_l, conv_s_l = [], [], [], [], []
    for l in range(DEPTH):
        lw = dict(g_mix=g_mix[l], w_in=w_in[l], g_q=g_q[l], w_uq=w_uq[l], g_kv=g_kv[l], w_uk=w_uk[l],
                  w_uv=w_uv[l], s5_a_re=s5_a_re[l], s5_a_im=s5_a_im[l], s5_log_dt=s5_log_dt[l],
                  s5_b_re=s5_b_re[l], s5_b_im=s5_b_im[l], s5_c_re=s5_c_re[l], s5_c_im=s5_c_im[l],
                  s5_d=s5_d[l], w_glu=w_glu[l], w_out=w_out[l], g_ffn=g_ffn[l], w_gate=w_gate[l],
                  w_up=w_up[l], conv_w=conv_w[l], conv_b=conv_b[l], w_down=w_down[l])
        hp, ckv_p, kr_p, sre_p, sim_p, conv_p = hybrid_layer(hp, pos_p, h0_zero, h0_zero, conv_zero, prompt_attention, lw)
        past_ckv = cache_ckv[l, page_table].reshape(db, n_past, KV_RANK)
        past_kr = cache_kr[l, page_table].reshape(db, n_past, QK_ROPE)
        attn_s = functools.partial(sample_attention, past_ckv=past_ckv, past_kr=past_kr)
        hs, ckv_s, kr_s, sre_s, sim_s, conv_s = hybrid_layer(hs, pos_s, state_s5_re[l], state_s5_im[l], state_conv[l], attn_s, lw)
        ckv_p_l.append(ckv_p); kr_p_l.append(kr_p); sre_p_l.append(sre_p); sim_p_l.append(sim_p); conv_p_l.append(conv_p)
        ckv_s_l.append(ckv_s); kr_s_l.append(kr_s); sre_s_l.append(sre_s); sim_s_l.append(sim_s); conv_s_l.append(conv_s)

    y_prompt = rmsnorm(hp[:, N_META:], g_final)
    y_sample = rmsnorm(hs, g_final)
    return (y_prompt, y_sample,
            jnp.stack(ckv_p_l), jnp.stack(kr_p_l), jnp.stack(sre_p_l), jnp.stack(sim_p_l), jnp.stack(conv_p_l),
            jnp.stack(ckv_s_l), jnp.stack(kr_s_l), jnp.stack(sre_s_l), jnp.stack(sim_s_l), jnp.stack(conv_s_l))
```

```python
import functools
import math

import jax
import jax.numpy as jnp
from jax import lax
from jax.experimental import pallas as pl
from jax.experimental.pallas import tpu as pltpu

F32 = jnp.float32
BF16 = jnp.bfloat16

D_MODEL = 1024
N_META = 16
S5_W = 512
S5_GROUP = 16
S5_GROUPS = 32
S5_STATE = 64
MLA_HEADS = 4
QK_NOPE = 128
QK_ROPE = 64
V_HEAD = 128
Q_RANK = 384
KV_RANK = 256
D_FF = 2816
CONV_W = 3
ROPE_BASE = 10000.0
EPS = 1e-6
PAGE_SIZE = 128
ATTN_SCALE = 1.0 / math.sqrt(QK_NOPE + QK_ROPE)

LANE = 128
ROPE_PAD = LANE
QK_CAT = KV_RANK + ROPE_PAD
IN_PAD = S5_W + Q_RANK + KV_RANK + ROPE_PAD
S5_CHUNK = 16
S5_PAIR = 2
FF_CHUNK = 256
VMEM_LIMIT = 56 * 1024 * 1024
NEG_INF = float("-inf")
HI = lax.Precision.HIGHEST


def _const_spec(shape):
    nd = len(shape)
    return pl.BlockSpec(shape, lambda *_: (0,) * nd, pipeline_mode=pl.Buffered(1))


def _rms(x, g):
    return x * lax.rsqrt(jnp.mean(x * x, axis=-1, keepdims=True) + EPS) * g


def _rope_slab(x, cos, sin):
    lane = lax.broadcasted_iota(jnp.int32, x.shape, 1)
    half = QK_ROPE // 2
    swapped = jnp.where(lane < half, pltpu.roll(x, LANE - half, 1), pltpu.roll(x, half, 1))
    return x * cos + swapped * sin


def _dot(a, b):
    return jnp.dot(a, b, preferred_element_type=F32)


def _dot_nt(a, b):
    return lax.dot_general(a, b, (((1,), (1,)), ((), ())), preferred_element_type=F32)


def _pre_kernel(x_ref, cos_ref, sin_ref, gmix_ref, win_ref, gq_ref, wuq_ref, wuk_ref, gkv_ref,
                u_ref, q_ref, kcat_ref, ckv_ref, kr_ref):
    x = x_ref[...]
    xn = _rms(x, gmix_ref[...]).astype(BF16)
    z = _dot(xn, win_ref[...])
    u_ref[...] = z[:, :S5_W]
    cq = z[:, S5_W:S5_W + Q_RANK]
    ckv_raw = z[:, S5_W + Q_RANK:S5_W + Q_RANK + KV_RANK]
    kr_raw = z[:, S5_W + Q_RANK + KV_RANK:]
    cos = cos_ref[...]
    sin = sin_ref[...]
    q = _dot(_rms(cq, gq_ref[...]).astype(BF16), wuq_ref[...])
    nope_w = MLA_HEADS * QK_NOPE
    for h in range(MLA_HEADS):
        qn = q[:, h * QK_NOPE:(h + 1) * QK_NOPE].astype(BF16)
        q_lat = _dot(qn, wuk_ref[h]) * ATTN_SCALE
        qr = _rope_slab(q[:, nope_w + h * ROPE_PAD:nope_w + (h + 1) * ROPE_PAD], cos, sin) * ATTN_SCALE
        q_ref[h, :, :KV_RANK] = q_lat.astype(q_ref.dtype)
        q_ref[h, :, KV_RANK:] = qr.astype(q_ref.dtype)
    ckv = _rms(ckv_raw, gkv_ref[...])
    kr = _rope_slab(kr_raw, cos, sin)
    ckv_ref[...] = ckv
    kr_ref[...] = kr[:, :QK_ROPE]
    kcat_ref[:, :KV_RANK] = ckv.astype(kcat_ref.dtype)
    kcat_ref[:, KV_RANK:] = kr.astype(kcat_ref.dtype)


def _pre_call(x2d, cos, sin, tab_map, tm, qdtype, w):
    rows = x2d.shape[0]
    assert rows % tm == 0
    row = lambda width: pl.BlockSpec((tm, width), lambda i: (i, 0))
    return pl.pallas_call(
        _pre_kernel,
        grid=(rows // tm,),
        in_specs=[
            row(D_MODEL),
            pl.BlockSpec((tm, ROPE_PAD), tab_map),
            pl.BlockSpec((tm, ROPE_PAD), tab_map),
            _const_spec((1, D_MODEL)),
            _const_spec((D_MODEL, IN_PAD)),
            _const_spec((1, Q_RANK)),
            _const_spec((Q_RANK, MLA_HEADS * (QK_NOPE + ROPE_PAD))),
            _const_spec((MLA_HEADS, QK_NOPE, KV_RANK)),
            _const_spec((1, KV_RANK)),
        ],
        out_specs=[
            row(S5_W),
            pl.BlockSpec((MLA_HEADS, tm, QK_CAT), lambda i: (0, i, 0)),
            row(QK_CAT),
            row(KV_RANK),
            row(QK_ROPE),
        ],
        out_shape=[
            jax.ShapeDtypeStruct((rows, S5_W), F32),
            jax.ShapeDtypeStruct((MLA_HEADS, rows, QK_CAT), qdtype),
            jax.ShapeDtypeStruct((rows, QK_CAT), qdtype),
            jax.ShapeDtypeStruct((rows, KV_RANK), F32),
            jax.ShapeDtypeStruct((rows, QK_ROPE), F32),
        ],
        compiler_params=pltpu.CompilerParams(dimension_semantics=("arbitrary",), vmem_limit_bytes=VMEM_LIMIT),
        name="pre",
    )(x2d, cos, sin, w["g_mix"], w["w_in"], w["g_q"], w["w_uq"], w["w_uk_t"], w["g_kv"])


def _s5_state_in_kernel(u_ref, wre_ref, wim_ref, sre_ref, sim_ref):
    u = jnp.concatenate([u_ref[0], u_ref[1]], axis=1)
    sre_ref[...] = _dot(u, wre_ref[0])
    sim_ref[...] = _dot(u, wim_ref[0])


def _s5_scan_kernel(sre_ref, sim_ref, are_ref, aim_ref, hre_ref, him_ref, fre_ref, fim_ref):
    nb, nc, width = sre_ref.shape
    ar = are_ref[...].reshape(1, 1, width)
    ai = aim_ref[...].reshape(1, 1, width)

    def body(c, carry):
        hr, hi = carry
        hre_ref[:, pl.ds(c, 1), :] = hr
        him_ref[:, pl.ds(c, 1), :] = hi
        sr = sre_ref[:, pl.ds(c, 1), :]
        si = sim_ref[:, pl.ds(c, 1), :]
        return ar * hr - ai * hi + sr, ar * hi + ai * hr + si

    zero = jnp.zeros((nb, 1, width), F32)
    hr, hi = lax.fori_loop(0, nc, body, (zero, zero))
    fre_ref[...] = hr
    fim_ref[...] = hi


def _s5_out_kernel(u_ref, hre_ref, him_ref, t_ref, mre_ref, mim_ref, y_ref):
    half = y_ref.shape[2]
    u = jnp.concatenate([u_ref[0], u_ref[1]], axis=1)
    y = (_dot(u, t_ref[0]) + _dot(hre_ref[...].astype(BF16), mre_ref[0])
         + _dot(him_ref[...].astype(BF16), mim_ref[0]))
    y_ref[0] = y[:, :half]
    y_ref[1] = y[:, half:]


def _s5_prompt(u_g, ops, nb):
    g, rows, cw = u_g.shape
    nc = rows // nb
    npair = g // S5_PAIR
    pw = S5_PAIR * S5_STATE
    u_spec = pl.BlockSpec((S5_PAIR, rows, cw), lambda p: (p, 0, 0))
    s_re, s_im = pl.pallas_call(
        _s5_state_in_kernel,
        grid=(npair,),
        in_specs=[u_spec,
                  pl.BlockSpec((1, S5_PAIR * cw, pw), lambda p: (p, 0, 0)),
                  pl.BlockSpec((1, S5_PAIR * cw, pw), lambda p: (p, 0, 0))],
        out_specs=[pl.BlockSpec((rows, pw), lambda p: (0, p))] * 2,
        out_shape=[jax.ShapeDtypeStruct((rows, npair * pw), F32)] * 2,
        compiler_params=pltpu.CompilerParams(dimension_semantics=("arbitrary",), vmem_limit_bytes=VMEM_LIMIT),
        name="s5_state_in",
    )(u_g, ops["w_re"], ops["w_im"])

    width = npair * pw
    sw = 512
    blk = pl.BlockSpec((nb, nc, sw), lambda j: (0, 0, j))
    vec = pl.BlockSpec((1, sw), lambda j: (0, j))
    fin = pl.BlockSpec((nb, 1, sw), lambda j: (0, 0, j))
    h_re, h_im, f_re, f_im = pl.pallas_call(
        _s5_scan_kernel,
        grid=(width // sw,),
        in_specs=[blk, blk, vec, vec],
        out_specs=[blk, blk, fin, fin],
        out_shape=[jax.ShapeDtypeStruct((nb, nc, width), F32)] * 2 + [jax.ShapeDtypeStruct((nb, 1, width), F32)] * 2,
        compiler_params=pltpu.CompilerParams(dimension_semantics=("arbitrary",), vmem_limit_bytes=VMEM_LIMIT),
        name="s5_scan",
    )(s_re.reshape(nb, nc, width), s_im.reshape(nb, nc, width), ops["a_re"], ops["a_im"])

    h_spec = pl.BlockSpec((rows, pw), lambda p: (0, p))
    y_g = pl.pallas_call(
        _s5_out_kernel,
        grid=(npair,),
        in_specs=[u_spec, h_spec, h_spec,
                  pl.BlockSpec((1, S5_PAIR * cw, S5_PAIR * cw), lambda p: (p, 0, 0)),
                  pl.BlockSpec((1, pw, S5_PAIR * cw), lambda p: (p, 0, 0)),
                  pl.BlockSpec((1, pw, S5_PAIR * cw), lambda p: (p, 0, 0))],
        out_specs=pl.BlockSpec((S5_PAIR, rows, cw), lambda p: (p, 0, 0)),
        out_shape=jax.ShapeDtypeStruct((g, rows, cw), F32),
        compiler_params=pltpu.CompilerParams(dimension_semantics=("arbitrary",), vmem_limit_bytes=VMEM_LIMIT),
        name="s5_out",
    )(u_g, h_re.reshape(rows, width), h_im.reshape(rows, width), ops["t"], ops["m_re"], ops["m_im"])
    return y_g, f_re, f_im


def _s5_sample_kernel(u_ref, h0re_ref, h0im_ref, t_ref, mre_ref, mim_ref, wre_ref, wim_ref, are_ref, aim_ref,
                      y_ref, h1re_ref, h1im_ref):
    half = y_ref.shape[2]
    u = jnp.concatenate([u_ref[0], u_ref[1]], axis=1)
    hr = h0re_ref[...]
    hi = h0im_ref[...]
    y = _dot(u, t_ref[0]) + _dot(hr.astype(BF16), mre_ref[0]) + _dot(hi.astype(BF16), mim_ref[0])
    y_ref[0] = y[:, :half]
    y_ref[1] = y[:, half:]
    ar = are_ref[...]
    ai = aim_ref[...]
    h1re_ref[...] = ar * hr - ai * hi + _dot(u, wre_ref[0])
    h1im_ref[...] = ar * hi + ai * hr + _dot(u, wim_ref[0])


def _s5_sample(u_g, h0_re, h0_im, ops):
    g, rows, cw = u_g.shape
    npair = g // S5_PAIR
    pw = S5_PAIR * S5_STATE
    h_spec = pl.BlockSpec((rows, pw), lambda p: (0, p))
    pair = lambda a, b: pl.BlockSpec((1, a, b), lambda p: (p, 0, 0))
    return pl.pallas_call(
        _s5_sample_kernel,
        grid=(npair,),
        in_specs=[pl.BlockSpec((S5_PAIR, rows, cw), lambda p: (p, 0, 0)), h_spec, h_spec,
                  pair(S5_PAIR * cw, S5_PAIR * cw), pair(pw, S5_PAIR * cw), pair(pw, S5_PAIR * cw),
                  pair(S5_PAIR * cw, pw), pair(S5_PAIR * cw, pw),
                  pl.BlockSpec((1, pw), lambda p: (0, p)), pl.BlockSpec((1, pw), lambda p: (0, p))],
        out_specs=[pl.BlockSpec((S5_PAIR, rows, cw), lambda p: (p, 0, 0)), h_spec, h_spec],
        out_shape=[jax.ShapeDtypeStruct((g, rows, cw), F32),
                   jax.ShapeDtypeStruct(h0_re.shape, F32), jax.ShapeDtypeStruct(h0_im.shape, F32)],
        compiler_params=pltpu.CompilerParams(dimension_semantics=("arbitrary",), vmem_limit_bytes=VMEM_LIMIT),
        name="s5_sample",
    )(u_g, h0_re, h0_im, ops["t"], ops["m_re"], ops["m_im"], ops["w_re"], ops["w_im"], ops["a_re"], ops["a_im"])


def _block_diag_pairs(m):
    g, a, b = m.shape
    m = m.reshape(g // S5_PAIR, S5_PAIR, a, b)
    z = jnp.zeros_like(m[:, 0])
    top = jnp.concatenate([m[:, 0], z], axis=2)
    bot = jnp.concatenate([z, m[:, 1]], axis=2)
    return jnp.concatenate([top, bot], axis=1)


def _s5_operators(a_re, a_im, log_dt, b_re, b_im, c_re, c_im, steps):
    dt = jnp.exp(log_dt)[:, None]
    mag = jnp.exp(dt * a_re)
    abr, abi = mag * jnp.cos(dt * a_im), mag * jnp.sin(dt * a_im)
    num_re, num_im = abr - 1.0, abi
    den = a_re * a_re + a_im * a_im
    f_re = (num_re * a_re + num_im * a_im) / den
    f_im = (num_im * a_re - num_re * a_im) / den
    bbr = f_re[..., None] * b_re - f_im[..., None] * b_im
    bbi = f_re[..., None] * b_im + f_im[..., None] * b_re

    def power_step(carry, _):
        pr, pi = carry
        return (pr * abr - pi * abi, pr * abi + pi * abr), (pr, pi)

    _, (pw_r, pw_i) = lax.scan(power_step, (jnp.ones_like(abr), jnp.zeros_like(abr)), None, length=steps + 1)
    cp_r = c_re[None] * pw_r[:, :, None, :] - c_im[None] * pw_i[:, :, None, :]
    cp_i = c_re[None] * pw_i[:, :, None, :] + c_im[None] * pw_r[:, :, None, :]
    kern = (jnp.einsum("kghp,gpj->kghj", cp_r[:steps], bbr, precision=HI)
            - jnp.einsum("kghp,gpj->kghj", cp_i[:steps], bbi, precision=HI))
    s_idx = jnp.arange(steps)[:, None]
    t_idx = jnp.arange(steps)[None, :]
    lag = t_idx - s_idx
    toe = jnp.where((lag >= 0)[:, :, None, None, None], kern[jnp.clip(lag, 0, steps - 1)], 0.0)
    g = a_re.shape[0]
    t_mat = toe.transpose(2, 0, 4, 1, 3).reshape(g, steps * S5_GROUP, steps * S5_GROUP)
    rev_r, rev_i = pw_r[steps - 1::-1][:steps], pw_i[steps - 1::-1][:steps]
    w_r = rev_r[..., None] * bbr[None] - rev_i[..., None] * bbi[None]
    w_i = rev_r[..., None] * bbi[None] + rev_i[..., None] * bbr[None]
    w_r = w_r.transpose(1, 0, 3, 2).reshape(g, steps * S5_GROUP, S5_STATE)
    w_i = w_i.transpose(1, 0, 3, 2).reshape(g, steps * S5_GROUP, S5_STATE)
    m_r = cp_r[1:].transpose(1, 3, 0, 2).reshape(g, S5_STATE, steps * S5_GROUP)
    m_i = -cp_i[1:].transpose(1, 3, 0, 2).reshape(g, S5_STATE, steps * S5_GROUP)
    return {
        "t": _block_diag_pairs(t_mat).astype(BF16),
        "w_re": _block_diag_pairs(w_r).astype(BF16),
        "w_im": _block_diag_pairs(w_i).astype(BF16),
        "m_re": _block_diag_pairs(m_r).astype(BF16),
        "m_im": _block_diag_pairs(m_i).astype(BF16),
        "a_re": pw_r[steps].reshape(1, g * S5_STATE),
        "a_im": pw_i[steps].reshape(1, g * S5_STATE),
    }


def _softmax_update(s, v, m_ref, l_ref, acc_ref):
    m_prev = m_ref[...]
    m_new = jnp.maximum(m_prev, jnp.max(s, axis=1, keepdims=True))
    alpha = jnp.exp(m_prev - m_new)
    p = jnp.exp(s - m_new)
    l_ref[...] = alpha * l_ref[...] + jnp.sum(p, axis=1, keepdims=True)
    acc_ref[...] = alpha * acc_ref[...] + _dot(p.astype(BF16), v)
    m_ref[...] = m_new


def _attn_prompt_kernel(q_ref, k_ref, kmeta_ref, wuv_ref, o_ref, m_ref, l_ref, acc_ref, *, tq):
    qi = pl.program_id(1)
    rows = MLA_HEADS * tq
    q = q_ref[...].reshape(rows, QK_CAT)

    km = kmeta_ref[...]
    s = _dot_nt(q, km)
    col = lax.broadcasted_iota(jnp.int32, s.shape, 1)
    s = jnp.where(col < N_META, s, NEG_INF)
    m0 = jnp.max(s, axis=1, keepdims=True)
    p = jnp.exp(s - m0)
    m_ref[...] = m0
    l_ref[...] = jnp.sum(p, axis=1, keepdims=True)
    acc_ref[...] = _dot(p.astype(BF16), km[:, :KV_RANK])

    def body(kj, carry):
        k = k_ref[0, pl.ds(pl.multiple_of(kj * tq, tq), tq), :]
        _softmax_update(_dot_nt(q, k), k[:, :KV_RANK], m_ref, l_ref, acc_ref)
        return carry

    lax.fori_loop(0, qi, body, 0)

    k = k_ref[0, pl.ds(pl.multiple_of(qi * tq, tq), tq), :]
    s = _dot_nt(q, k)
    t_q = lax.broadcasted_iota(jnp.int32, s.shape, 0) % tq
    t_k = lax.broadcasted_iota(jnp.int32, s.shape, 1)
    s = jnp.where(t_k <= t_q, s, NEG_INF)
    _softmax_update(s, k[:, :KV_RANK], m_ref, l_ref, acc_ref)

    o = (acc_ref[...] / l_ref[...]).astype(BF16)
    for h in range(MLA_HEADS):
        o_ref[:, h * V_HEAD:(h + 1) * V_HEAD] = _dot(o[h * tq:(h + 1) * tq], wuv_ref[h]).astype(o_ref.dtype)


def _attn_prompt(q, kcat, kmeta, w_uv, nb, tq):
    t = kcat.shape[1]
    nq = t // tq
    rows = MLA_HEADS * tq
    return pl.pallas_call(
        functools.partial(_attn_prompt_kernel, tq=tq),
        grid=(nb, nq),
        in_specs=[
            pl.BlockSpec((MLA_HEADS, tq, QK_CAT), lambda b, i: (0, b * nq + i, 0)),
            pl.BlockSpec((1, t, QK_CAT), lambda b, i: (b, 0, 0)),
            _const_spec(kmeta.shape),
            _const_spec(w_uv.shape),
        ],
        out_specs=pl.BlockSpec((tq, MLA_HEADS * V_HEAD), lambda b, i: (b * nq + i, 0)),
        out_shape=jax.ShapeDtypeStruct((nb * t, MLA_HEADS * V_HEAD), BF16),
        scratch_shapes=[pltpu.VMEM((rows, 1), F32), pltpu.VMEM((rows, 1), F32), pltpu.VMEM((rows, KV_RANK), F32)],
        compiler_params=pltpu.CompilerParams(dimension_semantics=("arbitrary", "arbitrary"),
                                             vmem_limit_bytes=VMEM_LIMIT),
        name="attn_prompt",
    )(q, kcat, kmeta, w_uv)


def _attn_meta_kernel(q_ref, kmeta_ref, wuv_ref, o_ref):
    rows = MLA_HEADS * N_META
    q = q_ref[...].reshape(rows, QK_CAT)
    km = kmeta_ref[...]
    s = _dot_nt(q, km)
    t_q = lax.broadcasted_iota(jnp.int32, s.shape, 0) % N_META
    t_k = lax.broadcasted_iota(jnp.int32, s.shape, 1)
    s = jnp.where(t_k <= t_q, s, NEG_INF)
    p = jnp.exp(s - jnp.max(s, axis=1, keepdims=True))
    o = _dot(p.astype(BF16), km[:, :KV_RANK]) / jnp.sum(p, axis=1, keepdims=True)
    o = o.astype(BF16)
    for h in range(MLA_HEADS):
        o_ref[:, h * V_HEAD:(h + 1) * V_HEAD] = _dot(o[h * N_META:(h + 1) * N_META], wuv_ref[h]).astype(o_ref.dtype)


def _attn_meta(q, kmeta, w_uv):
    return pl.pallas_call(
        _attn_meta_kernel,
        out_shape=jax.ShapeDtypeStruct((N_META, MLA_HEADS * V_HEAD), BF16),
        name="attn_meta",
    )(q, kmeta, w_uv)


def _attn_sample_kernel(pt_ref, q_ref, knew_ref, wuv_ref, *rest, n_pages, steps):
    ckv_refs = rest[:n_pages]
    kr_refs = rest[n_pages:2 * n_pages]
    o_ref, kbuf, rbuf, m_ref, l_ref, acc_ref = rest[2 * n_pages:]
    g = pl.program_id(1)
    rows = MLA_HEADS * steps

    @pl.when(g == 0)
    def _():
        m_ref[...] = jnp.full(m_ref.shape, NEG_INF, F32)
        l_ref[...] = jnp.zeros(l_ref.shape, F32)
        acc_ref[...] = jnp.zeros(acc_ref.shape, F32)

    for j in range(n_pages):
        kbuf[j * PAGE_SIZE:(j + 1) * PAGE_SIZE, :] = ckv_refs[j][...].astype(BF16)
        rbuf[j * PAGE_SIZE:(j + 1) * PAGE_SIZE, :] = kr_refs[j][...].astype(BF16)
    q = q_ref[...].reshape(rows, QK_CAT).astype(BF16)
    v = kbuf[...]
    s = _dot_nt(q[:, :KV_RANK], v) + _dot_nt(q[:, KV_RANK:KV_RANK + QK_ROPE], rbuf[...])
    _softmax_update(s, v, m_ref, l_ref, acc_ref)

    @pl.when(g == pl.num_programs(1) - 1)
    def _():
        kn = jnp.concatenate([knew_ref[...], jnp.zeros((PAGE_SIZE - steps, QK_CAT), F32)], axis=0).astype(BF16)
        s_new = _dot_nt(q, kn)
        t_q = lax.broadcasted_iota(jnp.int32, s_new.shape, 0) % steps
        t_k = lax.broadcasted_iota(jnp.int32, s_new.shape, 1)
        s_new = jnp.where(t_k <= t_q, s_new, NEG_INF)
        _softmax_update(s_new, kn[:, :KV_RANK], m_ref, l_ref, acc_ref)
        o = (acc_ref[...] / l_ref[...]).astype(BF16)
        for h in range(MLA_HEADS):
            o_ref[:, h * V_HEAD:(h + 1) * V_HEAD] = _dot(o[h * steps:(h + 1) * steps], wuv_ref[h])


def _attn_sample(page_table, q, knew, cache_ckv, cache_kr, w_uv, steps, n_pages):
    nb, pages_per_seq = page_table.shape
    assert pages_per_seq % n_pages == 0
    ng = pages_per_seq // n_pages
    rows = MLA_HEADS * steps

    def page_spec(width, j):
        return pl.BlockSpec((None, None, PAGE_SIZE, width), lambda b, g, pt: (0, pt[b, g * n_pages + j], 0, 0))

    const = lambda shape: pl.BlockSpec(shape, lambda b, g, pt: (0,) * len(shape))
    grid_spec = pltpu.PrefetchScalarGridSpec(
        num_scalar_prefetch=1,
        grid=(nb, ng),
        in_specs=[pl.BlockSpec((MLA_HEADS, steps, QK_CAT), lambda b, g, pt: (0, b, 0)),
                  pl.BlockSpec((steps, QK_CAT), lambda b, g, pt: (b, 0)),
                  const(w_uv.shape)]
                 + [page_spec(KV_RANK, j) for j in range(n_pages)]
                 + [page_spec(QK_ROPE, j) for j in range(n_pages)],
        out_specs=pl.BlockSpec((steps, MLA_HEADS * V_HEAD), lambda b, g, pt: (b, 0)),
        scratch_shapes=[pltpu.VMEM((n_pages * PAGE_SIZE, KV_RANK), BF16),
                        pltpu.VMEM((n_pages * PAGE_SIZE, QK_ROPE), BF16),
                        pltpu.VMEM((rows, 1), F32), pltpu.VMEM((rows, 1), F32), pltpu.VMEM((rows, KV_RANK), F32)],
    )
    return pl.pallas_call(
        functools.partial(_attn_sample_kernel, n_pages=n_pages, steps=steps),
        grid_spec=grid_spec,
        out_shape=jax.ShapeDtypeStruct((nb * steps, MLA_HEADS * V_HEAD), F32),
        compiler_params=pltpu.CompilerParams(dimension_semantics=("arbitrary", "arbitrary"),
                                             vmem_limit_bytes=VMEM_LIMIT),
        name="attn_sample",
    )(page_table, q, knew, w_uv, *([cache_ckv] * n_pages), *([cache_kr] * n_pages))


def _gelu_tanh(x):
    return 0.5 * x * (1.0 + jnp.tanh(math.sqrt(2.0 / math.pi) * (x + 0.044715 * (x * x * x))))


def _sigmoid(x):
    return 1.0 / (1.0 + jnp.exp(-x))


def _post_kernel(x_ref, yssm_ref, u_ref, ymla_ref, p1_ref, p2_ref, d_ref, wglu_ref, wout_ref, gffn_ref,
                 wgate_ref, wup_ref, convw_ref, convb_ref, wdown_ref, gfin_ref,
                 y_ref, gate_ref, hbuf, carry_ref, *, tm, seq_steps):
    long_seq = seq_steps == 0
    y = yssm_ref[...] + d_ref[...] * u_ref[...]
    y = _gelu_tanh(y)
    y = y * _sigmoid(_dot(y.astype(BF16), wglu_ref[...]))
    mixed = _dot(y.astype(BF16), wout_ref[:S5_W, :]) + _dot(ymla_ref[...].astype(BF16), wout_ref[S5_W:, :])
    x1 = x_ref[...] + mixed
    xn = _rms(x1, gffn_ref[...]).astype(BF16)

    row = lax.broadcasted_iota(jnp.int32, (tm, FF_CHUNK), 0)
    if long_seq:
        @pl.when(pl.program_id(1) == 0)
        def _():
            carry_ref[...] = p1_ref[...]
        t_in = row
    else:
        t_in = row % seq_steps

    for c in range(D_FF // FF_CHUNK):
        sl = slice(c * FF_CHUNK, (c + 1) * FF_CHUNK)
        gate = _dot(xn, wgate_ref[:, sl])
        up = _dot(xn, wup_ref[:, sl])
        if long_seq:
            back1 = jnp.broadcast_to(carry_ref[7:8, sl], gate.shape)
            back2 = jnp.where(t_in == 0, carry_ref[6:7, sl], back1)
            carry_ref[:, sl] = gate[tm - 8:, :]
            gate_ref[:, sl] = gate[tm - 8:, :]
        else:
            back1 = p1_ref[:, sl]
            back2 = p2_ref[:, sl]
            gate_ref[:, sl] = gate
        prev1 = jnp.where(t_in >= 1, pltpu.roll(gate, 1, 0), back1)
        prev2 = jnp.where(t_in >= 2, pltpu.roll(gate, 2, 0), back2)
        conv = (convb_ref[:, sl] + convw_ref[0:1, sl] * prev2 + convw_ref[1:2, sl] * prev1
                + convw_ref[2:3, sl] * gate)
        hbuf[:, sl] = (conv * _sigmoid(conv) * up).astype(BF16)

    x2 = x1 + _dot(hbuf[...], wdown_ref[...])
    y_ref[...] = _rms(x2, gfin_ref[...])


def _post_call(x2d, y_ssm, u, y_mla, p1, p2, w, nb, tm, seq_steps):
    rows = x2d.shape[0]
    nt = rows // (nb * tm)
    assert nb * nt * tm == rows
    long_seq = seq_steps == 0
    row = lambda width: pl.BlockSpec((tm, width), lambda b, i: (b * nt + i, 0))
    if long_seq:
        hist = [_const_spec((8, D_FF)), _const_spec((8, D_FF))]
        gate_spec = pl.BlockSpec((None, 8, D_FF), lambda b, i: (b, 0, 0))
        gate_shape = jax.ShapeDtypeStruct((nb, 8, D_FF), F32)
    else:
        hist = [row(D_FF), row(D_FF)]
        gate_spec = row(D_FF)
        gate_shape = jax.ShapeDtypeStruct((rows, D_FF), F32)
    return pl.pallas_call(
        functools.partial(_post_kernel, tm=tm, seq_steps=seq_steps),
        grid=(nb, nt),
        in_specs=[row(D_MODEL), row(S5_W), row(S5_W), row(MLA_HEADS * V_HEAD)] + hist + [
            _const_spec((1, S5_W)), _const_spec((S5_W, S5_W)), _const_spec((D_MODEL, D_MODEL)),
            _const_spec((1, D_MODEL)), _const_spec((D_MODEL, D_FF)), _const_spec((D_MODEL, D_FF)),
            _const_spec((CONV_W, D_FF)), _const_spec((1, D_FF)), _const_spec((D_FF, D_MODEL)),
            _const_spec((1, D_MODEL))],
        out_specs=[row(D_MODEL), gate_spec],
        out_shape=[jax.ShapeDtypeStruct((rows, D_MODEL), F32), gate_shape],
        scratch_shapes=[pltpu.VMEM((tm, D_FF), BF16), pltpu.VMEM((8, D_FF), F32)],
        compiler_params=pltpu.CompilerParams(dimension_semantics=("arbitrary", "arbitrary"),
                                             vmem_limit_bytes=VMEM_LIMIT),
        name="post",
    )(x2d, y_ssm, u, y_mla, p1, p2, w["s5_d"], w["w_glu"], w["w_out"], w["g_ffn"], w["w_gate"], w["w_up"],
      w["conv_w"], w["conv_b"], w["w_down"], w["g_final"])


def _rope_tables(pos):
    half = QK_ROPE // 2
    inv = ROPE_BASE ** (-jnp.arange(half, dtype=F32) / half)
    ang = pos.astype(F32)[:, None] * inv[None, :]
    cos, sin = jnp.cos(ang), jnp.sin(ang)
    pad = jnp.zeros((pos.shape[0], ROPE_PAD - QK_ROPE), F32)
    return jnp.concatenate([cos, cos, pad], axis=1), jnp.concatenate([-sin, sin, pad], axis=1)


def _to_groups(u, nb, steps_per_row):
    rows = u.shape[0] // steps_per_row
    u = u.reshape(rows, steps_per_row, S5_GROUPS, S5_GROUP).transpose(2, 0, 1, 3)
    return u.reshape(S5_GROUPS, rows, steps_per_row * S5_GROUP).astype(BF16)


def _from_groups(y, steps_per_row):
    g, rows, _ = y.shape
    return y.reshape(g, rows, steps_per_row, S5_GROUP).transpose(1, 2, 0, 3).reshape(rows * steps_per_row, S5_W)


def kernel(x_prompt, x_sample, cache_ckv, cache_kr, state_s5_re, state_s5_im, state_conv, page_table, meta_tokens, g_mix, w_in, g_q, w_uq, g_kv, w_uk, w_uv, s5_a_re, s5_a_im, s5_log_dt, s5_b_re, s5_b_im, s5_c_re, s5_c_im, s5_d, w_glu, w_out, g_ffn, w_gate, w_up, conv_w, conv_b, w_down, g_final):
    assert w_in.shape[0] == 1, "single-layer step"
    nb, seq, _ = x_prompt.shape
    db, steps, _ = x_sample.shape
    n_past = page_table.shape[1] * PAGE_SIZE
    assert seq % S5_CHUNK == 0 and N_META == S5_CHUNK and steps * 2 == S5_CHUNK

    w_in_pad = jnp.concatenate([w_in[0], jnp.zeros((D_MODEL, IN_PAD - w_in.shape[2]), F32)], axis=1)
    wq = w_uq[0].reshape(Q_RANK, MLA_HEADS, QK_NOPE + QK_ROPE)
    wq_rope = jnp.concatenate([wq[:, :, QK_NOPE:], jnp.zeros((Q_RANK, MLA_HEADS, ROPE_PAD - QK_ROPE), F32)], axis=2)
    wq_perm = jnp.concatenate([wq[:, :, :QK_NOPE].reshape(Q_RANK, -1), wq_rope.reshape(Q_RANK, -1)], axis=1)
    w = {
        "g_mix": g_mix[0].reshape(1, -1), "w_in": w_in_pad.astype(BF16),
        "g_q": g_q[0].reshape(1, -1), "w_uq": wq_perm.astype(BF16),
        "w_uk_t": w_uk[0].transpose(1, 2, 0).astype(BF16),
        "g_kv": g_kv[0].reshape(1, -1),
        "s5_d": s5_d[0].reshape(1, -1), "w_glu": w_glu[0].astype(BF16), "w_out": w_out[0].astype(BF16),
        "g_ffn": g_ffn[0].reshape(1, -1), "w_gate": w_gate[0].astype(BF16), "w_up": w_up[0].astype(BF16),
        "conv_w": conv_w[0], "conv_b": conv_b[0].reshape(1, -1), "w_down": w_down[0].astype(BF16),
        "g_final": g_final.reshape(1, -1),
    }
    w_uv_h = w_uv[0].transpose(1, 0, 2).astype(BF16)
    s5_args = (s5_a_re[0], s5_a_im[0], s5_log_dt[0], s5_b_re[0], s5_b_im[0], s5_c_re[0], s5_c_im[0])
    ops16 = _s5_operators(*s5_args, steps=S5_CHUNK)
    ops8 = _s5_operators(*s5_args, steps=steps)

    cos_m, sin_m = _rope_tables(jnp.arange(N_META, dtype=jnp.int32))
    cos_p, sin_p = _rope_tables(N_META + jnp.arange(seq, dtype=jnp.int32))
    tm_p = 512
    tm_s = 512
    cos_s, sin_s = _rope_tables(n_past + jnp.arange(tm_s, dtype=jnp.int32) % steps)

    tiles_per_seq = seq // tm_p
    u_m, q_m, kcat_m, ckv_m, kr_m = _pre_call(meta_tokens, cos_m, sin_m, lambda i: (0, 0), N_META, BF16, w)
    u_p, q_p, kcat_p, ckv_p, kr_p = _pre_call(x_prompt.reshape(nb * seq, D_MODEL), cos_p, sin_p,
                                              lambda i: (i % tiles_per_seq, 0), tm_p, BF16, w)
    u_s, q_s, kcat_s, ckv_s, kr_s = _pre_call(x_sample.reshape(db * steps, D_MODEL), cos_s, sin_s,
                                              lambda i: (0, 0), tm_s, F32, w)

    u_full = jnp.concatenate([jnp.broadcast_to(u_m[None], (nb, N_META, S5_W)), u_p.reshape(nb, seq, S5_W)], axis=1)
    y_g, s5_re_p, s5_im_p = _s5_prompt(_to_groups(u_full.reshape(-1, S5_W), nb, S5_CHUNK), ops16, nb)
    yssm_full = _from_groups(y_g, S5_CHUNK).reshape(nb, N_META + seq, S5_W)
    yssm_m = yssm_full[0, :N_META]
    yssm_p = yssm_full[:, N_META:].reshape(nb * seq, S5_W)
    y_gs, s5_re_s, s5_im_s = _s5_sample(_to_groups(u_s, db, steps), state_s5_re[0].reshape(db, -1),
                                        state_s5_im[0].reshape(db, -1), ops8)
    yssm_s = _from_groups(y_gs, steps)

    kmeta = jnp.concatenate([kcat_m, jnp.zeros((PAGE_SIZE - N_META, QK_CAT), BF16)], axis=0)
    ymla_m = _attn_meta(q_m, kmeta, w_uv_h)
    ymla_p = _attn_prompt(q_p, kcat_p.reshape(nb, seq, QK_CAT), kmeta, w_uv_h, nb, tq=256)
    ymla_s = _attn_sample(page_table, q_s, kcat_s, cache_ckv, cache_kr, w_uv_h, steps, n_pages=16)

    zeros_hist = jnp.zeros((8, D_FF), F32)
    _, gate_m = _post_call(meta_tokens, yssm_m, u_m, ymla_m, zeros_hist, zeros_hist, w, 1, N_META, 0)
    y_p, gate_p = _post_call(x_prompt.reshape(nb * seq, D_MODEL), yssm_p, u_p, ymla_p, gate_m[0], gate_m[0],
                             w, nb, tm_p, 0)
    sc = state_conv[0]
    back1 = jnp.pad(sc[:, 1:2], ((0, 0), (0, steps - 1), (0, 0))).reshape(db * steps, D_FF)
    back2 = jnp.pad(sc, ((0, 0), (0, steps - 2), (0, 0))).reshape(db * steps, D_FF)
    y_s, gate_s = _post_call(x_sample.reshape(db * steps, D_MODEL), yssm_s, u_s, ymla_s, back1, back2,
                             w, 1, tm_s // 2, steps)

    def with_meta(meta_rows, tok_rows, width):
        return jnp.concatenate([jnp.broadcast_to(meta_rows[None], (nb, N_META, width)),
                                tok_rows.reshape(nb, seq, width)], axis=1)[None]

    return (y_p.reshape(nb, seq, D_MODEL), y_s.reshape(db, steps, D_MODEL),
            with_meta(ckv_m, ckv_p, KV_RANK), with_meta(kr_m, kr_p, QK_ROPE),
            s5_re_p.reshape(1, nb, S5_GROUPS, S5_STATE), s5_im_p.reshape(1, nb, S5_GROUPS, S5_STATE),
            gate_p[:, 8 - (CONV_W - 1):][None],
            ckv_s.reshape(1, db, steps, KV_RANK), kr_s.reshape(1, db, steps, QK_ROPE),
            s5_re_s.reshape(1, db, S5_GROUPS, S5_STATE), s5_im_s.reshape(1, db, S5_GROUPS, S5_STATE),
            gate_s.reshape(db, steps, D_FF)[:, steps - (CONV_W - 1):][None])
```

```python
import functools
import math

import jax
import jax.numpy as jnp
from jax import lax
from jax.experimental import pallas as pl
from jax.experimental.pallas import tpu as pltpu

F32 = jnp.float32
BF16 = jnp.bfloat16

D_MODEL = 1024
N_META = 16
S5_W = 512
S5_GROUP = 16
S5_GROUPS = 32
S5_STATE = 64
MLA_HEADS = 4
QK_NOPE = 128
QK_ROPE = 64
V_HEAD = 128
Q_RANK = 384
KV_RANK = 256
D_FF = 2816
CONV_W = 3
ROPE_BASE = 10000.0
EPS = 1e-6
PAGE_SIZE = 128
ATTN_SCALE = 1.0 / math.sqrt(QK_NOPE + QK_ROPE)

LANE = 128
ROPE_PAD = LANE
QK_CAT = KV_RANK + ROPE_PAD
IN_PAD = S5_W + Q_RANK + KV_RANK + ROPE_PAD
S5_CHUNK = 16
S5_PAIR = 2
FF_CHUNK = 256
PAGE_GROUP = 4
VMEM_LIMIT = 56 * 1024 * 1024
NEG_INF = float("-inf")
HI = lax.Precision.HIGHEST


def _const_spec(shape):
    nd = len(shape)
    return pl.BlockSpec(shape, lambda *_: (0,) * nd, pipeline_mode=pl.Buffered(1))


def _rms(x, g):
    return x * lax.rsqrt(jnp.mean(x * x, axis=-1, keepdims=True) + EPS) * g


def _rope_slab(x, cos, sin):
    lane = lax.broadcasted_iota(jnp.int32, x.shape, 1)
    half = QK_ROPE // 2
    swapped = jnp.where(lane < half, pltpu.roll(x, LANE - half, 1), pltpu.roll(x, half, 1))
    return x * cos + swapped * sin


def _dot(a, b):
    return jnp.dot(a, b, preferred_element_type=F32)


def _dot_nt(a, b):
    return lax.dot_general(a, b, (((1,), (1,)), ((), ())), preferred_element_type=F32)


def _pre_kernel(x_ref, cos_ref, sin_ref, cost_ref, sint_ref, gmix_ref, win_ref, gq_ref, wuq_ref, wuk_ref, gkv_ref,
                u_ref, q_ref, kcat_ref, ckv_ref, kr_ref, *maybe_vt_ref, q_transposed):
    x = x_ref[...]
    xn = _rms(x, gmix_ref[...]).astype(BF16)
    z = _dot(xn, win_ref[...])
    u_ref[...] = z[:, :S5_W]
    cq = z[:, S5_W:S5_W + Q_RANK]
    ckv_raw = z[:, S5_W + Q_RANK:S5_W + Q_RANK + KV_RANK]
    kr_raw = z[:, S5_W + Q_RANK + KV_RANK:]
    cqn = _rms(cq, gq_ref[...]).astype(BF16)
    nope_w = MLA_HEADS * QK_NOPE
    half = QK_ROPE // 2
    if q_transposed:
        qt = _dot_nt(wuq_ref[...], cqn)
        cost = cost_ref[...]
        sint = sint_ref[...]
        for h in range(MLA_HEADS):
            qn = qt[h * QK_NOPE:(h + 1) * QK_NOPE].astype(BF16)
            q_lat = _dot(wuk_ref[h], qn) * ATTN_SCALE
            xr = qt[nope_w + h * ROPE_PAD:nope_w + (h + 1) * ROPE_PAD]
            swapped = jnp.concatenate([xr[half:QK_ROPE], xr[:half], xr[QK_ROPE:]], axis=0)
            qr = (xr * cost + swapped * sint) * ATTN_SCALE
            q_ref[h, :KV_RANK, :] = q_lat.astype(q_ref.dtype)
            q_ref[h, KV_RANK:, :] = qr.astype(q_ref.dtype)
    else:
        q = _dot(cqn, wuq_ref[...])
        for h in range(MLA_HEADS):
            qn = q[:, h * QK_NOPE:(h + 1) * QK_NOPE].astype(BF16)
            q_lat = _dot(qn, wuk_ref[h]) * ATTN_SCALE
            qr = _rope_slab(q[:, nope_w + h * ROPE_PAD:nope_w + (h + 1) * ROPE_PAD],
                            cos_ref[...], sin_ref[...]) * ATTN_SCALE
            q_ref[h, :, :KV_RANK] = q_lat.astype(q_ref.dtype)
            q_ref[h, :, KV_RANK:] = qr.astype(q_ref.dtype)
    ckv = _rms(ckv_raw, gkv_ref[...])
    kr = _rope_slab(kr_raw, cos_ref[...], sin_ref[...])
    ckv_ref[...] = ckv
    kr_ref[...] = kr[:, :QK_ROPE]
    kcat_ref[:, :KV_RANK] = ckv.astype(kcat_ref.dtype)
    kcat_ref[:, KV_RANK:] = kr.astype(kcat_ref.dtype)
    if q_transposed:
        maybe_vt_ref[0][...] = ckv.T.astype(BF16)


def _pre_call(x2d, cos, sin, tab_map, tm, qdtype, w, q_transposed=False):
    rows = x2d.shape[0]
    assert rows % tm == 0
    row = lambda width: pl.BlockSpec((tm, width), lambda i: (i, 0))
    tab_map_t = lambda i: tab_map(i)[::-1]
    q_w = MLA_HEADS * (QK_NOPE + ROPE_PAD)
    if q_transposed:
        wuq, wuk = w["w_uq_t"], w["w_uk"]
        q_spec = pl.BlockSpec((MLA_HEADS, QK_CAT, tm), lambda i: (0, 0, i))
        q_shape = jax.ShapeDtypeStruct((MLA_HEADS, QK_CAT, rows), qdtype)
        extra_specs = [pl.BlockSpec((KV_RANK, tm), lambda i: (0, i))]
        extra_shapes = [jax.ShapeDtypeStruct((KV_RANK, rows), BF16)]
    else:
        wuq, wuk = w["w_uq"], w["w_uk_t"]
        q_spec = pl.BlockSpec((MLA_HEADS, tm, QK_CAT), lambda i: (0, i, 0))
        q_shape = jax.ShapeDtypeStruct((MLA_HEADS, rows, QK_CAT), qdtype)
        extra_specs, extra_shapes = [], []
    return pl.pallas_call(
        functools.partial(_pre_kernel, q_transposed=q_transposed),
        grid=(rows // tm,),
        in_specs=[
            row(D_MODEL),
            pl.BlockSpec((tm, ROPE_PAD), tab_map),
            pl.BlockSpec((tm, ROPE_PAD), tab_map),
            pl.BlockSpec((ROPE_PAD, tm), tab_map_t),
            pl.BlockSpec((ROPE_PAD, tm), tab_map_t),
            _const_spec((1, D_MODEL)),
            _const_spec((D_MODEL, IN_PAD)),
            _const_spec((1, Q_RANK)),
            _const_spec(wuq.shape),
            _const_spec(wuk.shape),
            _const_spec((1, KV_RANK)),
        ],
        out_specs=[row(S5_W), q_spec, row(QK_CAT), row(KV_RANK), row(QK_ROPE)] + extra_specs,
        out_shape=[
            jax.ShapeDtypeStruct((rows, S5_W), F32),
            q_shape,
            jax.ShapeDtypeStruct((rows, QK_CAT), qdtype),
            jax.ShapeDtypeStruct((rows, KV_RANK), F32),
            jax.ShapeDtypeStruct((rows, QK_ROPE), F32),
        ] + extra_shapes,
        compiler_params=pltpu.CompilerParams(dimension_semantics=("arbitrary",), vmem_limit_bytes=VMEM_LIMIT),
        name="pre",
    )(x2d, cos, sin, cos.T, sin.T, w["g_mix"], w["w_in"], w["g_q"], wuq, wuk, w["g_kv"])


def _s5_state_in_kernel(u_ref, wre_ref, wim_ref, sre_ref, sim_ref):
    u = jnp.concatenate([u_ref[0], u_ref[1]], axis=1)
    sre_ref[...] = _dot(u, wre_ref[0])
    sim_ref[...] = _dot(u, wim_ref[0])


def _s5_scan_kernel(sre_ref, sim_ref, are_ref, aim_ref, hre_ref, him_ref, fre_ref, fim_ref):
    nb, nc, width = sre_ref.shape
    ar = are_ref[...].reshape(1, 1, width)
    ai = aim_ref[...].reshape(1, 1, width)

    def body(c, carry):
        hr, hi = carry
        hre_ref[:, pl.ds(c, 1), :] = hr
        him_ref[:, pl.ds(c, 1), :] = hi
        sr = sre_ref[:, pl.ds(c, 1), :]
        si = sim_ref[:, pl.ds(c, 1), :]
        return ar * hr - ai * hi + sr, ar * hi + ai * hr + si

    zero = jnp.zeros((nb, 1, width), F32)
    hr, hi = lax.fori_loop(0, nc, body, (zero, zero))
    fre_ref[...] = hr
    fim_ref[...] = hi


def _s5_out_kernel(u_ref, hre_ref, him_ref, t_ref, mre_ref, mim_ref, y_ref):
    half = y_ref.shape[2]
    u = jnp.concatenate([u_ref[0], u_ref[1]], axis=1)
    y = (_dot(u, t_ref[0]) + _dot(hre_ref[...].astype(BF16), mre_ref[0])
         + _dot(him_ref[...].astype(BF16), mim_ref[0]))
    y_ref[0] = y[:, :half]
    y_ref[1] = y[:, half:]


def _s5_prompt(u_g, ops, nb):
    g, rows, cw = u_g.shape
    nc = rows // nb
    npair = g // S5_PAIR
    pw = S5_PAIR * S5_STATE
    u_spec = pl.BlockSpec((S5_PAIR, rows, cw), lambda p: (p, 0, 0))
    s_re, s_im = pl.pallas_call(
        _s5_state_in_kernel,
        grid=(npair,),
        in_specs=[u_spec,
                  pl.BlockSpec((1, S5_PAIR * cw, pw), lambda p: (p, 0, 0)),
                  pl.BlockSpec((1, S5_PAIR * cw, pw), lambda p: (p, 0, 0))],
        out_specs=[pl.BlockSpec((rows, pw), lambda p: (0, p))] * 2,
        out_shape=[jax.ShapeDtypeStruct((rows, npair * pw), F32)] * 2,
        compiler_params=pltpu.CompilerParams(dimension_semantics=("arbitrary",), vmem_limit_bytes=VMEM_LIMIT),
        name="s5_state_in",
    )(u_g, ops["w_re"], ops["w_im"])

    width = npair * pw
    sw = 512
    blk = pl.BlockSpec((nb, nc, sw), lambda j: (0, 0, j))
    vec = pl.BlockSpec((1, sw), lambda j: (0, j))
    fin = pl.BlockSpec((nb, 1, sw), lambda j: (0, 0, j))
    h_re, h_im, f_re, f_im = pl.pallas_call(
        _s5_scan_kernel,
        grid=(width // sw,),
        in_specs=[blk, blk, vec, vec],
        out_specs=[blk, blk, fin, fin],
        out_shape=[jax.ShapeDtypeStruct((nb, nc, width), F32)] * 2 + [jax.ShapeDtypeStruct((nb, 1, width), F32)] * 2,
        compiler_params=pltpu.CompilerParams(dimension_semantics=("arbitrary",), vmem_limit_bytes=VMEM_LIMIT),
        name="s5_scan",
    )(s_re.reshape(nb, nc, width), s_im.reshape(nb, nc, width), ops["a_re"], ops["a_im"])

    h_spec = pl.BlockSpec((rows, pw), lambda p: (0, p))
    y_g = pl.pallas_call(
        _s5_out_kernel,
        grid=(npair,),
        in_specs=[u_spec, h_spec, h_spec,
                  pl.BlockSpec((1, S5_PAIR * cw, S5_PAIR * cw), lambda p: (p, 0, 0)),
                  pl.BlockSpec((1, pw, S5_PAIR * cw), lambda p: (p, 0, 0)),
                  pl.BlockSpec((1, pw, S5_PAIR * cw), lambda p: (p, 0, 0))],
        out_specs=pl.BlockSpec((S5_PAIR, rows, cw), lambda p: (p, 0, 0)),
        out_shape=jax.ShapeDtypeStruct((g, rows, cw), F32),
        compiler_params=pltpu.CompilerParams(dimension_semantics=("arbitrary",), vmem_limit_bytes=VMEM_LIMIT),
        name="s5_out",
    )(u_g, h_re.reshape(rows, width), h_im.reshape(rows, width), ops["t"], ops["m_re"], ops["m_im"])
    return y_g, f_re, f_im


def _s5_sample_kernel(u_ref, h0re_ref, h0im_ref, t_ref, mre_ref, mim_ref, wre_ref, wim_ref, are_ref, aim_ref,
                      y_ref, h1re_ref, h1im_ref):
    half = y_ref.shape[2]
    u = jnp.concatenate([u_ref[0], u_ref[1]], axis=1)
    hr = h0re_ref[...]
    hi = h0im_ref[...]
    y = _dot(u, t_ref[0]) + _dot(hr.astype(BF16), mre_ref[0]) + _dot(hi.astype(BF16), mim_ref[0])
    y_ref[0] = y[:, :half]
    y_ref[1] = y[:, half:]
    ar = are_ref[...]
    ai = aim_ref[...]
    h1re_ref[...] = ar * hr - ai * hi + _dot(u, wre_ref[0])
    h1im_ref[...] = ar * hi + ai * hr + _dot(u, wim_ref[0])


def _s5_sample(u_g, h0_re, h0_im, ops):
    g, rows, cw = u_g.shape
    npair = g // S5_PAIR
    pw = S5_PAIR * S5_STATE
    h_spec = pl.BlockSpec((rows, pw), lambda p: (0, p))
    pair = lambda a, b: pl.BlockSpec((1, a, b), lambda p: (p, 0, 0))
    return pl.pallas_call(
        _s5_sample_kernel,
        grid=(npair,),
        in_specs=[pl.BlockSpec((S5_PAIR, rows, cw), lambda p: (p, 0, 0)), h_spec, h_spec,
                  pair(S5_PAIR * cw, S5_PAIR * cw), pair(pw, S5_PAIR * cw), pair(pw, S5_PAIR * cw),
                  pair(S5_PAIR * cw, pw), pair(S5_PAIR * cw, pw),
                  pl.BlockSpec((1, pw), lambda p: (0, p)), pl.BlockSpec((1, pw), lambda p: (0, p))],
        out_specs=[pl.BlockSpec((S5_PAIR, rows, cw), lambda p: (p, 0, 0)), h_spec, h_spec],
        out_shape=[jax.ShapeDtypeStruct((g, rows, cw), F32),
                   jax.ShapeDtypeStruct(h0_re.shape, F32), jax.ShapeDtypeStruct(h0_im.shape, F32)],
        compiler_params=pltpu.CompilerParams(dimension_semantics=("arbitrary",), vmem_limit_bytes=VMEM_LIMIT),
        name="s5_sample",
    )(u_g, h0_re, h0_im, ops["t"], ops["m_re"], ops["m_im"], ops["w_re"], ops["w_im"], ops["a_re"], ops["a_im"])


def _block_diag_pairs(m):
    g, a, b = m.shape
    m = m.reshape(g // S5_PAIR, S5_PAIR, a, b)
    z = jnp.zeros_like(m[:, 0])
    top = jnp.concatenate([m[:, 0], z], axis=2)
    bot = jnp.concatenate([z, m[:, 1]], axis=2)
    return jnp.concatenate([top, bot], axis=1)


def _s5_operators(a_re, a_im, log_dt, b_re, b_im, c_re, c_im, steps):
    dt = jnp.exp(log_dt)[:, None]
    mag = jnp.exp(dt * a_re)
    abr, abi = mag * jnp.cos(dt * a_im), mag * jnp.sin(dt * a_im)
    num_re, num_im = abr - 1.0, abi
    den = a_re * a_re + a_im * a_im
    f_re = (num_re * a_re + num_im * a_im) / den
    f_im = (num_im * a_re - num_re * a_im) / den
    bbr = f_re[..., None] * b_re - f_im[..., None] * b_im
    bbi = f_re[..., None] * b_im + f_im[..., None] * b_re

    def power_step(carry, _):
        pr, pi = carry
        return (pr * abr - pi * abi, pr * abi + pi * abr), (pr, pi)

    _, (pw_r, pw_i) = lax.scan(power_step, (jnp.ones_like(abr), jnp.zeros_like(abr)), None, length=steps + 1)
    cp_r = c_re[None] * pw_r[:, :, None, :] - c_im[None] * pw_i[:, :, None, :]
    cp_i = c_re[None] * pw_i[:, :, None, :] + c_im[None] * pw_r[:, :, None, :]
    kern = (jnp.einsum("kghp,gpj->kghj", cp_r[:steps], bbr, precision=HI)
            - jnp.einsum("kghp,gpj->kghj", cp_i[:steps], bbi, precision=HI))
    s_idx = jnp.arange(steps)[:, None]
    t_idx = jnp.arange(steps)[None, :]
    lag = t_idx - s_idx
    toe = jnp.where((lag >= 0)[:, :, None, None, None], kern[jnp.clip(lag, 0, steps - 1)], 0.0)
    g = a_re.shape[0]
    t_mat = toe.transpose(2, 0, 4, 1, 3).reshape(g, steps * S5_GROUP, steps * S5_GROUP)
    rev_r, rev_i = pw_r[steps - 1::-1][:steps], pw_i[steps - 1::-1][:steps]
    w_r = rev_r[..., None] * bbr[None] - rev_i[..., None] * bbi[None]
    w_i = rev_r[..., None] * bbi[None] + rev_i[..., None] * bbr[None]
    w_r = w_r.transpose(1, 0, 3, 2).reshape(g, steps * S5_GROUP, S5_STATE)
    w_i = w_i.transpose(1, 0, 3, 2).reshape(g, steps * S5_GROUP, S5_STATE)
    m_r = cp_r[1:].transpose(1, 3, 0, 2).reshape(g, S5_STATE, steps * S5_GROUP)
    m_i = -cp_i[1:].transpose(1, 3, 0, 2).reshape(g, S5_STATE, steps * S5_GROUP)
    return {
        "t": _block_diag_pairs(t_mat).astype(BF16),
        "w_re": _block_diag_pairs(w_r).astype(BF16),
        "w_im": _block_diag_pairs(w_i).astype(BF16),
        "m_re": _block_diag_pairs(m_r).astype(BF16),
        "m_im": _block_diag_pairs(m_i).astype(BF16),
        "a_re": pw_r[steps].reshape(1, g * S5_STATE),
        "a_im": pw_i[steps].reshape(1, g * S5_STATE),
    }


def _softmax_update(s, v, m_ref, l_ref, acc_ref):
    m_prev = m_ref[...]
    m_new = jnp.maximum(m_prev, jnp.max(s, axis=1, keepdims=True))
    alpha = jnp.exp(m_prev - m_new)
    p = jnp.exp(s - m_new)
    l_ref[...] = alpha * l_ref[...] + jnp.sum(p, axis=1, keepdims=True)
    acc_ref[...] = alpha * acc_ref[...] + _dot(p.astype(BF16), v)
    m_ref[...] = m_new


def _softmax_update_t(s, vt, m_ref, l_ref, acc_ref):
    m_prev = m_ref[...]
    m_new = jnp.maximum(m_prev, jnp.max(s, axis=0, keepdims=True))
    alpha = jnp.exp(m_prev - m_new)
    p = jnp.exp(s - m_new)
    l_ref[...] = alpha * l_ref[...] + jnp.sum(p, axis=0, keepdims=True)
    acc_ref[...] = alpha * acc_ref[...] + _dot(vt, p.astype(BF16))
    m_ref[...] = m_new


def _attn_prompt_kernel(qt_ref, k_ref, vt_ref, kmeta_ref, vmeta_t_ref, wuvt_ref, o_ref, m_ref, l_ref, acc_ref, *, tq):
    qi = pl.program_id(1)
    qt = jnp.concatenate([qt_ref[h] for h in range(MLA_HEADS)], axis=1)

    s = _dot(kmeta_ref[...], qt)
    s = jnp.where(lax.broadcasted_iota(jnp.int32, s.shape, 0) < N_META, s, NEG_INF)
    m0 = jnp.max(s, axis=0, keepdims=True)
    p = jnp.exp(s - m0)
    m_ref[...] = m0
    l_ref[...] = jnp.sum(p, axis=0, keepdims=True)
    acc_ref[...] = _dot(vmeta_t_ref[...], p.astype(BF16))

    def body(kj, carry):
        start = pl.multiple_of(kj * tq, tq)
        _softmax_update_t(_dot(k_ref[0, pl.ds(start, tq), :], qt), vt_ref[:, pl.ds(start, tq)], m_ref, l_ref, acc_ref)
        return carry

    lax.fori_loop(0, qi, body, 0)

    start = pl.multiple_of(qi * tq, tq)
    s = _dot(k_ref[0, pl.ds(start, tq), :], qt)
    t_k = lax.broadcasted_iota(jnp.int32, s.shape, 0)
    t_q = lax.broadcasted_iota(jnp.int32, s.shape, 1) % tq
    s = jnp.where(t_k <= t_q, s, NEG_INF)
    _softmax_update_t(s, vt_ref[:, pl.ds(start, tq)], m_ref, l_ref, acc_ref)

    ot = (acc_ref[...] / l_ref[...]).astype(BF16)
    for h in range(MLA_HEADS):
        yt = _dot(wuvt_ref[h], ot[:, h * tq:(h + 1) * tq])
        o_ref[:, h * V_HEAD:(h + 1) * V_HEAD] = yt.T.astype(o_ref.dtype)


def _attn_prompt(qt, kcat, vt, kmeta, vmeta_t, w_uv_t, nb, tq):
    t = kcat.shape[1]
    nq = t // tq
    cols = MLA_HEADS * tq
    return pl.pallas_call(
        functools.partial(_attn_prompt_kernel, tq=tq),
        grid=(nb, nq),
        in_specs=[
            pl.BlockSpec((MLA_HEADS, QK_CAT, tq), lambda b, i: (0, 0, b * nq + i)),
            pl.BlockSpec((1, t, QK_CAT), lambda b, i: (b, 0, 0)),
            pl.BlockSpec((KV_RANK, t), lambda b, i: (0, b)),
            _const_spec(kmeta.shape),
            _const_spec(vmeta_t.shape),
            _const_spec(w_uv_t.shape),
        ],
        out_specs=pl.BlockSpec((tq, MLA_HEADS * V_HEAD), lambda b, i: (b * nq + i, 0)),
        out_shape=jax.ShapeDtypeStruct((nb * t, MLA_HEADS * V_HEAD), BF16),
        scratch_shapes=[pltpu.VMEM((1, cols), F32), pltpu.VMEM((1, cols), F32), pltpu.VMEM((KV_RANK, cols), F32)],
        compiler_params=pltpu.CompilerParams(dimension_semantics=("arbitrary", "arbitrary"),
                                             vmem_limit_bytes=VMEM_LIMIT),
        name="attn_prompt",
    )(qt, kcat, vt, kmeta, vmeta_t, w_uv_t)


def _attn_meta_kernel(q_ref, kmeta_ref, wuv_ref, o_ref):
    rows = MLA_HEADS * N_META
    q = q_ref[...].reshape(rows, QK_CAT)
    km = kmeta_ref[...]
    s = _dot_nt(q, km)
    t_q = lax.broadcasted_iota(jnp.int32, s.shape, 0) % N_META
    t_k = lax.broadcasted_iota(jnp.int32, s.shape, 1)
    s = jnp.where(t_k <= t_q, s, NEG_INF)
    p = jnp.exp(s - jnp.max(s, axis=1, keepdims=True))
    o = _dot(p.astype(BF16), km[:, :KV_RANK]) / jnp.sum(p, axis=1, keepdims=True)
    o = o.astype(BF16)
    for h in range(MLA_HEADS):
        o_ref[:, h * V_HEAD:(h + 1) * V_HEAD] = _dot(o[h * N_META:(h + 1) * N_META], wuv_ref[h]).astype(o_ref.dtype)


def _attn_meta(q, kmeta, w_uv):
    return pl.pallas_call(
        _attn_meta_kernel,
        out_shape=jax.ShapeDtypeStruct((N_META, MLA_HEADS * V_HEAD), BF16),
        name="attn_meta",
    )(q, kmeta, w_uv)


def _attn_sample_kernel(pt_ref, q_ref, knew_ref, wuv_ref, *rest, n_pages, steps):
    ckv_refs = rest[:n_pages]
    kr_refs = rest[n_pages:2 * n_pages]
    o_ref, m_ref, l_ref, acc_ref = rest[2 * n_pages:]
    g = pl.program_id(1)
    rows = MLA_HEADS * steps

    @pl.when(g == 0)
    def _():
        m_ref[...] = jnp.full(m_ref.shape, NEG_INF, F32)
        l_ref[...] = jnp.zeros(l_ref.shape, F32)
        acc_ref[...] = jnp.zeros(acc_ref.shape, F32)

    q = q_ref[...].reshape(rows, QK_CAT).astype(BF16)
    q_lat = q[:, :KV_RANK]
    q_rope = q[:, KV_RANK:KV_RANK + QK_ROPE]
    values, scores = [], []
    for j0 in range(0, n_pages, PAGE_GROUP):
        group = range(j0, j0 + PAGE_GROUP)
        ckv = jnp.concatenate([ckv_refs[j][...].astype(BF16) for j in group], axis=0)
        kr_t = jnp.concatenate([kr_refs[j][...].astype(BF16) for j in group], axis=1)
        scores.append(_dot_nt(q_lat, ckv) + _dot(q_rope, kr_t))
        values.append(ckv)
    s = jnp.concatenate(scores, axis=1)
    m_prev = m_ref[...]
    m_new = jnp.maximum(m_prev, jnp.max(s, axis=1, keepdims=True))
    alpha = jnp.exp(m_prev - m_new)
    p = jnp.exp(s - m_new)
    l_ref[...] = alpha * l_ref[...] + jnp.sum(p, axis=1, keepdims=True)
    p = p.astype(BF16)
    gk = PAGE_GROUP * PAGE_SIZE
    pv = _dot(p[:, :gk], values[0])
    for i in range(1, len(values)):
        pv += _dot(p[:, i * gk:(i + 1) * gk], values[i])
    acc_ref[...] = alpha * acc_ref[...] + pv
    m_ref[...] = m_new

    @pl.when(g == pl.num_programs(1) - 1)
    def _():
        kn = jnp.concatenate([knew_ref[...], jnp.zeros((PAGE_SIZE - steps, QK_CAT), F32)], axis=0).astype(BF16)
        s_new = _dot_nt(q, kn)
        t_q = lax.broadcasted_iota(jnp.int32, s_new.shape, 0) % steps
        t_k = lax.broadcasted_iota(jnp.int32, s_new.shape, 1)
        s_new = jnp.where(t_k <= t_q, s_new, NEG_INF)
        _softmax_update(s_new, kn[:, :KV_RANK], m_ref, l_ref, acc_ref)
        o = (acc_ref[...] / l_ref[...]).astype(BF16)
        for h in range(MLA_HEADS):
            o_ref[:, h * V_HEAD:(h + 1) * V_HEAD] = _dot(o[h * steps:(h + 1) * steps], wuv_ref[h])


def _attn_sample(page_table, q, knew, cache_ckv, cache_kr_t, w_uv, steps, n_pages):
    nb, pages_per_seq = page_table.shape
    assert pages_per_seq % n_pages == 0 and n_pages % PAGE_GROUP == 0
    ng = pages_per_seq // n_pages
    rows = MLA_HEADS * steps

    def page_spec(shape, j):
        return pl.BlockSpec((None, None) + shape, lambda b, g, pt: (0, pt[b, g * n_pages + j], 0, 0))

    const = lambda shape: pl.BlockSpec(shape, lambda b, g, pt: (0,) * len(shape))
    grid_spec = pltpu.PrefetchScalarGridSpec(
        num_scalar_prefetch=1,
        grid=(nb, ng),
        in_specs=[pl.BlockSpec((MLA_HEADS, steps, QK_CAT), lambda b, g, pt: (0, b, 0)),
                  pl.BlockSpec((steps, QK_CAT), lambda b, g, pt: (b, 0)),
                  const(w_uv.shape)]
                 + [page_spec((PAGE_SIZE, KV_RANK), j) for j in range(n_pages)]
                 + [page_spec((QK_ROPE, PAGE_SIZE), j) for j in range(n_pages)],
        out_specs=pl.BlockSpec((steps, MLA_HEADS * V_HEAD), lambda b, g, pt: (b, 0)),
        scratch_shapes=[pltpu.VMEM((rows, 1), F32), pltpu.VMEM((rows, 1), F32), pltpu.VMEM((rows, KV_RANK), F32)],
    )
    return pl.pallas_call(
        functools.partial(_attn_sample_kernel, n_pages=n_pages, steps=steps),
        grid_spec=grid_spec,
        out_shape=jax.ShapeDtypeStruct((nb * steps, MLA_HEADS * V_HEAD), F32),
        compiler_params=pltpu.CompilerParams(dimension_semantics=("arbitrary", "arbitrary"),
                                             vmem_limit_bytes=VMEM_LIMIT),
        name="attn_sample",
    )(page_table, q, knew, w_uv, *([cache_ckv] * n_pages), *([cache_kr_t] * n_pages))


def _gelu_tanh(x):
    return 0.5 * x * (1.0 + jnp.tanh(math.sqrt(2.0 / math.pi) * (x + 0.044715 * (x * x * x))))


def _sigmoid(x):
    return 1.0 / (1.0 + jnp.exp(-x))


def _post_kernel(x_ref, yssm_ref, u_ref, ymla_ref, p1_ref, p2_ref, d_ref, wglu_ref, wout_ref, gffn_ref,
                 wgate_ref, wup_ref, convw_ref, convb_ref, wdown_ref, gfin_ref,
                 y_ref, gate_ref, hbuf, carry_ref, *, tm, seq_steps):
    long_seq = seq_steps == 0
    y = yssm_ref[...] + d_ref[...] * u_ref[...]
    y = _gelu_tanh(y)
    y = y * _sigmoid(_dot(y.astype(BF16), wglu_ref[...]))
    mixed = _dot(y.astype(BF16), wout_ref[:S5_W, :]) + _dot(ymla_ref[...].astype(BF16), wout_ref[S5_W:, :])
    x1 = x_ref[...] + mixed
    xn = _rms(x1, gffn_ref[...]).astype(BF16)

    row = lax.broadcasted_iota(jnp.int32, (tm, FF_CHUNK), 0)
    if long_seq:
        @pl.when(pl.program_id(1) == 0)
        def _():
            carry_ref[...] = p1_ref[...]
        t_in = row
    else:
        t_in = row % seq_steps

    for c in range(D_FF // FF_CHUNK):
        sl = slice(c * FF_CHUNK, (c + 1) * FF_CHUNK)
        gate = _dot(xn, wgate_ref[:, sl])
        up = _dot(xn, wup_ref[:, sl])
        if long_seq:
            back1 = jnp.broadcast_to(carry_ref[7:8, sl], gate.shape)
            back2 = jnp.where(t_in == 0, carry_ref[6:7, sl], back1)
            carry_ref[:, sl] = gate[tm - 8:, :]
            gate_ref[:, sl] = gate[tm - 8:, :]
        else:
            back1 = p1_ref[:, sl]
            back2 = p2_ref[:, sl]
            gate_ref[:, sl] = gate
        prev1 = jnp.where(t_in >= 1, pltpu.roll(gate, 1, 0), back1)
        prev2 = jnp.where(t_in >= 2, pltpu.roll(gate, 2, 0), back2)
        conv = (convb_ref[:, sl] + convw_ref[0:1, sl] * prev2 + convw_ref[1:2, sl] * prev1
                + convw_ref[2:3, sl] * gate)
        hbuf[:, sl] = (conv * _sigmoid(conv) * up).astype(BF16)

    x2 = x1 + _dot(hbuf[...], wdown_ref[...])
    y_ref[...] = _rms(x2, gfin_ref[...])


def _post_call(x2d, y_ssm, u, y_mla, p1, p2, w, nb, tm, seq_steps):
    rows = x2d.shape[0]
    nt = rows // (nb * tm)
    assert nb * nt * tm == rows
    long_seq = seq_steps == 0
    row = lambda width: pl.BlockSpec((tm, width), lambda b, i: (b * nt + i, 0))
    if long_seq:
        hist = [_const_spec((8, D_FF)), _const_spec((8, D_FF))]
        gate_spec = pl.BlockSpec((None, 8, D_FF), lambda b, i: (b, 0, 0))
        gate_shape = jax.ShapeDtypeStruct((nb, 8, D_FF), F32)
    else:
        hist = [row(D_FF), row(D_FF)]
        gate_spec = row(D_FF)
        gate_shape = jax.ShapeDtypeStruct((rows, D_FF), F32)
    return pl.pallas_call(
        functools.partial(_post_kernel, tm=tm, seq_steps=seq_steps),
        grid=(nb, nt),
        in_specs=[row(D_MODEL), row(S5_W), row(S5_W), row(MLA_HEADS * V_HEAD)] + hist + [
            _const_spec((1, S5_W)), _const_spec((S5_W, S5_W)), _const_spec((D_MODEL, D_MODEL)),
            _const_spec((1, D_MODEL)), _const_spec((D_MODEL, D_FF)), _const_spec((D_MODEL, D_FF)),
            _const_spec((CONV_W, D_FF)), _const_spec((1, D_FF)), _const_spec((D_FF, D_MODEL)),
            _const_spec((1, D_MODEL))],
        out_specs=[row(D_MODEL), gate_spec],
        out_shape=[jax.ShapeDtypeStruct((rows, D_MODEL), F32), gate_shape],
        scratch_shapes=[pltpu.VMEM((tm, D_FF), BF16), pltpu.VMEM((8, D_FF), F32)],
        compiler_params=pltpu.CompilerParams(dimension_semantics=("arbitrary", "arbitrary"),
                                             vmem_limit_bytes=VMEM_LIMIT),
        name="post",
    )(x2d, y_ssm, u, y_mla, p1, p2, w["s5_d"], w["w_glu"], w["w_out"], w["g_ffn"], w["w_gate"], w["w_up"],
      w["conv_w"], w["conv_b"], w["w_down"], w["g_final"])


def _rope_tables(pos):
    half = QK_ROPE // 2
    inv = ROPE_BASE ** (-jnp.arange(half, dtype=F32) / half)
    ang = pos.astype(F32)[:, None] * inv[None, :]
    cos, sin = jnp.cos(ang), jnp.sin(ang)
    pad = jnp.zeros((pos.shape[0], ROPE_PAD - QK_ROPE), F32)
    return jnp.concatenate([cos, cos, pad], axis=1), jnp.concatenate([-sin, sin, pad], axis=1)


def _to_groups(u, nb, steps_per_row):
    rows = u.shape[0] // steps_per_row
    u = u.reshape(rows, steps_per_row, S5_GROUPS, S5_GROUP).transpose(2, 0, 1, 3)
    return u.reshape(S5_GROUPS, rows, steps_per_row * S5_GROUP).astype(BF16)


def _from_groups(y, steps_per_row):
    g, rows, _ = y.shape
    return y.reshape(g, rows, steps_per_row, S5_GROUP).transpose(1, 2, 0, 3).reshape(rows * steps_per_row, S5_W)


def kernel(x_prompt, x_sample, cache_ckv, cache_kr, state_s5_re, state_s5_im, state_conv, page_table, meta_tokens, g_mix, w_in, g_q, w_uq, g_kv, w_uk, w_uv, s5_a_re, s5_a_im, s5_log_dt, s5_b_re, s5_b_im, s5_c_re, s5_c_im, s5_d, w_glu, w_out, g_ffn, w_gate, w_up, conv_w, conv_b, w_down, g_final):
    assert w_in.shape[0] == 1, "single-layer step"
    nb, seq, _ = x_prompt.shape
    db, steps, _ = x_sample.shape
    n_past = page_table.shape[1] * PAGE_SIZE
    assert seq % S5_CHUNK == 0 and N_META == S5_CHUNK and steps * 2 == S5_CHUNK

    w_in_pad = jnp.concatenate([w_in[0], jnp.zeros((D_MODEL, IN_PAD - w_in.shape[2]), F32)], axis=1)
    wq = w_uq[0].reshape(Q_RANK, MLA_HEADS, QK_NOPE + QK_ROPE)
    wq_rope = jnp.concatenate([wq[:, :, QK_NOPE:], jnp.zeros((Q_RANK, MLA_HEADS, ROPE_PAD - QK_ROPE), F32)], axis=2)
    wq_perm = jnp.concatenate([wq[:, :, :QK_NOPE].reshape(Q_RANK, -1), wq_rope.reshape(Q_RANK, -1)], axis=1)
    w = {
        "g_mix": g_mix[0].reshape(1, -1), "w_in": w_in_pad.astype(BF16),
        "g_q": g_q[0].reshape(1, -1), "w_uq": wq_perm.astype(BF16), "w_uq_t": wq_perm.T.astype(BF16),
        "w_uk_t": w_uk[0].transpose(1, 2, 0).astype(BF16),
        "w_uk": w_uk[0].transpose(1, 0, 2).astype(BF16),
        "g_kv": g_kv[0].reshape(1, -1),
        "s5_d": s5_d[0].reshape(1, -1), "w_glu": w_glu[0].astype(BF16), "w_out": w_out[0].astype(BF16),
        "g_ffn": g_ffn[0].reshape(1, -1), "w_gate": w_gate[0].astype(BF16), "w_up": w_up[0].astype(BF16),
        "conv_w": conv_w[0], "conv_b": conv_b[0].reshape(1, -1), "w_down": w_down[0].astype(BF16),
        "g_final": g_final.reshape(1, -1),
    }
    w_uv_h = w_uv[0].transpose(1, 0, 2).astype(BF16)
    s5_args = (s5_a_re[0], s5_a_im[0], s5_log_dt[0], s5_b_re[0], s5_b_im[0], s5_c_re[0], s5_c_im[0])
    ops16 = _s5_operators(*s5_args, steps=S5_CHUNK)
    ops8 = _s5_operators(*s5_args, steps=steps)

    cos_m, sin_m = _rope_tables(jnp.arange(N_META, dtype=jnp.int32))
    cos_p, sin_p = _rope_tables(N_META + jnp.arange(seq, dtype=jnp.int32))
    tm_p = 512
    tm_s = 512
    cos_s, sin_s = _rope_tables(n_past + jnp.arange(tm_s, dtype=jnp.int32) % steps)

    tiles_per_seq = seq // tm_p
    u_m, q_m, kcat_m, ckv_m, kr_m = _pre_call(meta_tokens, cos_m, sin_m, lambda i: (0, 0), N_META, BF16, w)
    u_p, qt_p, kcat_p, ckv_p, kr_p, vt_p = _pre_call(x_prompt.reshape(nb * seq, D_MODEL), cos_p, sin_p,
                                                     lambda i: (i % tiles_per_seq, 0), tm_p, BF16, w,
                                                     q_transposed=True)
    u_s, q_s, kcat_s, ckv_s, kr_s = _pre_call(x_sample.reshape(db * steps, D_MODEL), cos_s, sin_s,
                                              lambda i: (0, 0), tm_s, F32, w)

    u_full = jnp.concatenate([jnp.broadcast_to(u_m[None], (nb, N_META, S5_W)), u_p.reshape(nb, seq, S5_W)], axis=1)
    y_g, s5_re_p, s5_im_p = _s5_prompt(_to_groups(u_full.reshape(-1, S5_W), nb, S5_CHUNK), ops16, nb)
    yssm_full = _from_groups(y_g, S5_CHUNK).reshape(nb, N_META + seq, S5_W)
    yssm_m = yssm_full[0, :N_META]
    yssm_p = yssm_full[:, N_META:].reshape(nb * seq, S5_W)
    y_gs, s5_re_s, s5_im_s = _s5_sample(_to_groups(u_s, db, steps), state_s5_re[0].reshape(db, -1),
                                        state_s5_im[0].reshape(db, -1), ops8)
    yssm_s = _from_groups(y_gs, steps)

    kmeta = jnp.concatenate([kcat_m, jnp.zeros((PAGE_SIZE - N_META, QK_CAT), BF16)], axis=0)
    ymla_m = _attn_meta(q_m, kmeta, w_uv_h)
    ymla_p = _attn_prompt(qt_p, kcat_p.reshape(nb, seq, QK_CAT), vt_p, kmeta, kmeta[:, :KV_RANK].T,
                          w_uv[0].transpose(1, 2, 0).astype(BF16), nb, tq=256)
    ymla_s = _attn_sample(page_table, q_s, kcat_s, cache_ckv, jnp.swapaxes(cache_kr, 2, 3), w_uv_h, steps,
                          n_pages=16)

    zeros_hist = jnp.zeros((8, D_FF), F32)
    _, gate_m = _post_call(meta_tokens, yssm_m, u_m, ymla_m, zeros_hist, zeros_hist, w, 1, N_META, 0)
    y_p, gate_p = _post_call(x_prompt.reshape(nb * seq, D_MODEL), yssm_p, u_p, ymla_p, gate_m[0], gate_m[0],
                             w, nb, tm_p, 0)
    sc = state_conv[0]
    back1 = jnp.pad(sc[:, 1:2], ((0, 0), (0, steps - 1), (0, 0))).reshape(db * steps, D_FF)
    back2 = jnp.pad(sc, ((0, 0), (0, steps - 2), (0, 0))).reshape(db * steps, D_FF)
    y_s, gate_s = _post_call(x_sample.reshape(db * steps, D_MODEL), yssm_s, u_s, ymla_s, back1, back2,
                             w, 1, tm_s // 2, steps)

    def with_meta(meta_rows, tok_rows, width):
        return jnp.concatenate([jnp.broadcast_to(meta_rows[None], (nb, N_META, width)),
                                tok_rows.reshape(nb, seq, width)], axis=1)[None]

    return (y_p.reshape(nb, seq, D_MODEL), y_s.reshape(db, steps, D_MODEL),
            with_meta(ckv_m, ckv_p, KV_RANK), with_meta(kr_m, kr_p, QK_ROPE),
            s5_re_p.reshape(1, nb, S5_GROUPS, S5_STATE), s5_im_p.reshape(1, nb, S5_GROUPS, S5_STATE),
            gate_p[:, 8 - (CONV_W - 1):][None],
            ckv_s.reshape(1, db, steps, KV_RANK), kr_s.reshape(1, db, steps, QK_ROPE),
            s5_re_s.reshape(1, db, S5_GROUPS, S5_STATE), s5_im_s.reshape(1, db, S5_GROUPS, S5_STATE),
            gate_s.reshape(db, steps, D_FF)[:, steps - (CONV_W - 1):][None])
```

```python
import functools
import math

import jax
import jax.numpy as jnp
from jax import lax
from jax.experimental import pallas as pl
from jax.experimental.pallas import tpu as pltpu

F32 = jnp.float32
BF16 = jnp.bfloat16

D_MODEL = 1024
N_META = 16
S5_W = 512
S5_GROUP = 16
S5_GROUPS = 32
S5_STATE = 64
MLA_HEADS = 4
QK_NOPE = 128
QK_ROPE = 64
V_HEAD = 128
Q_RANK = 384
KV_RANK = 256
D_FF = 2816
CONV_W = 3
ROPE_BASE = 10000.0
EPS = 1e-6
PAGE_SIZE = 128
ATTN_SCALE = 1.0 / math.sqrt(QK_NOPE + QK_ROPE)

LANE = 128
ROPE_PAD = LANE
QK_CAT = KV_RANK + ROPE_PAD
IN_PAD = S5_W + Q_RANK + KV_RANK + ROPE_PAD
S5_CHUNK = 8
S5_OCT = LANE // S5_GROUP
S5_NOCT = S5_GROUPS // S5_OCT
S5_STATE_W = S5_GROUPS * S5_STATE
S5_SCAN_W = 256
FF_CHUNK = 256
PAGE_GROUP = 4
VMEM_LIMIT = 56 * 1024 * 1024
NEG_INF = float("-inf")
HI = lax.Precision.HIGHEST


def _const_spec(shape):
    nd = len(shape)
    return pl.BlockSpec(shape, lambda *_: (0,) * nd, pipeline_mode=pl.Buffered(1))


def _params(*semantics):
    return pltpu.CompilerParams(dimension_semantics=semantics, vmem_limit_bytes=VMEM_LIMIT)


def _rms(x, g):
    return x * lax.rsqrt(jnp.mean(x * x, axis=-1, keepdims=True) + EPS) * g


def _rope_slab(x, cos, sin):
    lane = lax.broadcasted_iota(jnp.int32, x.shape, 1)
    half = QK_ROPE // 2
    swapped = jnp.where(lane < half, pltpu.roll(x, LANE - half, 1), pltpu.roll(x, half, 1))
    return x * cos + swapped * sin


def _dot(a, b):
    return jnp.dot(a, b, preferred_element_type=F32)


def _dot_nt(a, b):
    return lax.dot_general(a, b, (((1,), (1,)), ((), ())), preferred_element_type=F32)


def _pre_kernel(x_ref, cos_ref, sin_ref, cost_ref, sint_ref, gmix_ref, win_ref, gq_ref, wuq_ref, wuk_ref, gkv_ref,
                u_ref, ut_ref, q_ref, kcat_ref, ckv_ref, kr_ref, *rest, q_transposed):
    x = x_ref[...]
    xn = _rms(x, gmix_ref[...]).astype(BF16)
    z = _dot(xn, win_ref[...])
    u_ref[...] = z[:, :S5_W]
    slab_ref = rest[-1]
    chunks = u_ref.shape[0] // S5_CHUNK
    for o in range(S5_NOCT):
        slab_ref[o] = z[:, o * LANE:(o + 1) * LANE]
        for s in range(S5_CHUNK):
            ut_ref[o, s] = slab_ref[o, pl.ds(s, chunks, stride=S5_CHUNK), :].astype(BF16)
    cq = z[:, S5_W:S5_W + Q_RANK]
    ckv_raw = z[:, S5_W + Q_RANK:S5_W + Q_RANK + KV_RANK]
    kr_raw = z[:, S5_W + Q_RANK + KV_RANK:]
    cqn = _rms(cq, gq_ref[...]).astype(BF16)
    nope_w = MLA_HEADS * QK_NOPE
    half = QK_ROPE // 2
    if q_transposed:
        qt = _dot_nt(wuq_ref[...], cqn)
        cost = cost_ref[...]
        sint = sint_ref[...]
        for h in range(MLA_HEADS):
            qn = qt[h * QK_NOPE:(h + 1) * QK_NOPE].astype(BF16)
            q_lat = _dot(wuk_ref[h], qn) * ATTN_SCALE
            xr = qt[nope_w + h * ROPE_PAD:nope_w + (h + 1) * ROPE_PAD]
            swapped = jnp.concatenate([xr[half:QK_ROPE], xr[:half], xr[QK_ROPE:]], axis=0)
            qr = (xr * cost + swapped * sint) * ATTN_SCALE
            q_ref[h, :KV_RANK, :] = q_lat.astype(q_ref.dtype)
            q_ref[h, KV_RANK:, :] = qr.astype(q_ref.dtype)
    else:
        q = _dot(cqn, wuq_ref[...])
        for h in range(MLA_HEADS):
            qn = q[:, h * QK_NOPE:(h + 1) * QK_NOPE].astype(BF16)
            q_lat = _dot(qn, wuk_ref[h]) * ATTN_SCALE
            qr = _rope_slab(q[:, nope_w + h * ROPE_PAD:nope_w + (h + 1) * ROPE_PAD],
                            cos_ref[...], sin_ref[...]) * ATTN_SCALE
            q_ref[h, :, :KV_RANK] = q_lat.astype(q_ref.dtype)
            q_ref[h, :, KV_RANK:] = qr.astype(q_ref.dtype)
    ckv = _rms(ckv_raw, gkv_ref[...])
    kr = _rope_slab(kr_raw, cos_ref[...], sin_ref[...])
    ckv_ref[...] = ckv
    kr_ref[...] = kr[:, :QK_ROPE]
    kcat_ref[:, :KV_RANK] = ckv.astype(kcat_ref.dtype)
    kcat_ref[:, KV_RANK:] = kr.astype(kcat_ref.dtype)
    if q_transposed:
        rest[0][...] = ckv.T.astype(BF16)


def _pre_call(x2d, cos, sin, tab_map, tm, qdtype, w, q_transposed=False):
    rows = x2d.shape[0]
    assert rows % tm == 0 and tm % S5_CHUNK == 0
    row = lambda width: pl.BlockSpec((tm, width), lambda i: (i, 0))
    tab_map_t = lambda i: tab_map(i)[::-1]
    if q_transposed:
        wuq, wuk = w["w_uq_t"], w["w_uk"]
        q_spec = pl.BlockSpec((MLA_HEADS, QK_CAT, tm), lambda i: (0, 0, i))
        q_shape = jax.ShapeDtypeStruct((MLA_HEADS, QK_CAT, rows), qdtype)
        extra_specs = [pl.BlockSpec((KV_RANK, tm), lambda i: (0, i))]
        extra_shapes = [jax.ShapeDtypeStruct((KV_RANK, rows), BF16)]
    else:
        wuq, wuk = w["w_uq"], w["w_uk_t"]
        q_spec = pl.BlockSpec((MLA_HEADS, tm, QK_CAT), lambda i: (0, i, 0))
        q_shape = jax.ShapeDtypeStruct((MLA_HEADS, rows, QK_CAT), qdtype)
        extra_specs, extra_shapes = [], []
    return pl.pallas_call(
        functools.partial(_pre_kernel, q_transposed=q_transposed),
        grid=(rows // tm,),
        in_specs=[
            row(D_MODEL),
            pl.BlockSpec((tm, ROPE_PAD), tab_map),
            pl.BlockSpec((tm, ROPE_PAD), tab_map),
            pl.BlockSpec((ROPE_PAD, tm), tab_map_t),
            pl.BlockSpec((ROPE_PAD, tm), tab_map_t),
            _const_spec((1, D_MODEL)),
            _const_spec((D_MODEL, IN_PAD)),
            _const_spec((1, Q_RANK)),
            _const_spec(wuq.shape),
            _const_spec(wuk.shape),
            _const_spec((1, KV_RANK)),
        ],
        out_specs=[row(S5_W), pl.BlockSpec((S5_NOCT, S5_CHUNK, tm // S5_CHUNK, LANE), lambda i: (0, 0, i, 0)),
                   q_spec, row(QK_CAT), row(KV_RANK), row(QK_ROPE)] + extra_specs,
        out_shape=[
            jax.ShapeDtypeStruct((rows, S5_W), F32),
            jax.ShapeDtypeStruct((S5_NOCT, S5_CHUNK, rows // S5_CHUNK, LANE), BF16),
            q_shape,
            jax.ShapeDtypeStruct((rows, QK_CAT), qdtype),
            jax.ShapeDtypeStruct((rows, KV_RANK), F32),
            jax.ShapeDtypeStruct((rows, QK_ROPE), F32),
        ] + extra_shapes,
        scratch_shapes=[pltpu.VMEM((S5_NOCT, tm, LANE), F32)],
        compiler_params=_params("arbitrary"),
        name="pre",
    )(x2d, cos, sin, cos.T, sin.T, w["g_mix"], w["w_in"], w["g_q"], wuq, wuk, w["g_kv"])


def _chunk_inputs(ut_ref):
    return jnp.concatenate([ut_ref[s] for s in range(S5_CHUNK)], axis=1)


def _s5_state_in_kernel(ut_ref, wre_ref, wim_ref, sre_ref, sim_ref):
    u = _chunk_inputs(ut_ref)
    sre_ref[...] = _dot(u, wre_ref[0])
    sim_ref[...] = _dot(u, wim_ref[0])


def _s5_scan_kernel(sre_ref, sim_ref, h0re_ref, h0im_ref, are_ref, aim_ref, hre_ref, him_ref, fre_ref, fim_ref):
    nb, nc, width = sre_ref.shape
    ar = are_ref[...].reshape(1, 1, width)
    ai = aim_ref[...].reshape(1, 1, width)

    def body(c, carry):
        hr, hi = carry
        hre_ref[:, pl.ds(c, 1), :] = hr
        him_ref[:, pl.ds(c, 1), :] = hi
        sr = sre_ref[:, pl.ds(c, 1), :]
        si = sim_ref[:, pl.ds(c, 1), :]
        return ar * hr - ai * hi + sr, ar * hi + ai * hr + si

    hr, hi = lax.fori_loop(0, nc, body, (h0re_ref[...], h0im_ref[...]))
    fre_ref[...] = hr
    fim_ref[...] = hi


def _s5_out_kernel(ut_ref, hre_ref, him_ref, t_ref, mre_ref, mim_ref, yt_ref):
    y = (_dot(_chunk_inputs(ut_ref), t_ref[0]) + _dot(hre_ref[...].astype(BF16), mre_ref[0])
         + _dot(him_ref[...].astype(BF16), mim_ref[0]))
    for t in range(S5_CHUNK):
        yt_ref[t] = y[:, t * LANE:(t + 1) * LANE]


def _s5_mixer(ut, h0_re, h0_im, ops, nb):
    rows = ut.shape[2]
    nc = rows // nb
    tr = min(rows, 1024)
    assert rows % tr == 0
    ow = S5_OCT * S5_STATE
    cw = S5_CHUNK * LANE
    ut_spec = pl.BlockSpec((None, S5_CHUNK, tr, LANE), lambda q, r: (q, 0, r, 0))
    st_spec = pl.BlockSpec((tr, ow), lambda q, r: (r, q))
    oct_spec = lambda a, b: pl.BlockSpec((1, a, b), lambda q, r: (q, 0, 0))
    s_re, s_im = pl.pallas_call(
        _s5_state_in_kernel,
        grid=(S5_NOCT, rows // tr),
        in_specs=[ut_spec, oct_spec(cw, ow), oct_spec(cw, ow)],
        out_specs=[st_spec, st_spec],
        out_shape=[jax.ShapeDtypeStruct((rows, S5_STATE_W), F32)] * 2,
        compiler_params=_params("arbitrary", "arbitrary"),
        name="s5_state_in",
    )(ut, ops["w_re"], ops["w_im"])

    blk = pl.BlockSpec((nb, nc, S5_SCAN_W), lambda j: (0, 0, j))
    one = pl.BlockSpec((nb, 1, S5_SCAN_W), lambda j: (0, 0, j))
    vec = pl.BlockSpec((1, S5_SCAN_W), lambda j: (0, j))
    h_re, h_im, f_re, f_im = pl.pallas_call(
        _s5_scan_kernel,
        grid=(S5_STATE_W // S5_SCAN_W,),
        in_specs=[blk, blk, one, one, vec, vec],
        out_specs=[blk, blk, one, one],
        out_shape=[jax.ShapeDtypeStruct((nb, nc, S5_STATE_W), F32)] * 2
                  + [jax.ShapeDtypeStruct((nb, 1, S5_STATE_W), F32)] * 2,
        compiler_params=_params("arbitrary"),
        name="s5_scan",
    )(s_re.reshape(nb, nc, S5_STATE_W), s_im.reshape(nb, nc, S5_STATE_W), h0_re, h0_im, ops["a_re"], ops["a_im"])

    yt = pl.pallas_call(
        _s5_out_kernel,
        grid=(S5_NOCT, rows // tr),
        in_specs=[ut_spec, st_spec, st_spec, oct_spec(cw, cw), oct_spec(ow, cw), oct_spec(ow, cw)],
        out_specs=ut_spec,
        out_shape=jax.ShapeDtypeStruct(ut.shape, F32),
        compiler_params=_params("arbitrary", "arbitrary"),
        name="s5_out",
    )(ut, h_re.reshape(rows, S5_STATE_W), h_im.reshape(rows, S5_STATE_W), ops["t"], ops["m_re"], ops["m_im"])
    return yt, f_re, f_im


def _s5_operators(a_re, a_im, log_dt, b_re, b_im, c_re, c_im):
    steps = S5_CHUNK
    dt = jnp.exp(log_dt)[:, None]
    k = jnp.arange(steps + 1, dtype=F32)[:, None, None]
    mag = jnp.exp(k * (dt * a_re)[None])
    pw_r = mag * jnp.cos(k * (dt * a_im)[None])
    pw_i = mag * jnp.sin(k * (dt * a_im)[None])
    abr, abi = pw_r[1], pw_i[1]
    num_re, num_im = abr - 1.0, abi
    den = a_re * a_re + a_im * a_im
    f_re = (num_re * a_re + num_im * a_im) / den
    f_im = (num_im * a_re - num_re * a_im) / den
    bbr = f_re[..., None] * b_re - f_im[..., None] * b_im
    bbi = f_re[..., None] * b_im + f_im[..., None] * b_re
    cp_r = c_re[None] * pw_r[:, :, None, :] - c_im[None] * pw_i[:, :, None, :]
    cp_i = c_re[None] * pw_i[:, :, None, :] + c_im[None] * pw_r[:, :, None, :]
    kern = (jnp.einsum("kghp,gpj->kghj", cp_r[:steps], bbr, precision=HI)
            - jnp.einsum("kghp,gpj->kghj", cp_i[:steps], bbi, precision=HI))
    lag = jnp.arange(steps)[None, :] - jnp.arange(steps)[:, None]
    toe = jnp.where((lag >= 0)[:, :, None, None, None], kern[jnp.clip(lag, 0, steps - 1)], 0.0)
    k_rev = steps - 1.0 - k[:steps]
    rev_mag = jnp.exp(k_rev * (dt * a_re)[None])
    rev_r, rev_i = rev_mag * jnp.cos(k_rev * (dt * a_im)[None]), rev_mag * jnp.sin(k_rev * (dt * a_im)[None])
    w_r = rev_r[..., None] * bbr[None] - rev_i[..., None] * bbi[None]
    w_i = rev_r[..., None] * bbi[None] + rev_i[..., None] * bbr[None]

    octs = lambda a, axis: a.reshape(a.shape[:axis] + (S5_NOCT, S5_OCT) + a.shape[axis + 1:])
    same_group = jnp.arange(S5_OCT)[:, None] == jnp.arange(S5_OCT)[None, :]
    cw, ow = steps * LANE, S5_OCT * S5_STATE

    def block_diag(a, perm, g_axis, k_axis):
        a = jnp.expand_dims(a.transpose(perm), k_axis)
        shape = [1] * a.ndim
        shape[g_axis], shape[k_axis] = S5_OCT, S5_OCT
        return jnp.where(same_group.reshape(shape), a, 0.0)

    t_mat = block_diag(octs(toe, 2), (2, 0, 3, 5, 1, 4), 2, 5).reshape(S5_NOCT, cw, cw)
    w_re = block_diag(octs(w_r, 1), (1, 0, 2, 4, 3), 2, 4).reshape(S5_NOCT, cw, ow)
    w_im = block_diag(octs(w_i, 1), (1, 0, 2, 4, 3), 2, 4).reshape(S5_NOCT, cw, ow)
    m_re = block_diag(octs(cp_r[1:], 1), (1, 2, 4, 0, 3), 1, 4).reshape(S5_NOCT, ow, cw)
    m_im = block_diag(octs(-cp_i[1:], 1), (1, 2, 4, 0, 3), 1, 4).reshape(S5_NOCT, ow, cw)
    return {
        "t": t_mat.astype(BF16), "w_re": w_re.astype(BF16), "w_im": w_im.astype(BF16),
        "m_re": m_re.astype(BF16), "m_im": m_im.astype(BF16),
        "a_re": pw_r[steps].reshape(1, S5_STATE_W), "a_im": pw_i[steps].reshape(1, S5_STATE_W),
    }


def _softmax_update_t(s, vt, m_ref, l_ref, acc_ref):
    m_prev = m_ref[...]
    m_new = jnp.maximum(m_prev, jnp.max(s, axis=0, keepdims=True))
    alpha = jnp.exp(m_prev - m_new)
    p = jnp.exp(s - m_new)
    l_ref[...] = alpha * l_ref[...] + jnp.sum(p, axis=0, keepdims=True)
    acc_ref[...] = alpha * acc_ref[...] + _dot(vt, p.astype(BF16))
    m_ref[...] = m_new


def _attn_prompt_kernel(qt_ref, k_ref, vt_ref, kmeta_ref, vmeta_t_ref, wuvt_ref, o_ref, m_ref, l_ref, acc_ref, *, tq):
    qi = pl.program_id(1)
    qt = jnp.concatenate([qt_ref[h] for h in range(MLA_HEADS)], axis=1)

    s = _dot(kmeta_ref[...], qt)
    s = jnp.where(lax.broadcasted_iota(jnp.int32, s.shape, 0) < N_META, s, NEG_INF)
    m0 = jnp.max(s, axis=0, keepdims=True)
    p = jnp.exp(s - m0)
    m_ref[...] = m0
    l_ref[...] = jnp.sum(p, axis=0, keepdims=True)
    acc_ref[...] = _dot(vmeta_t_ref[...], p.astype(BF16))

    def body(kj, carry):
        start = pl.multiple_of(kj * tq, tq)
        _softmax_update_t(_dot(k_ref[0, pl.ds(start, tq), :], qt), vt_ref[:, pl.ds(start, tq)], m_ref, l_ref, acc_ref)
        return carry

    lax.fori_loop(0, qi, body, 0)

    start = pl.multiple_of(qi * tq, tq)
    s = _dot(k_ref[0, pl.ds(start, tq), :], qt)
    t_k = lax.broadcasted_iota(jnp.int32, s.shape, 0)
    t_q = lax.broadcasted_iota(jnp.int32, s.shape, 1) % tq
    s = jnp.where(t_k <= t_q, s, NEG_INF)
    _softmax_update_t(s, vt_ref[:, pl.ds(start, tq)], m_ref, l_ref, acc_ref)

    ot = (acc_ref[...] / l_ref[...]).astype(BF16)
    for h in range(MLA_HEADS):
        yt = _dot(wuvt_ref[h], ot[:, h * tq:(h + 1) * tq])
        o_ref[:, h * V_HEAD:(h + 1) * V_HEAD] = yt.T.astype(o_ref.dtype)


def _attn_prompt(qt, kcat, vt, kmeta, vmeta_t, w_uv_t, nb, tq):
    t = kcat.shape[1]
    nq = t // tq
    cols = MLA_HEADS * tq
    return pl.pallas_call(
        functools.partial(_attn_prompt_kernel, tq=tq),
        grid=(nb, nq),
        in_specs=[
            pl.BlockSpec((MLA_HEADS, QK_CAT, tq), lambda b, i: (0, 0, b * nq + i)),
            pl.BlockSpec((1, t, QK_CAT), lambda b, i: (b, 0, 0)),
            pl.BlockSpec((KV_RANK, t), lambda b, i: (0, b)),
            _const_spec(kmeta.shape),
            _const_spec(vmeta_t.shape),
            _const_spec(w_uv_t.shape),
        ],
        out_specs=pl.BlockSpec((tq, MLA_HEADS * V_HEAD), lambda b, i: (b * nq + i, 0)),
        out_shape=jax.ShapeDtypeStruct((nb * t, MLA_HEADS * V_HEAD), BF16),
        scratch_shapes=[pltpu.VMEM((1, cols), F32), pltpu.VMEM((1, cols), F32), pltpu.VMEM((KV_RANK, cols), F32)],
        compiler_params=_params("arbitrary", "arbitrary"),
        name="attn_prompt",
    )(qt, kcat, vt, kmeta, vmeta_t, w_uv_t)


def _attn_meta_kernel(q_ref, kmeta_ref, wuv_ref, o_ref):
    rows = MLA_HEADS * N_META
    q = q_ref[...].reshape(rows, QK_CAT)
    km = kmeta_ref[...]
    s = _dot_nt(q, km)
    t_q = lax.broadcasted_iota(jnp.int32, s.shape, 0) % N_META
    t_k = lax.broadcasted_iota(jnp.int32, s.shape, 1)
    s = jnp.where(t_k <= t_q, s, NEG_INF)
    p = jnp.exp(s - jnp.max(s, axis=1, keepdims=True))
    o = _dot(p.astype(BF16), km[:, :KV_RANK]) / jnp.sum(p, axis=1, keepdims=True)
    o = o.astype(BF16)
    for h in range(MLA_HEADS):
        o_ref[:, h * V_HEAD:(h + 1) * V_HEAD] = _dot(o[h * N_META:(h + 1) * N_META], wuv_ref[h]).astype(o_ref.dtype)


def _attn_meta(q, kmeta, w_uv):
    return pl.pallas_call(
        _attn_meta_kernel,
        out_shape=jax.ShapeDtypeStruct((N_META, MLA_HEADS * V_HEAD), BF16),
        name="attn_meta",
    )(q, kmeta, w_uv)


def _attn_sample_kernel(pt_ref, q_ref, knew_ref, wuv_ref, *rest, n_pages, steps):
    ckv_refs = rest[:n_pages]
    kr_refs = rest[n_pages:2 * n_pages]
    o_ref = rest[2 * n_pages]
    rows = MLA_HEADS * steps
    q = q_ref[...].reshape(rows, QK_CAT).astype(BF16)
    q_lat = q[:, :KV_RANK]
    q_rope = q[:, KV_RANK:KV_RANK + QK_ROPE]
    groups = range(0, n_pages, PAGE_GROUP)

    def latent_keys(j0):
        return jnp.concatenate([ckv_refs[j][...].astype(BF16) for j in range(j0, j0 + PAGE_GROUP)], axis=0)

    scores = []
    for j0 in groups:
        kr_t = jnp.concatenate([kr_refs[j][...].astype(BF16) for j in range(j0, j0 + PAGE_GROUP)], axis=1)
        scores.append(_dot_nt(q_lat, latent_keys(j0)) + _dot(q_rope, kr_t))
    kn = jnp.concatenate([knew_ref[...], jnp.zeros((PAGE_SIZE - steps, QK_CAT), F32)], axis=0).astype(BF16)
    s_new = _dot_nt(q, kn)
    t_q = lax.broadcasted_iota(jnp.int32, s_new.shape, 0) % steps
    t_k = lax.broadcasted_iota(jnp.int32, s_new.shape, 1)
    scores.append(jnp.where(t_k <= t_q, s_new, NEG_INF))

    s = jnp.concatenate(scores, axis=1)
    p = jnp.exp(s - jnp.max(s, axis=1, keepdims=True))
    l = jnp.sum(p, axis=1, keepdims=True)
    p = p.astype(BF16)
    gk = PAGE_GROUP * PAGE_SIZE
    acc = [_dot(p[:, n_pages * PAGE_SIZE:], kn[:, :KV_RANK]), jnp.zeros((rows, KV_RANK), F32)]
    for i, j0 in enumerate(groups):
        acc[i % 2] = acc[i % 2] + _dot(p[:, i * gk:(i + 1) * gk], latent_keys(j0))
    o = ((acc[0] + acc[1]) / l).astype(BF16)
    for h in range(MLA_HEADS):
        o_ref[:, h * V_HEAD:(h + 1) * V_HEAD] = _dot(o[h * steps:(h + 1) * steps], wuv_ref[h])


def _attn_sample(page_table, q, knew, cache_ckv, cache_kr_t, w_uv, steps):
    nb, n_pages = page_table.shape
    assert n_pages % PAGE_GROUP == 0

    def page_spec(shape, j):
        return pl.BlockSpec((None, None) + shape, lambda b, pt: (0, pt[b, j], 0, 0))

    grid_spec = pltpu.PrefetchScalarGridSpec(
        num_scalar_prefetch=1,
        grid=(nb,),
        in_specs=[pl.BlockSpec((MLA_HEADS, steps, QK_CAT), lambda b, pt: (0, b, 0)),
                  pl.BlockSpec((steps, QK_CAT), lambda b, pt: (b, 0)),
                  pl.BlockSpec(w_uv.shape, lambda b, pt: (0, 0, 0))]
                 + [page_spec((PAGE_SIZE, KV_RANK), j) for j in range(n_pages)]
                 + [page_spec((QK_ROPE, PAGE_SIZE), j) for j in range(n_pages)],
        out_specs=pl.BlockSpec((steps, MLA_HEADS * V_HEAD), lambda b, pt: (b, 0)),
    )
    return pl.pallas_call(
        functools.partial(_attn_sample_kernel, n_pages=n_pages, steps=steps),
        grid_spec=grid_spec,
        out_shape=jax.ShapeDtypeStruct((nb * steps, MLA_HEADS * V_HEAD), F32),
        compiler_params=_params("arbitrary"),
        name="attn_sample",
    )(page_table, q, knew, w_uv, *([cache_ckv] * n_pages), *([cache_kr_t] * n_pages))


def _gelu_tanh(x):
    return 0.5 * x * (1.0 + jnp.tanh(math.sqrt(2.0 / math.pi) * (x + 0.044715 * (x * x * x))))


def _sigmoid(x):
    return 1.0 / (1.0 + jnp.exp(-x))


def _post_kernel(x_ref, yt_ref, u_ref, ymla_ref, p1_ref, p2_ref, d_ref, wglu_ref, wout_ref, gffn_ref,
                 wgate_ref, wup_ref, convw_ref, convb_ref, wdown_ref, gfin_ref,
                 y_ref, gate_ref, ybuf, hbuf, carry_ref, *, tm, seq_steps):
    long_seq = seq_steps == 0
    for o in range(S5_NOCT):
        for t in range(S5_CHUNK):
            ybuf[o, pl.ds(t, tm // S5_CHUNK, stride=S5_CHUNK), :] = yt_ref[o, t]
    y_ssm = jnp.concatenate([ybuf[o] for o in range(S5_NOCT)], axis=1)
    y = y_ssm + d_ref[...] * u_ref[...]
    y = _gelu_tanh(y)
    y = y * _sigmoid(_dot(y.astype(BF16), wglu_ref[...]))
    mixed = _dot(y.astype(BF16), wout_ref[:S5_W, :]) + _dot(ymla_ref[...].astype(BF16), wout_ref[S5_W:, :])
    x1 = x_ref[...] + mixed
    xn = _rms(x1, gffn_ref[...]).astype(BF16)

    row = lax.broadcasted_iota(jnp.int32, (tm, FF_CHUNK), 0)
    if long_seq:
        @pl.when(pl.program_id(1) == 0)
        def _():
            carry_ref[...] = p1_ref[...]
        t_in = row
    else:
        t_in = row % seq_steps

    for c in range(D_FF // FF_CHUNK):
        sl = slice(c * FF_CHUNK, (c + 1) * FF_CHUNK)
        gate = _dot(xn, wgate_ref[:, sl])
        up = _dot(xn, wup_ref[:, sl])
        if long_seq:
            back1 = jnp.broadcast_to(carry_ref[7:8, sl], gate.shape)
            back2 = jnp.where(t_in == 0, carry_ref[6:7, sl], back1)
            carry_ref[:, sl] = gate[tm - 8:, :]
            gate_ref[:, sl] = gate[tm - 8:, :]
        else:
            back1 = p1_ref[:, sl]
            back2 = p2_ref[:, sl]
            gate_ref[:, sl] = gate
        prev1 = jnp.where(t_in >= 1, pltpu.roll(gate, 1, 0), back1)
        prev2 = jnp.where(t_in >= 2, pltpu.roll(gate, 2, 0), back2)
        conv = (convb_ref[:, sl] + convw_ref[0:1, sl] * prev2 + convw_ref[1:2, sl] * prev1
                + convw_ref[2:3, sl] * gate)
        hbuf[:, sl] = (conv * _sigmoid(conv) * up).astype(BF16)

    x2 = x1 + _dot(hbuf[...], wdown_ref[...])
    y_ref[...] = _rms(x2, gfin_ref[...])


def _post_call(x2d, yt, u, y_mla, p1, p2, w, nb, tm, seq_steps):
    rows = x2d.shape[0]
    nt = rows // (nb * tm)
    assert nb * nt * tm == rows and tm % S5_CHUNK == 0
    long_seq = seq_steps == 0
    row = lambda width: pl.BlockSpec((tm, width), lambda b, i: (b * nt + i, 0))
    yt_spec = pl.BlockSpec((S5_NOCT, S5_CHUNK, tm // S5_CHUNK, LANE), lambda b, i: (0, 0, b * nt + i, 0))
    if long_seq:
        hist = [_const_spec((8, D_FF)), _const_spec((8, D_FF))]
        gate_spec = pl.BlockSpec((None, 8, D_FF), lambda b, i: (b, 0, 0))
        gate_shape = jax.ShapeDtypeStruct((nb, 8, D_FF), F32)
    else:
        hist = [row(D_FF), row(D_FF)]
        gate_spec = row(D_FF)
        gate_shape = jax.ShapeDtypeStruct((rows, D_FF), F32)
    return pl.pallas_call(
        functools.partial(_post_kernel, tm=tm, seq_steps=seq_steps),
        grid=(nb, nt),
        in_specs=[row(D_MODEL), yt_spec, row(S5_W), row(MLA_HEADS * V_HEAD)] + hist + [
            _const_spec((1, S5_W)), _const_spec((S5_W, S5_W)), _const_spec((D_MODEL, D_MODEL)),
            _const_spec((1, D_MODEL)), _const_spec((D_MODEL, D_FF)), _const_spec((D_MODEL, D_FF)),
            _const_spec((CONV_W, D_FF)), _const_spec((1, D_FF)), _const_spec((D_FF, D_MODEL)),
            _const_spec((1, D_MODEL))],
        out_specs=[row(D_MODEL), gate_spec],
        out_shape=[jax.ShapeDtypeStruct((rows, D_MODEL), F32), gate_shape],
        scratch_shapes=[pltpu.VMEM((S5_NOCT, tm, LANE), F32), pltpu.VMEM((tm, D_FF), BF16),
                        pltpu.VMEM((8, D_FF), F32)],
        compiler_params=_params("arbitrary", "arbitrary"),
        name="post",
    )(x2d, yt, u, y_mla, p1, p2, w["s5_d"], w["w_glu"], w["w_out"], w["g_ffn"], w["w_gate"], w["w_up"],
      w["conv_w"], w["conv_b"], w["w_down"], w["g_final"])


def _rope_tables(pos):
    half = QK_ROPE // 2
    inv = ROPE_BASE ** (-jnp.arange(half, dtype=F32) / half)
    ang = pos.astype(F32)[:, None] * inv[None, :]
    cos, sin = jnp.cos(ang), jnp.sin(ang)
    pad = jnp.zeros((pos.shape[0], ROPE_PAD - QK_ROPE), F32)
    return jnp.concatenate([cos, cos, pad], axis=1), jnp.concatenate([-sin, sin, pad], axis=1)


def kernel(x_prompt, x_sample, cache_ckv, cache_kr, state_s5_re, state_s5_im, state_conv, page_table, meta_tokens, g_mix, w_in, g_q, w_uq, g_kv, w_uk, w_uv, s5_a_re, s5_a_im, s5_log_dt, s5_b_re, s5_b_im, s5_c_re, s5_c_im, s5_d, w_glu, w_out, g_ffn, w_gate, w_up, conv_w, conv_b, w_down, g_final):
    assert w_in.shape[0] == 1, "single-layer step"
    nb, seq, _ = x_prompt.shape
    db, steps, _ = x_sample.shape
    n_past = page_table.shape[1] * PAGE_SIZE
    assert seq % S5_CHUNK == 0 and N_META % S5_CHUNK == 0 and steps == S5_CHUNK

    w_in_pad = jnp.concatenate([w_in[0], jnp.zeros((D_MODEL, IN_PAD - w_in.shape[2]), F32)], axis=1)
    wq = w_uq[0].reshape(Q_RANK, MLA_HEADS, QK_NOPE + QK_ROPE)
    wq_rope = jnp.concatenate([wq[:, :, QK_NOPE:], jnp.zeros((Q_RANK, MLA_HEADS, ROPE_PAD - QK_ROPE), F32)], axis=2)
    wq_perm = jnp.concatenate([wq[:, :, :QK_NOPE].reshape(Q_RANK, -1), wq_rope.reshape(Q_RANK, -1)], axis=1)
    w = {
        "g_mix": g_mix[0].reshape(1, -1), "w_in": w_in_pad.astype(BF16),
        "g_q": g_q[0].reshape(1, -1), "w_uq": wq_perm.astype(BF16), "w_uq_t": wq_perm.T.astype(BF16),
        "w_uk_t": w_uk[0].transpose(1, 2, 0).astype(BF16),
        "w_uk": w_uk[0].transpose(1, 0, 2).astype(BF16),
        "g_kv": g_kv[0].reshape(1, -1),
        "s5_d": s5_d[0].reshape(1, -1), "w_glu": w_glu[0].astype(BF16), "w_out": w_out[0].astype(BF16),
        "g_ffn": g_ffn[0].reshape(1, -1), "w_gate": w_gate[0].astype(BF16), "w_up": w_up[0].astype(BF16),
        "conv_w": conv_w[0], "conv_b": conv_b[0].reshape(1, -1), "w_down": w_down[0].astype(BF16),
        "g_final": g_final.reshape(1, -1),
    }
    w_uv_h = w_uv[0].transpose(1, 0, 2).astype(BF16)
    ops = _s5_operators(s5_a_re[0], s5_a_im[0], s5_log_dt[0], s5_b_re[0], s5_b_im[0], s5_c_re[0], s5_c_im[0])

    cos_m, sin_m = _rope_tables(jnp.arange(N_META, dtype=jnp.int32))
    cos_p, sin_p = _rope_tables(N_META + jnp.arange(seq, dtype=jnp.int32))
    tm_p = 512
    tm_s = 512
    cos_s, sin_s = _rope_tables(n_past + jnp.arange(tm_s, dtype=jnp.int32) % steps)

    tiles_per_seq = seq // tm_p
    u_m, ut_m, q_m, kcat_m, ckv_m, kr_m = _pre_call(meta_tokens, cos_m, sin_m, lambda i: (0, 0), N_META, BF16, w)
    u_p, ut_p, qt_p, kcat_p, ckv_p, kr_p, vt_p = _pre_call(
        x_prompt.reshape(nb * seq, D_MODEL), cos_p, sin_p, lambda i: (i % tiles_per_seq, 0), tm_p, BF16, w,
        q_transposed=True)
    u_s, ut_s, q_s, kcat_s, ckv_s, kr_s = _pre_call(x_sample.reshape(db * steps, D_MODEL), cos_s, sin_s,
                                                    lambda i: (0, 0), tm_s, F32, w)

    zero_state = jnp.zeros((1, 1, S5_STATE_W), F32)
    yt_m, fm_re, fm_im = _s5_mixer(ut_m, zero_state, zero_state, ops, 1)
    yt_p, s5_re_p, s5_im_p = _s5_mixer(ut_p, jnp.broadcast_to(fm_re, (nb, 1, S5_STATE_W)),
                                       jnp.broadcast_to(fm_im, (nb, 1, S5_STATE_W)), ops, nb)
    yt_s, s5_re_s, s5_im_s = _s5_mixer(ut_s, state_s5_re[0].reshape(db, 1, S5_STATE_W),
                                       state_s5_im[0].reshape(db, 1, S5_STATE_W), ops, db)

    kmeta = jnp.concatenate([kcat_m, jnp.zeros((PAGE_SIZE - N_META, QK_CAT), BF16)], axis=0)
    ymla_m = _attn_meta(q_m, kmeta, w_uv_h)
    ymla_p = _attn_prompt(qt_p, kcat_p.reshape(nb, seq, QK_CAT), vt_p, kmeta, kmeta[:, :KV_RANK].T,
                          w_uv[0].transpose(1, 2, 0).astype(BF16), nb, tq=256)
    ymla_s = _attn_sample(page_table, q_s, kcat_s, cache_ckv, jnp.swapaxes(cache_kr, 2, 3), w_uv_h, steps)

    zeros_hist = jnp.zeros((8, D_FF), F32)
    _, gate_m = _post_call(meta_tokens, yt_m, u_m, ymla_m, zeros_hist, zeros_hist, w, 1, N_META, 0)
    y_p, gate_p = _post_call(x_prompt.reshape(nb * seq, D_MODEL), yt_p, u_p, ymla_p, gate_m[0], gate_m[0],
                             w, nb, tm_p, 0)
    sc = state_conv[0]
    back1 = jnp.pad(sc[:, 1:2], ((0, 0), (0, steps - 1), (0, 0))).reshape(db * steps, D_FF)
    back2 = jnp.pad(sc, ((0, 0), (0, steps - 2), (0, 0))).reshape(db * steps, D_FF)
    y_s, gate_s = _post_call(x_sample.reshape(db * steps, D_MODEL), yt_s, u_s, ymla_s, back1, back2,
                             w, 1, tm_s // 2, steps)

    def with_meta(meta_rows, tok_rows, width):
        return jnp.concatenate([jnp.broadcast_to(meta_rows[None], (nb, N_META, width)),
                                tok_rows.reshape(nb, seq, width)], axis=1)[None]

    return (y_p.reshape(nb, seq, D_MODEL), y_s.reshape(db, steps, D_MODEL),
            with_meta(ckv_m, ckv_p, KV_RANK), with_meta(kr_m, kr_p, QK_ROPE),
            s5_re_p.reshape(1, nb, S5_GROUPS, S5_STATE), s5_im_p.reshape(1, nb, S5_GROUPS, S5_STATE),
            gate_p[:, 8 - (CONV_W - 1):][None],
            ckv_s.reshape(1, db, steps, KV_RANK), kr_s.reshape(1, db, steps, QK_ROPE),
            s5_re_s.reshape(1, db, S5_GROUPS, S5_STATE), s5_im_s.reshape(1, db, S5_GROUPS, S5_STATE),
            gate_s.reshape(db, steps, D_FF)[:, steps - (CONV_W - 1):][None])
```

```python
import functools
import math

import jax
import jax.numpy as jnp
from jax import lax
from jax.experimental import pallas as pl
from jax.experimental.pallas import tpu as pltpu

F32 = jnp.float32
BF16 = jnp.bfloat16

D_MODEL = 1024
N_META = 16
S5_W = 512
S5_GROUP = 16
S5_GROUPS = 32
S5_STATE = 64
MLA_HEADS = 4
QK_NOPE = 128
QK_ROPE = 64
V_HEAD = 128
Q_RANK = 384
KV_RANK = 256
D_FF = 2816
CONV_W = 3
ROPE_BASE = 10000.0
EPS = 1e-6
PAGE_SIZE = 128
ATTN_SCALE = 1.0 / math.sqrt(QK_NOPE + QK_ROPE)
LOG2_E = math.log2(math.e)

LANE = 128
ROPE_PAD = LANE
QK_CAT = KV_RANK + ROPE_PAD
IN_PAD = S5_W + Q_RANK + KV_RANK + ROPE_PAD
S5_CHUNK = 8
S5_OCT = LANE // S5_GROUP
S5_NOCT = S5_GROUPS // S5_OCT
S5_STATE_W = S5_GROUPS * S5_STATE
S5_SCAN_W = 256
S5_POW_PAD = 16
FF_CHUNK = 256
PAGE_GROUP = 4
VMEM_LIMIT = 56 * 1024 * 1024
NEG_INF = float("-inf")
HI = lax.Precision.HIGHEST


def _const_spec(shape):
    nd = len(shape)
    return pl.BlockSpec(shape, lambda *_: (0,) * nd, pipeline_mode=pl.Buffered(1))


def _params(*semantics):
    return pltpu.CompilerParams(dimension_semantics=semantics, vmem_limit_bytes=VMEM_LIMIT)


def _rms(x, g):
    return x * lax.rsqrt(jnp.mean(x * x, axis=-1, keepdims=True) + EPS) * g


def _rope_slab(x, cos, sin):
    lane = lax.broadcasted_iota(jnp.int32, x.shape, 1)
    half = QK_ROPE // 2
    swapped = jnp.where(lane < half, pltpu.roll(x, LANE - half, 1), pltpu.roll(x, half, 1))
    return x * cos + swapped * sin


def _dot(a, b):
    return jnp.dot(a, b, preferred_element_type=F32)


def _dot_nt(a, b):
    return lax.dot_general(a, b, (((1,), (1,)), ((), ())), preferred_element_type=F32)


def _pre_kernel(x_ref, cos_ref, sin_ref, cost_ref, sint_ref, gmix_ref, win_ref, gq_ref, wuq_ref, wuk_ref, gkv_ref,
                u_ref, ut_ref, q_ref, kcat_ref, ckv_ref, kr_ref, *rest, q_transposed):
    x = x_ref[...]
    xn = _rms(x, gmix_ref[...]).astype(BF16)
    z = _dot(xn, win_ref[...])
    u_ref[...] = z[:, :S5_W]
    slab_ref = rest[-1]
    chunks = u_ref.shape[0] // S5_CHUNK
    for o in range(S5_NOCT):
        slab_ref[o] = z[:, o * LANE:(o + 1) * LANE]
        for s in range(S5_CHUNK):
            ut_ref[o, s] = slab_ref[o, pl.ds(s, chunks, stride=S5_CHUNK), :].astype(BF16)
    cq = z[:, S5_W:S5_W + Q_RANK]
    ckv_raw = z[:, S5_W + Q_RANK:S5_W + Q_RANK + KV_RANK]
    kr_raw = z[:, S5_W + Q_RANK + KV_RANK:]
    cqn = _rms(cq, gq_ref[...]).astype(BF16)
    nope_w = MLA_HEADS * QK_NOPE
    half = QK_ROPE // 2
    if q_transposed:
        qt = _dot_nt(wuq_ref[...], cqn)
        cost = cost_ref[...]
        sint = sint_ref[...]
        for h in range(MLA_HEADS):
            qn = qt[h * QK_NOPE:(h + 1) * QK_NOPE].astype(BF16)
            q_lat = _dot(wuk_ref[h], qn) * (ATTN_SCALE * LOG2_E)
            xr = qt[nope_w + h * ROPE_PAD:nope_w + (h + 1) * ROPE_PAD]
            swapped = jnp.concatenate([xr[half:QK_ROPE], xr[:half], xr[QK_ROPE:]], axis=0)
            qr = (xr * cost + swapped * sint) * (ATTN_SCALE * LOG2_E)
            q_ref[h, :KV_RANK, :] = q_lat.astype(q_ref.dtype)
            q_ref[h, KV_RANK:, :] = qr.astype(q_ref.dtype)
    else:
        q = _dot(cqn, wuq_ref[...])
        for h in range(MLA_HEADS):
            qn = q[:, h * QK_NOPE:(h + 1) * QK_NOPE].astype(BF16)
            q_lat = _dot(qn, wuk_ref[h]) * ATTN_SCALE
            qr = _rope_slab(q[:, nope_w + h * ROPE_PAD:nope_w + (h + 1) * ROPE_PAD],
                            cos_ref[...], sin_ref[...]) * ATTN_SCALE
            q_ref[h, :, :KV_RANK] = q_lat.astype(q_ref.dtype)
            q_ref[h, :, KV_RANK:] = qr.astype(q_ref.dtype)
    ckv = _rms(ckv_raw, gkv_ref[...])
    kr = _rope_slab(kr_raw, cos_ref[...], sin_ref[...])
    ckv_ref[...] = ckv
    kr_ref[...] = kr[:, :QK_ROPE]
    kcat_ref[:, :KV_RANK] = ckv.astype(kcat_ref.dtype)
    kcat_ref[:, KV_RANK:] = kr.astype(kcat_ref.dtype)
    if q_transposed:
        rest[0][...] = ckv.T.astype(BF16)


def _pre_call(x2d, cos, sin, tab_map, tm, qdtype, w, q_transposed=False):
    rows = x2d.shape[0]
    assert rows % tm == 0 and tm % S5_CHUNK == 0
    row = lambda width: pl.BlockSpec((tm, width), lambda i: (i, 0))
    tab_map_t = lambda i: tab_map(i)[::-1]
    if q_transposed:
        wuq, wuk = w["w_uq_t"], w["w_uk"]
        q_spec = pl.BlockSpec((MLA_HEADS, QK_CAT, tm), lambda i: (0, 0, i))
        q_shape = jax.ShapeDtypeStruct((MLA_HEADS, QK_CAT, rows), qdtype)
        extra_specs = [pl.BlockSpec((KV_RANK, tm), lambda i: (0, i))]
        extra_shapes = [jax.ShapeDtypeStruct((KV_RANK, rows), BF16)]
    else:
        wuq, wuk = w["w_uq"], w["w_uk_t"]
        q_spec = pl.BlockSpec((MLA_HEADS, tm, QK_CAT), lambda i: (0, i, 0))
        q_shape = jax.ShapeDtypeStruct((MLA_HEADS, rows, QK_CAT), qdtype)
        extra_specs, extra_shapes = [], []
    return pl.pallas_call(
        functools.partial(_pre_kernel, q_transposed=q_transposed),
        grid=(rows // tm,),
        in_specs=[
            row(D_MODEL),
            pl.BlockSpec((tm, ROPE_PAD), tab_map),
            pl.BlockSpec((tm, ROPE_PAD), tab_map),
            pl.BlockSpec((ROPE_PAD, tm), tab_map_t),
            pl.BlockSpec((ROPE_PAD, tm), tab_map_t),
            _const_spec((1, D_MODEL)),
            _const_spec((D_MODEL, IN_PAD)),
            _const_spec((1, Q_RANK)),
            _const_spec(wuq.shape),
            _const_spec(wuk.shape),
            _const_spec((1, KV_RANK)),
        ],
        out_specs=[row(S5_W), pl.BlockSpec((S5_NOCT, S5_CHUNK, tm // S5_CHUNK, LANE), lambda i: (0, 0, i, 0)),
                   q_spec, row(QK_CAT), row(KV_RANK), row(QK_ROPE)] + extra_specs,
        out_shape=[
            jax.ShapeDtypeStruct((rows, S5_W), F32),
            jax.ShapeDtypeStruct((S5_NOCT, S5_CHUNK, rows // S5_CHUNK, LANE), BF16),
            q_shape,
            jax.ShapeDtypeStruct((rows, QK_CAT), qdtype),
            jax.ShapeDtypeStruct((rows, KV_RANK), F32),
            jax.ShapeDtypeStruct((rows, QK_ROPE), F32),
        ] + extra_shapes,
        scratch_shapes=[pltpu.VMEM((S5_NOCT, tm, LANE), F32)],
        compiler_params=_params("arbitrary"),
        name="pre",
    )(x2d, cos, sin, cos.T, sin.T, w["g_mix"], w["w_in"], w["g_q"], wuq, wuk, w["g_kv"])


def _chunk_inputs(ut_ref):
    return jnp.concatenate([ut_ref[s] for s in range(S5_CHUNK)], axis=1)


def _s5_state_in_kernel(ut_ref, wre_ref, wim_ref, sre_ref, sim_ref):
    u = _chunk_inputs(ut_ref)
    sre_ref[...] = _dot(u, wre_ref[0])
    sim_ref[...] = _dot(u, wim_ref[0])


def _s5_scan_kernel(sre_ref, sim_ref, h0re_ref, h0im_ref, are_ref, aim_ref, hre_ref, him_ref, fre_ref, fim_ref):
    nb, nc, width = sre_ref.shape
    ar = are_ref[...].reshape(1, 1, width)
    ai = aim_ref[...].reshape(1, 1, width)

    def body(c, carry):
        hr, hi = carry
        hre_ref[:, pl.ds(c, 1), :] = hr
        him_ref[:, pl.ds(c, 1), :] = hi
        sr = sre_ref[:, pl.ds(c, 1), :]
        si = sim_ref[:, pl.ds(c, 1), :]
        return ar * hr - ai * hi + sr, ar * hi + ai * hr + si

    hr, hi = lax.fori_loop(0, nc, body, (h0re_ref[...], h0im_ref[...]))
    fre_ref[...] = hr
    fim_ref[...] = hi


def _s5_out_kernel(ut_ref, hre_ref, him_ref, t_ref, mre_ref, mim_ref, yt_ref):
    y = (_dot(_chunk_inputs(ut_ref), t_ref[0]) + _dot(hre_ref[...].astype(BF16), mre_ref[0])
         + _dot(him_ref[...].astype(BF16), mim_ref[0]))
    for t in range(S5_CHUNK):
        yt_ref[t] = y[:, t * LANE:(t + 1) * LANE]


def _s5_mixer(ut, h0_re, h0_im, ops, nb):
    rows = ut.shape[2]
    nc = rows // nb
    tr = min(rows, 1024)
    assert rows % tr == 0
    ow = S5_OCT * S5_STATE
    cw = S5_CHUNK * LANE
    ut_spec = pl.BlockSpec((None, S5_CHUNK, tr, LANE), lambda q, r: (q, 0, r, 0))
    st_spec = pl.BlockSpec((tr, ow), lambda q, r: (r, q))
    oct_spec = lambda a, b: pl.BlockSpec((1, a, b), lambda q, r: (q, 0, 0))
    s_re, s_im = pl.pallas_call(
        _s5_state_in_kernel,
        grid=(S5_NOCT, rows // tr),
        in_specs=[ut_spec, oct_spec(cw, ow), oct_spec(cw, ow)],
        out_specs=[st_spec, st_spec],
        out_shape=[jax.ShapeDtypeStruct((rows, S5_STATE_W), F32)] * 2,
        compiler_params=_params("arbitrary", "arbitrary"),
        name="s5_state_in",
    )(ut, ops["w_re"], ops["w_im"])

    blk = pl.BlockSpec((nb, nc, S5_SCAN_W), lambda j: (0, 0, j))
    one = pl.BlockSpec((nb, 1, S5_SCAN_W), lambda j: (0, 0, j))
    vec = pl.BlockSpec((1, S5_SCAN_W), lambda j: (0, j))
    h_re, h_im, f_re, f_im = pl.pallas_call(
        _s5_scan_kernel,
        grid=(S5_STATE_W // S5_SCAN_W,),
        in_specs=[blk, blk, one, one, vec, vec],
        out_specs=[blk, blk, one, one],
        out_shape=[jax.ShapeDtypeStruct((nb, nc, S5_STATE_W), F32)] * 2
                  + [jax.ShapeDtypeStruct((nb, 1, S5_STATE_W), F32)] * 2,
        compiler_params=_params("arbitrary"),
        name="s5_scan",
    )(s_re.reshape(nb, nc, S5_STATE_W), s_im.reshape(nb, nc, S5_STATE_W), h0_re, h0_im, ops["a_re"], ops["a_im"])

    yt = pl.pallas_call(
        _s5_out_kernel,
        grid=(S5_NOCT, rows // tr),
        in_specs=[ut_spec, st_spec, st_spec, oct_spec(cw, cw), oct_spec(ow, cw), oct_spec(ow, cw)],
        out_specs=ut_spec,
        out_shape=jax.ShapeDtypeStruct(ut.shape, F32),
        compiler_params=_params("arbitrary", "arbitrary"),
        name="s5_out",
    )(ut, h_re.reshape(rows, S5_STATE_W), h_im.reshape(rows, S5_STATE_W), ops["t"], ops["m_re"], ops["m_im"])
    return yt, f_re, f_im


def _s5_operator_kernel(btr_ref, bti_ref, cr_ref, ci_ref, pl_re, pl_im, pr_re, pr_im,
                        t_ref, wre_ref, wim_ref, mre_ref, mim_ref):
    btr, bti = btr_ref[0], bti_ref[0]
    cr, ci = cr_ref[0], ci_ref[0]
    hi_dot = lambda a, b: jnp.dot(a, b, precision=HI, preferred_element_type=F32)
    kerns = []
    for k in range(S5_CHUNK + 1):
        ar, ai = pr_re[0, :, k:k + 1], pr_im[0, :, k:k + 1]
        cpr = cr * ar - ci * ai
        cpi = cr * ai + ci * ar
        if k < S5_CHUNK:
            kerns.append((hi_dot(btr, cpr) - hi_dot(bti, cpi)).astype(BF16))
            s = S5_CHUNK - 1 - k
            lr, li = pl_re[0, k:k + 1, :], pl_im[0, k:k + 1, :]
            wre_ref[0, s * LANE:(s + 1) * LANE, :] = (btr * lr - bti * li).astype(BF16)
            wim_ref[0, s * LANE:(s + 1) * LANE, :] = (btr * li + bti * lr).astype(BF16)
        if k >= 1:
            t = k - 1
            mre_ref[0, :, t * LANE:(t + 1) * LANE] = cpr.astype(BF16)
            mim_ref[0, :, t * LANE:(t + 1) * LANE] = (-cpi).astype(BF16)
    zero = jnp.zeros((LANE, LANE), BF16)
    for s in range(S5_CHUNK):
        for t in range(S5_CHUNK):
            t_ref[0, s * LANE:(s + 1) * LANE, t * LANE:(t + 1) * LANE] = kerns[t - s] if t >= s else zero


def _s5_operators(a_re, a_im, log_dt, b_re, b_im, c_re, c_im):
    steps = S5_CHUNK
    dt = jnp.exp(log_dt)[:, None]
    k = jnp.arange(steps + 1, dtype=F32)[:, None, None]
    mag = jnp.exp(k * (dt * a_re)[None])
    pw_r = mag * jnp.cos(k * (dt * a_im)[None])
    pw_i = mag * jnp.sin(k * (dt * a_im)[None])
    abr, abi = pw_r[1], pw_i[1]
    num_re, num_im = abr - 1.0, abi
    den = a_re * a_re + a_im * a_im
    f_re = (num_re * a_re + num_im * a_im) / den
    f_im = (num_im * a_re - num_re * a_im) / den
    bbr = f_re[..., None] * b_re - f_im[..., None] * b_im
    bbi = f_re[..., None] * b_im + f_im[..., None] * b_re
    same_group = jnp.arange(S5_OCT)[:, None, None, None] == jnp.arange(S5_OCT)[None, None, :, None]
    ow = S5_OCT * S5_STATE

    def block_diag(a):
        a = a.reshape(S5_NOCT, S5_OCT, a.shape[1], 1, a.shape[2])
        return jnp.where(same_group, a, 0.0).reshape(S5_NOCT, S5_OCT * a.shape[2], S5_OCT * a.shape[4])

    pad_k = S5_POW_PAD - (steps + 1)
    lanes = lambda p: jnp.pad(p.reshape(steps + 1, S5_NOCT, ow).transpose(1, 0, 2), ((0, 0), (0, pad_k), (0, 0)))
    rows = lambda p: jnp.pad(p.reshape(steps + 1, S5_NOCT, ow).transpose(1, 2, 0), ((0, 0), (0, 0), (0, pad_k)))
    bt_re, bt_im = block_diag(bbr.transpose(0, 2, 1)), block_diag(bbi.transpose(0, 2, 1))
    cb_re, cb_im = block_diag(c_re.transpose(0, 2, 1)), block_diag(c_im.transpose(0, 2, 1))
    cw = steps * LANE
    oct_spec = lambda a, b: pl.BlockSpec((1, a, b), lambda q: (q, 0, 0))
    t_mat, w_re, w_im, m_re, m_im = pl.pallas_call(
        _s5_operator_kernel,
        grid=(S5_NOCT,),
        in_specs=[oct_spec(LANE, ow), oct_spec(LANE, ow), oct_spec(ow, LANE), oct_spec(ow, LANE),
                  oct_spec(S5_POW_PAD, ow), oct_spec(S5_POW_PAD, ow), oct_spec(ow, S5_POW_PAD),
                  oct_spec(ow, S5_POW_PAD)],
        out_specs=[oct_spec(cw, cw), oct_spec(cw, ow), oct_spec(cw, ow), oct_spec(ow, cw), oct_spec(ow, cw)],
        out_shape=[jax.ShapeDtypeStruct((S5_NOCT, cw, cw), BF16)]
                  + [jax.ShapeDtypeStruct((S5_NOCT, cw, ow), BF16)] * 2
                  + [jax.ShapeDtypeStruct((S5_NOCT, ow, cw), BF16)] * 2,
        compiler_params=_params("arbitrary"),
        name="s5_operators",
    )(bt_re, bt_im, cb_re, cb_im, lanes(pw_r), lanes(pw_i), rows(pw_r), rows(pw_i))
    return {
        "t": t_mat, "w_re": w_re, "w_im": w_im, "m_re": m_re, "m_im": m_im,
        "a_re": pw_r[steps].reshape(1, S5_STATE_W), "a_im": pw_i[steps].reshape(1, S5_STATE_W),
    }


def _softmax_update_t(s, vt, m_ref, l_ref, acc_ref):
    m_prev = m_ref[...]
    m_new = jnp.maximum(m_prev, jnp.max(s, axis=0, keepdims=True))
    alpha = jnp.exp2(m_prev - m_new)
    p = jnp.exp2(s - m_new)
    l_ref[...] = alpha * l_ref[...] + jnp.sum(p, axis=0, keepdims=True)
    acc_ref[...] = alpha * acc_ref[...] + _dot(vt, p.astype(BF16))
    m_ref[...] = m_new


def _attn_prompt_kernel(qt_ref, k_ref, vt_ref, kmeta_ref, vmeta_t_ref, wuvt_ref, o_ref, m_ref, l_ref, acc_ref, s_ref,
                        *, tq):
    qi = pl.program_id(1)
    qt = jnp.concatenate([qt_ref[h] for h in range(MLA_HEADS)], axis=1)

    def keys(j):
        return pl.ds(pl.multiple_of(j * tq, tq), tq)

    def scores(j):
        return _dot(k_ref[0, keys(j), :], qt)

    s = _dot(kmeta_ref[...], qt)
    s = jnp.where(lax.broadcasted_iota(jnp.int32, s.shape, 0) < N_META, s, NEG_INF)
    m0 = jnp.max(s, axis=0, keepdims=True)
    p = jnp.exp2(s - m0)
    m_ref[...] = m0
    l_ref[...] = jnp.sum(p, axis=0, keepdims=True)
    acc_ref[...] = _dot(vmeta_t_ref[...], p.astype(BF16))

    s_ref[...] = scores(0)

    def body(kj, carry):
        s_next = scores(kj + 1)
        _softmax_update_t(s_ref[...], vt_ref[:, keys(kj)], m_ref, l_ref, acc_ref)
        s_ref[...] = s_next
        return carry

    lax.fori_loop(0, qi, body, 0)

    s = s_ref[...]
    t_k = lax.broadcasted_iota(jnp.int32, s.shape, 0)
    t_q = lax.broadcasted_iota(jnp.int32, s.shape, 1) % tq
    s = jnp.where(t_k <= t_q, s, NEG_INF)
    _softmax_update_t(s, vt_ref[:, keys(qi)], m_ref, l_ref, acc_ref)

    ot = (acc_ref[...] / l_ref[...]).astype(BF16)
    for h in range(MLA_HEADS):
        yt = _dot(wuvt_ref[h], ot[:, h * tq:(h + 1) * tq])
        o_ref[:, h * V_HEAD:(h + 1) * V_HEAD] = yt.T.astype(o_ref.dtype)


def _attn_prompt(qt, kcat, vt, kmeta, vmeta_t, w_uv_t, nb, tq):
    t = kcat.shape[1]
    nq = t // tq
    cols = MLA_HEADS * tq
    return pl.pallas_call(
        functools.partial(_attn_prompt_kernel, tq=tq),
        grid=(nb, nq),
        in_specs=[
            pl.BlockSpec((MLA_HEADS, QK_CAT, tq), lambda b, i: (0, 0, b * nq + i)),
            pl.BlockSpec((1, t, QK_CAT), lambda b, i: (b, 0, 0)),
            pl.BlockSpec((KV_RANK, t), lambda b, i: (0, b)),
            _const_spec(kmeta.shape),
            _const_spec(vmeta_t.shape),
            _const_spec(w_uv_t.shape),
        ],
        out_specs=pl.BlockSpec((tq, MLA_HEADS * V_HEAD), lambda b, i: (b * nq + i, 0)),
        out_shape=jax.ShapeDtypeStruct((nb * t, MLA_HEADS * V_HEAD), BF16),
        scratch_shapes=[pltpu.VMEM((1, cols), F32), pltpu.VMEM((1, cols), F32), pltpu.VMEM((KV_RANK, cols), F32),
                        pltpu.VMEM((tq, cols), F32)],
        compiler_params=_params("arbitrary", "arbitrary"),
        name="attn_prompt",
    )(qt, kcat, vt, kmeta, vmeta_t, w_uv_t)


def _attn_meta_kernel(q_ref, kmeta_ref, wuv_ref, o_ref):
    rows = MLA_HEADS * N_META
    q = q_ref[...].reshape(rows, QK_CAT)
    km = kmeta_ref[...]
    s = _dot_nt(q, km)
    t_q = lax.broadcasted_iota(jnp.int32, s.shape, 0) % N_META
    t_k = lax.broadcasted_iota(jnp.int32, s.shape, 1)
    s = jnp.where(t_k <= t_q, s, NEG_INF)
    p = jnp.exp(s - jnp.max(s, axis=1, keepdims=True))
    o = _dot(p.astype(BF16), km[:, :KV_RANK]) / jnp.sum(p, axis=1, keepdims=True)
    o = o.astype(BF16)
    for h in range(MLA_HEADS):
        o_ref[:, h * V_HEAD:(h + 1) * V_HEAD] = _dot(o[h * N_META:(h + 1) * N_META], wuv_ref[h]).astype(o_ref.dtype)


def _attn_meta(q, kmeta, w_uv):
    return pl.pallas_call(
        _attn_meta_kernel,
        out_shape=jax.ShapeDtypeStruct((N_META, MLA_HEADS * V_HEAD), BF16),
        name="attn_meta",
    )(q, kmeta, w_uv)


def _attn_sample_kernel(pt_ref, q_ref, knew_ref, wuv_ref, *rest, n_pages, steps):
    ckv_refs = rest[:n_pages]
    kr_refs = rest[n_pages:2 * n_pages]
    o_ref = rest[2 * n_pages]
    rows = MLA_HEADS * steps
    q = q_ref[...].reshape(rows, QK_CAT).astype(BF16)
    q_lat = q[:, :KV_RANK]
    q_rope = q[:, KV_RANK:KV_RANK + QK_ROPE]
    groups = range(0, n_pages, PAGE_GROUP)

    def latent_keys(j0):
        return jnp.concatenate([ckv_refs[j][...].astype(BF16) for j in range(j0, j0 + PAGE_GROUP)], axis=0)

    scores = []
    for j0 in groups:
        kr_t = jnp.concatenate([kr_refs[j][...].astype(BF16) for j in range(j0, j0 + PAGE_GROUP)], axis=1)
        scores.append(_dot_nt(q_lat, latent_keys(j0)) + _dot(q_rope, kr_t))
    kn = jnp.concatenate([knew_ref[...], jnp.zeros((PAGE_SIZE - steps, QK_CAT), F32)], axis=0).astype(BF16)
    s_new = _dot_nt(q, kn)
    t_q = lax.broadcasted_iota(jnp.int32, s_new.shape, 0) % steps
    t_k = lax.broadcasted_iota(jnp.int32, s_new.shape, 1)
    scores.append(jnp.where(t_k <= t_q, s_new, NEG_INF))

    s = jnp.concatenate(scores, axis=1)
    p = jnp.exp(s - jnp.max(s, axis=1, keepdims=True))
    l = jnp.sum(p, axis=1, keepdims=True)
    p = p.astype(BF16)
    gk = PAGE_GROUP * PAGE_SIZE
    acc = [_dot(p[:, n_pages * PAGE_SIZE:], kn[:, :KV_RANK]), jnp.zeros((rows, KV_RANK), F32)]
    for i, j0 in enumerate(groups):
        acc[i % 2] = acc[i % 2] + _dot(p[:, i * gk:(i + 1) * gk], latent_keys(j0))
    o = ((acc[0] + acc[1]) / l).astype(BF16)
    for h in range(MLA_HEADS):
        o_ref[:, h * V_HEAD:(h + 1) * V_HEAD] = _dot(o[h * steps:(h + 1) * steps], wuv_ref[h])


def _attn_sample(page_table, q, knew, cache_ckv, cache_kr_t, w_uv, steps):
    nb, n_pages = page_table.shape
    assert n_pages % PAGE_GROUP == 0

    def page_spec(shape, j):
        return pl.BlockSpec((None, None) + shape, lambda b, pt: (0, pt[b, j], 0, 0))

    grid_spec = pltpu.PrefetchScalarGridSpec(
        num_scalar_prefetch=1,
        grid=(nb,),
        in_specs=[pl.BlockSpec((MLA_HEADS, steps, QK_CAT), lambda b, pt: (0, b, 0)),
                  pl.BlockSpec((steps, QK_CAT), lambda b, pt: (b, 0)),
                  pl.BlockSpec(w_uv.shape, lambda b, pt: (0, 0, 0))]
                 + [page_spec((PAGE_SIZE, KV_RANK), j) for j in range(n_pages)]
                 + [page_spec((QK_ROPE, PAGE_SIZE), j) for j in range(n_pages)],
        out_specs=pl.BlockSpec((steps, MLA_HEADS * V_HEAD), lambda b, pt: (b, 0)),
    )
    return pl.pallas_call(
        functools.partial(_attn_sample_kernel, n_pages=n_pages, steps=steps),
        grid_spec=grid_spec,
        out_shape=jax.ShapeDtypeStruct((nb * steps, MLA_HEADS * V_HEAD), F32),
        compiler_params=_params("arbitrary"),
        name="attn_sample",
    )(page_table, q, knew, w_uv, *([cache_ckv] * n_pages), *([cache_kr_t] * n_pages))


def _gelu_tanh(x):
    return 0.5 * x * (1.0 + jnp.tanh(math.sqrt(2.0 / math.pi) * (x + 0.044715 * (x * x * x))))


def _sigmoid(x):
    return 1.0 / (1.0 + jnp.exp(-x))


def _post_kernel(x_ref, yt_ref, u_ref, ymla_ref, p1_ref, p2_ref, d_ref, wglu_ref, wout_ref, gffn_ref,
                 wgate_ref, wup_ref, convw_ref, convb_ref, wdown_ref, gfin_ref,
                 y_ref, gate_ref, ybuf, hbuf, carry_ref, *, tm, seq_steps):
    long_seq = seq_steps == 0
    for o in range(S5_NOCT):
        for t in range(S5_CHUNK):
            ybuf[o, pl.ds(t, tm // S5_CHUNK, stride=S5_CHUNK), :] = yt_ref[o, t]
    y_ssm = jnp.concatenate([ybuf[o] for o in range(S5_NOCT)], axis=1)
    y = y_ssm + d_ref[...] * u_ref[...]
    y = _gelu_tanh(y)
    y = y * _sigmoid(_dot(y.astype(BF16), wglu_ref[...]))
    mixed = _dot(y.astype(BF16), wout_ref[:S5_W, :]) + _dot(ymla_ref[...].astype(BF16), wout_ref[S5_W:, :])
    x1 = x_ref[...] + mixed
    xn = _rms(x1, gffn_ref[...]).astype(BF16)

    row = lax.broadcasted_iota(jnp.int32, (tm, FF_CHUNK), 0)
    if long_seq:
        @pl.when(pl.program_id(1) == 0)
        def _():
            carry_ref[...] = p1_ref[...]
        t_in = row
    else:
        t_in = row % seq_steps

    for c in range(D_FF // FF_CHUNK):
        sl = slice(c * FF_CHUNK, (c + 1) * FF_CHUNK)
        gate = _dot(xn, wgate_ref[:, sl])
        up = _dot(xn, wup_ref[:, sl])
        if long_seq:
            back1 = jnp.broadcast_to(carry_ref[7:8, sl], gate.shape)
            back2 = jnp.where(t_in == 0, carry_ref[6:7, sl], back1)
            carry_ref[:, sl] = gate[tm - 8:, :]
            gate_ref[:, sl] = gate[tm - 8:, :]
        else:
            back1 = p1_ref[:, sl]
            back2 = p2_ref[:, sl]
            gate_ref[:, sl] = gate
        prev1 = jnp.where(t_in >= 1, pltpu.roll(gate, 1, 0), back1)
        prev2 = jnp.where(t_in >= 2, pltpu.roll(gate, 2, 0), back2)
        conv = (convb_ref[:, sl] + convw_ref[0:1, sl] * prev2 + convw_ref[1:2, sl] * prev1
                + convw_ref[2:3, sl] * gate)
        hbuf[:, sl] = (conv * _sigmoid(conv) * up).astype(BF16)

    x2 = x1 + _dot(hbuf[...], wdown_ref[...])
    y_ref[...] = _rms(x2, gfin_ref[...])


def _post_call(x2d, yt, u, y_mla, p1, p2, w, nb, tm, seq_steps):
    rows = x2d.shape[0]
    nt = rows // (nb * tm)
    assert nb * nt * tm == rows and tm % S5_CHUNK == 0
    long_seq = seq_steps == 0
    row = lambda width: pl.BlockSpec((tm, width), lambda b, i: (b * nt + i, 0))
    yt_spec = pl.BlockSpec((S5_NOCT, S5_CHUNK, tm // S5_CHUNK, LANE), lambda b, i: (0, 0, b * nt + i, 0))
    if long_seq:
        hist = [_const_spec((8, D_FF)), _const_spec((8, D_FF))]
        gate_spec = pl.BlockSpec((None, 8, D_FF), lambda b, i: (b, 0, 0))
        gate_shape = jax.ShapeDtypeStruct((nb, 8, D_FF), F32)
    else:
        hist = [row(D_FF), row(D_FF)]
        gate_spec = row(D_FF)
        gate_shape = jax.ShapeDtypeStruct((rows, D_FF), F32)
    return pl.pallas_call(
        functools.partial(_post_kernel, tm=tm, seq_steps=seq_steps),
        grid=(nb, nt),
        in_specs=[row(D_MODEL), yt_spec, row(S5_W), row(MLA_HEADS * V_HEAD)] + hist + [
            _const_spec((1, S5_W)), _const_spec((S5_W, S5_W)), _const_spec((D_MODEL, D_MODEL)),
            _const_spec((1, D_MODEL)), _const_spec((D_MODEL, D_FF)), _const_spec((D_MODEL, D_FF)),
            _const_spec((CONV_W, D_FF)), _const_spec((1, D_FF)), _const_spec((D_FF, D_MODEL)),
            _const_spec((1, D_MODEL))],
        out_specs=[row(D_MODEL), gate_spec],
        out_shape=[jax.ShapeDtypeStruct((rows, D_MODEL), F32), gate_shape],
        scratch_shapes=[pltpu.VMEM((S5_NOCT, tm, LANE), F32), pltpu.VMEM((tm, D_FF), BF16),
                        pltpu.VMEM((8, D_FF), F32)],
        compiler_params=_params("arbitrary", "arbitrary"),
        name="post",
    )(x2d, yt, u, y_mla, p1, p2, w["s5_d"], w["w_glu"], w["w_out"], w["g_ffn"], w["w_gate"], w["w_up"],
      w["conv_w"], w["conv_b"], w["w_down"], w["g_final"])


def _rope_tables(pos):
    half = QK_ROPE // 2
    inv = ROPE_BASE ** (-jnp.arange(half, dtype=F32) / half)
    ang = pos.astype(F32)[:, None] * inv[None, :]
    cos, sin = jnp.cos(ang), jnp.sin(ang)
    pad = jnp.zeros((pos.shape[0], ROPE_PAD - QK_ROPE), F32)
    return jnp.concatenate([cos, cos, pad], axis=1), jnp.concatenate([-sin, sin, pad], axis=1)


def kernel(x_prompt, x_sample, cache_ckv, cache_kr, state_s5_re, state_s5_im, state_conv, page_table, meta_tokens, g_mix, w_in, g_q, w_uq, g_kv, w_uk, w_uv, s5_a_re, s5_a_im, s5_log_dt, s5_b_re, s5_b_im, s5_c_re, s5_c_im, s5_d, w_glu, w_out, g_ffn, w_gate, w_up, conv_w, conv_b, w_down, g_final):
    assert w_in.shape[0] == 1, "single-layer step"
    nb, seq, _ = x_prompt.shape
    db, steps, _ = x_sample.shape
    n_past = page_table.shape[1] * PAGE_SIZE
    assert seq % S5_CHUNK == 0 and N_META % S5_CHUNK == 0 and steps == S5_CHUNK

    w_in_pad = jnp.concatenate([w_in[0], jnp.zeros((D_MODEL, IN_PAD - w_in.shape[2]), F32)], axis=1)
    wq = w_uq[0].reshape(Q_RANK, MLA_HEADS, QK_NOPE + QK_ROPE)
    wq_rope = jnp.concatenate([wq[:, :, QK_NOPE:], jnp.zeros((Q_RANK, MLA_HEADS, ROPE_PAD - QK_ROPE), F32)], axis=2)
    wq_perm = jnp.concatenate([wq[:, :, :QK_NOPE].reshape(Q_RANK, -1), wq_rope.reshape(Q_RANK, -1)], axis=1)
    w = {
        "g_mix": g_mix[0].reshape(1, -1), "w_in": w_in_pad.astype(BF16),
        "g_q": g_q[0].reshape(1, -1), "w_uq": wq_perm.astype(BF16), "w_uq_t": wq_perm.T.astype(BF16),
        "w_uk_t": w_uk[0].transpose(1, 2, 0).astype(BF16),
        "w_uk": w_uk[0].transpose(1, 0, 2).astype(BF16),
        "g_kv": g_kv[0].reshape(1, -1),
        "s5_d": s5_d[0].reshape(1, -1), "w_glu": w_glu[0].astype(BF16), "w_out": w_out[0].astype(BF16),
        "g_ffn": g_ffn[0].reshape(1, -1), "w_gate": w_gate[0].astype(BF16), "w_up": w_up[0].astype(BF16),
        "conv_w": conv_w[0], "conv_b": conv_b[0].reshape(1, -1), "w_down": w_down[0].astype(BF16),
        "g_final": g_final.reshape(1, -1),
    }
    w_uv_h = w_uv[0].transpose(1, 0, 2).astype(BF16)
    ops = _s5_operators(s5_a_re[0], s5_a_im[0], s5_log_dt[0], s5_b_re[0], s5_b_im[0], s5_c_re[0], s5_c_im[0])

    cos_m, sin_m = _rope_tables(jnp.arange(N_META, dtype=jnp.int32))
    cos_p, sin_p = _rope_tables(N_META + jnp.arange(seq, dtype=jnp.int32))
    tm_p = 512
    tm_s = 512
    cos_s, sin_s = _rope_tables(n_past + jnp.arange(tm_s, dtype=jnp.int32) % steps)

    tiles_per_seq = seq // tm_p
    u_m, ut_m, q_m, kcat_m, ckv_m, kr_m = _pre_call(meta_tokens, cos_m, sin_m, lambda i: (0, 0), N_META, BF16, w)
    u_p, ut_p, qt_p, kcat_p, ckv_p, kr_p, vt_p = _pre_call(
        x_prompt.reshape(nb * seq, D_MODEL), cos_p, sin_p, lambda i: (i % tiles_per_seq, 0), tm_p, BF16, w,
        q_transposed=True)
    u_s, ut_s, q_s, kcat_s, ckv_s, kr_s = _pre_call(x_sample.reshape(db * steps, D_MODEL), cos_s, sin_s,
                                                    lambda i: (0, 0), tm_s, F32, w)

    zero_state = jnp.zeros((1, 1, S5_STATE_W), F32)
    yt_m, fm_re, fm_im = _s5_mixer(ut_m, zero_state, zero_state, ops, 1)
    yt_p, s5_re_p, s5_im_p = _s5_mixer(ut_p, jnp.broadcast_to(fm_re, (nb, 1, S5_STATE_W)),
                                       jnp.broadcast_to(fm_im, (nb, 1, S5_STATE_W)), ops, nb)
    yt_s, s5_re_s, s5_im_s = _s5_mixer(ut_s, state_s5_re[0].reshape(db, 1, S5_STATE_W),
                                       state_s5_im[0].reshape(db, 1, S5_STATE_W), ops, db)

    kmeta = jnp.concatenate([kcat_m, jnp.zeros((PAGE_SIZE - N_META, QK_CAT), BF16)], axis=0)
    ymla_m = _attn_meta(q_m, kmeta, w_uv_h)
    ymla_p = _attn_prompt(qt_p, kcat_p.reshape(nb, seq, QK_CAT), vt_p, kmeta, kmeta[:, :KV_RANK].T,
                          w_uv[0].transpose(1, 2, 0).astype(BF16), nb, tq=256)
    ymla_s = _attn_sample(page_table, q_s, kcat_s, cache_ckv, jnp.swapaxes(cache_kr, 2, 3), w_uv_h, steps)

    zeros_hist = jnp.zeros((8, D_FF), F32)
    _, gate_m = _post_call(meta_tokens, yt_m, u_m, ymla_m, zeros_hist, zeros_hist, w, 1, N_META, 0)
    y_p, gate_p = _post_call(x_prompt.reshape(nb * seq, D_MODEL), yt_p, u_p, ymla_p, gate_m[0], gate_m[0],
                             w, nb, tm_p, 0)
    sc = state_conv[0]
    back1 = jnp.pad(sc[:, 1:2], ((0, 0), (0, steps - 1), (0, 0))).reshape(db * steps, D_FF)
    back2 = jnp.pad(sc, ((0, 0), (0, steps - 2), (0, 0))).reshape(db * steps, D_FF)
    y_s, gate_s = _post_call(x_sample.reshape(db * steps, D_MODEL), yt_s, u_s, ymla_s, back1, back2,
                             w, 1, tm_s // 2, steps)

    def with_meta(meta_rows, tok_rows, width):
        return jnp.concatenate([jnp.broadcast_to(meta_rows[None], (nb, N_META, width)),
                                tok_rows.reshape(nb, seq, width)], axis=1)[None]

    return (y_p.reshape(nb, seq, D_MODEL), y_s.reshape(db, steps, D_MODEL),
            with_meta(ckv_m, ckv_p, KV_RANK), with_meta(kr_m, kr_p, QK_ROPE),
            s5_re_p.reshape(1, nb, S5_GROUPS, S5_STATE), s5_im_p.reshape(1, nb, S5_GROUPS, S5_STATE),
            gate_p[:, 8 - (CONV_W - 1):][None],
            ckv_s.reshape(1, db, steps, KV_RANK), kr_s.reshape(1, db, steps, QK_ROPE),
            s5_re_s.reshape(1, db, S5_GROUPS, S5_STATE), s5_im_s.reshape(1, db, S5_GROUPS, S5_STATE),
            gate_s.reshape(db, steps, D_FF)[:, steps - (CONV_W - 1):][None])
```

```python
import functools
import math

import jax
import jax.numpy as jnp
from jax import lax
from jax.experimental import pallas as pl
from jax.experimental.pallas import tpu as pltpu

F32 = jnp.float32
BF16 = jnp.bfloat16

D_MODEL = 1024
N_META = 16
S5_W = 512
S5_GROUP = 16
S5_GROUPS = 32
S5_STATE = 64
MLA_HEADS = 4
QK_NOPE = 128
QK_ROPE = 64
V_HEAD = 128
Q_RANK = 384
KV_RANK = 256
D_FF = 2816
CONV_W = 3
ROPE_BASE = 10000.0
EPS = 1e-6
PAGE_SIZE = 128
ATTN_SCALE = 1.0 / math.sqrt(QK_NOPE + QK_ROPE)
LOG2_E = math.log2(math.e)

LANE = 128
ROPE_PAD = LANE
QK_CAT = KV_RANK + ROPE_PAD
IN_PAD = S5_W + Q_RANK + KV_RANK + ROPE_PAD
S5_CHUNK = 8
S5_OCT = LANE // S5_GROUP
S5_NOCT = S5_GROUPS // S5_OCT
S5_STATE_W = S5_GROUPS * S5_STATE
S5_SCAN_W = 256
S5_POW_PAD = 16
FF_CHUNK = 256
PAGE_GROUP = 4
VMEM_LIMIT = 56 * 1024 * 1024
NEG_INF = float("-inf")
HI = lax.Precision.HIGHEST


def _const_spec(shape):
    nd = len(shape)
    return pl.BlockSpec(shape, lambda *_: (0,) * nd, pipeline_mode=pl.Buffered(1))


def _params(*semantics):
    return pltpu.CompilerParams(dimension_semantics=semantics, vmem_limit_bytes=VMEM_LIMIT)


def _rms(x, g):
    return x * lax.rsqrt(jnp.mean(x * x, axis=-1, keepdims=True) + EPS) * g


def _rope_slab(x, cos, sin):
    lane = lax.broadcasted_iota(jnp.int32, x.shape, 1)
    half = QK_ROPE // 2
    swapped = jnp.where(lane < half, pltpu.roll(x, LANE - half, 1), pltpu.roll(x, half, 1))
    return x * cos + swapped * sin


def _dot(a, b):
    return jnp.dot(a, b, preferred_element_type=F32)


def _dot_nt(a, b):
    return lax.dot_general(a, b, (((1,), (1,)), ((), ())), preferred_element_type=F32)


def _pre_kernel(x_ref, cos_ref, sin_ref, cost_ref, sint_ref, gmix_ref, win_ref, gq_ref, wuq_ref, wuk_ref, gkv_ref,
                u_ref, ut_ref, q_ref, kcat_ref, ckv_ref, kr_ref, *rest, q_transposed):
    x = x_ref[...]
    xn = _rms(x, gmix_ref[...]).astype(BF16)
    z = _dot(xn, win_ref[...])
    u_ref[...] = z[:, :S5_W]
    slab_ref = rest[-1]
    chunks = u_ref.shape[0] // S5_CHUNK
    for o in range(S5_NOCT):
        slab_ref[o] = z[:, o * LANE:(o + 1) * LANE]
        for s in range(S5_CHUNK):
            ut_ref[o, s] = slab_ref[o, pl.ds(s, chunks, stride=S5_CHUNK), :].astype(BF16)
    cq = z[:, S5_W:S5_W + Q_RANK]
    ckv_raw = z[:, S5_W + Q_RANK:S5_W + Q_RANK + KV_RANK]
    kr_raw = z[:, S5_W + Q_RANK + KV_RANK:]
    cqn = _rms(cq, gq_ref[...]).astype(BF16)
    nope_w = MLA_HEADS * QK_NOPE
    half = QK_ROPE // 2
    if q_transposed:
        qt = _dot_nt(wuq_ref[...], cqn)
        cost = cost_ref[...]
        sint = sint_ref[...]
        for h in range(MLA_HEADS):
            qn = qt[h * QK_NOPE:(h + 1) * QK_NOPE].astype(BF16)
            q_lat = _dot(wuk_ref[h], qn) * (ATTN_SCALE * LOG2_E)
            xr = qt[nope_w + h * ROPE_PAD:nope_w + (h + 1) * ROPE_PAD]
            swapped = jnp.concatenate([xr[half:QK_ROPE], xr[:half], xr[QK_ROPE:]], axis=0)
            qr = (xr * cost + swapped * sint) * (ATTN_SCALE * LOG2_E)
            q_ref[h, :KV_RANK, :] = q_lat.astype(q_ref.dtype)
            q_ref[h, KV_RANK:, :] = qr.astype(q_ref.dtype)
    else:
        q = _dot(cqn, wuq_ref[...])
        for h in range(MLA_HEADS):
            qn = q[:, h * QK_NOPE:(h + 1) * QK_NOPE].astype(BF16)
            q_lat = _dot(qn, wuk_ref[h]) * ATTN_SCALE
            qr = _rope_slab(q[:, nope_w + h * ROPE_PAD:nope_w + (h + 1) * ROPE_PAD],
                            cos_ref[...], sin_ref[...]) * ATTN_SCALE
            q_ref[h, :, :KV_RANK] = q_lat.astype(q_ref.dtype)
            q_ref[h, :, KV_RANK:] = qr.astype(q_ref.dtype)
    ckv = _rms(ckv_raw, gkv_ref[...])
    kr = _rope_slab(kr_raw, cos_ref[...], sin_ref[...])
    ckv_ref[...] = ckv
    kr_ref[...] = kr[:, :QK_ROPE]
    kcat_ref[:, :KV_RANK] = ckv.astype(kcat_ref.dtype)
    kcat_ref[:, KV_RANK:] = kr.astype(kcat_ref.dtype)
    if q_transposed:
        rest[0][...] = ckv.T.astype(BF16)


def _pre_call(x2d, cos, sin, tab_map, tm, qdtype, w, q_transposed=False):
    rows = x2d.shape[0]
    assert rows % tm == 0 and tm % S5_CHUNK == 0
    row = lambda width: pl.BlockSpec((tm, width), lambda i: (i, 0))
    tab_map_t = lambda i: tab_map(i)[::-1]
    if q_transposed:
        wuq, wuk = w["w_uq_t"], w["w_uk"]
        q_spec = pl.BlockSpec((MLA_HEADS, QK_CAT, tm), lambda i: (0, 0, i))
        q_shape = jax.ShapeDtypeStruct((MLA_HEADS, QK_CAT, rows), qdtype)
        extra_specs = [pl.BlockSpec((KV_RANK, tm), lambda i: (0, i))]
        extra_shapes = [jax.ShapeDtypeStruct((KV_RANK, rows), BF16)]
    else:
        wuq, wuk = w["w_uq"], w["w_uk_t"]
        q_spec = pl.BlockSpec((MLA_HEADS, tm, QK_CAT), lambda i: (0, i, 0))
        q_shape = jax.ShapeDtypeStruct((MLA_HEADS, rows, QK_CAT), qdtype)
        extra_specs, extra_shapes = [], []
    return pl.pallas_call(
        functools.partial(_pre_kernel, q_transposed=q_transposed),
        grid=(rows // tm,),
        in_specs=[
            row(D_MODEL),
            pl.BlockSpec((tm, ROPE_PAD), tab_map),
            pl.BlockSpec((tm, ROPE_PAD), tab_map),
            pl.BlockSpec((ROPE_PAD, tm), tab_map_t),
            pl.BlockSpec((ROPE_PAD, tm), tab_map_t),
            _const_spec((1, D_MODEL)),
            _const_spec((D_MODEL, IN_PAD)),
            _const_spec((1, Q_RANK)),
            _const_spec(wuq.shape),
            _const_spec(wuk.shape),
            _const_spec((1, KV_RANK)),
        ],
        out_specs=[row(S5_W), pl.BlockSpec((S5_NOCT, S5_CHUNK, tm // S5_CHUNK, LANE), lambda i: (0, 0, i, 0)),
                   q_spec, row(QK_CAT), row(KV_RANK), row(QK_ROPE)] + extra_specs,
        out_shape=[
            jax.ShapeDtypeStruct((rows, S5_W), F32),
            jax.ShapeDtypeStruct((S5_NOCT, S5_CHUNK, rows // S5_CHUNK, LANE), BF16),
            q_shape,
            jax.ShapeDtypeStruct((rows, QK_CAT), qdtype),
            jax.ShapeDtypeStruct((rows, KV_RANK), F32),
            jax.ShapeDtypeStruct((rows, QK_ROPE), F32),
        ] + extra_shapes,
        scratch_shapes=[pltpu.VMEM((S5_NOCT, tm, LANE), F32)],
        compiler_params=_params("arbitrary"),
        name="pre",
    )(x2d, cos, sin, cos.T, sin.T, w["g_mix"], w["w_in"], w["g_q"], wuq, wuk, w["g_kv"])


def _chunk_inputs(ut_ref):
    return jnp.concatenate([ut_ref[s] for s in range(S5_CHUNK)], axis=1)


def _s5_state_in_kernel(ut_ref, wre_ref, wim_ref, sre_ref, sim_ref):
    u = _chunk_inputs(ut_ref)
    sre_ref[...] = _dot(u, wre_ref[0])
    sim_ref[...] = _dot(u, wim_ref[0])


def _s5_scan_kernel(sre_ref, sim_ref, h0re_ref, h0im_ref, are_ref, aim_ref, hre_ref, him_ref, fre_ref, fim_ref):
    nb, nc, width = sre_ref.shape
    ar = are_ref[...].reshape(1, 1, width)
    ai = aim_ref[...].reshape(1, 1, width)

    def body(c, carry):
        hr, hi = carry
        hre_ref[:, pl.ds(c, 1), :] = hr
        him_ref[:, pl.ds(c, 1), :] = hi
        sr = sre_ref[:, pl.ds(c, 1), :]
        si = sim_ref[:, pl.ds(c, 1), :]
        return ar * hr - ai * hi + sr, ar * hi + ai * hr + si

    hr, hi = lax.fori_loop(0, nc, body, (h0re_ref[...], h0im_ref[...]))
    fre_ref[...] = hr
    fim_ref[...] = hi


def _s5_out_kernel(ut_ref, hre_ref, him_ref, t_ref, mre_ref, mim_ref, yt_ref):
    y = (_dot(_chunk_inputs(ut_ref), t_ref[0]) + _dot(hre_ref[...].astype(BF16), mre_ref[0])
         + _dot(him_ref[...].astype(BF16), mim_ref[0]))
    for t in range(S5_CHUNK):
        yt_ref[t] = y[:, t * LANE:(t + 1) * LANE]


def _s5_mixer(ut, h0_re, h0_im, ops, nb):
    rows = ut.shape[2]
    nc = rows // nb
    tr = min(rows, 1024)
    assert rows % tr == 0
    ow = S5_OCT * S5_STATE
    cw = S5_CHUNK * LANE
    ut_spec = pl.BlockSpec((None, S5_CHUNK, tr, LANE), lambda q, r: (q, 0, r, 0))
    st_spec = pl.BlockSpec((tr, ow), lambda q, r: (r, q))
    oct_spec = lambda a, b: pl.BlockSpec((1, a, b), lambda q, r: (q, 0, 0))
    s_re, s_im = pl.pallas_call(
        _s5_state_in_kernel,
        grid=(S5_NOCT, rows // tr),
        in_specs=[ut_spec, oct_spec(cw, ow), oct_spec(cw, ow)],
        out_specs=[st_spec, st_spec],
        out_shape=[jax.ShapeDtypeStruct((rows, S5_STATE_W), F32)] * 2,
        compiler_params=_params("arbitrary", "arbitrary"),
        name="s5_state_in",
    )(ut, ops["w_re"], ops["w_im"])

    blk = pl.BlockSpec((nb, nc, S5_SCAN_W), lambda j: (0, 0, j))
    one = pl.BlockSpec((nb, 1, S5_SCAN_W), lambda j: (0, 0, j))
    vec = pl.BlockSpec((1, S5_SCAN_W), lambda j: (0, j))
    h_re, h_im, f_re, f_im = pl.pallas_call(
        _s5_scan_kernel,
        grid=(S5_STATE_W // S5_SCAN_W,),
        in_specs=[blk, blk, one, one, vec, vec],
        out_specs=[blk, blk, one, one],
        out_shape=[jax.ShapeDtypeStruct((nb, nc, S5_STATE_W), F32)] * 2
                  + [jax.ShapeDtypeStruct((nb, 1, S5_STATE_W), F32)] * 2,
        compiler_params=_params("arbitrary"),
        name="s5_scan",
    )(s_re.reshape(nb, nc, S5_STATE_W), s_im.reshape(nb, nc, S5_STATE_W), h0_re, h0_im, ops["a_re"], ops["a_im"])

    yt = pl.pallas_call(
        _s5_out_kernel,
        grid=(S5_NOCT, rows // tr),
        in_specs=[ut_spec, st_spec, st_spec, oct_spec(cw, cw), oct_spec(ow, cw), oct_spec(ow, cw)],
        out_specs=ut_spec,
        out_shape=jax.ShapeDtypeStruct(ut.shape, F32),
        compiler_params=_params("arbitrary", "arbitrary"),
        name="s5_out",
    )(ut, h_re.reshape(rows, S5_STATE_W), h_im.reshape(rows, S5_STATE_W), ops["t"], ops["m_re"], ops["m_im"])
    return yt, f_re, f_im


def _s5_operator_kernel(btr_ref, bti_ref, cr_ref, ci_ref, pl_re, pl_im, pr_re, pr_im,
                        t_ref, wre_ref, wim_ref, mre_ref, mim_ref):
    btr, bti = btr_ref[0], bti_ref[0]
    cr, ci = cr_ref[0], ci_ref[0]
    hi_dot = lambda a, b: jnp.dot(a, b, precision=HI, preferred_element_type=F32)
    kerns = []
    for k in range(S5_CHUNK + 1):
        ar, ai = pr_re[0, :, k:k + 1], pr_im[0, :, k:k + 1]
        cpr = cr * ar - ci * ai
        cpi = cr * ai + ci * ar
        if k < S5_CHUNK:
            kerns.append((hi_dot(btr, cpr) - hi_dot(bti, cpi)).astype(BF16))
            s = S5_CHUNK - 1 - k
            lr, li = pl_re[0, k:k + 1, :], pl_im[0, k:k + 1, :]
            wre_ref[0, s * LANE:(s + 1) * LANE, :] = (btr * lr - bti * li).astype(BF16)
            wim_ref[0, s * LANE:(s + 1) * LANE, :] = (btr * li + bti * lr).astype(BF16)
        if k >= 1:
            t = k - 1
            mre_ref[0, :, t * LANE:(t + 1) * LANE] = cpr.astype(BF16)
            mim_ref[0, :, t * LANE:(t + 1) * LANE] = (-cpi).astype(BF16)
    zero = jnp.zeros((LANE, LANE), BF16)
    for s in range(S5_CHUNK):
        for t in range(S5_CHUNK):
            t_ref[0, s * LANE:(s + 1) * LANE, t * LANE:(t + 1) * LANE] = kerns[t - s] if t >= s else zero


def _s5_operators(a_re, a_im, log_dt, b_re, b_im, c_re, c_im):
    steps = S5_CHUNK
    dt = jnp.exp(log_dt)[:, None]
    k = jnp.arange(steps + 1, dtype=F32)[:, None, None]
    mag = jnp.exp(k * (dt * a_re)[None])
    pw_r = mag * jnp.cos(k * (dt * a_im)[None])
    pw_i = mag * jnp.sin(k * (dt * a_im)[None])
    abr, abi = pw_r[1], pw_i[1]
    num_re, num_im = abr - 1.0, abi
    den = a_re * a_re + a_im * a_im
    f_re = (num_re * a_re + num_im * a_im) / den
    f_im = (num_im * a_re - num_re * a_im) / den
    bbr = f_re[..., None] * b_re - f_im[..., None] * b_im
    bbi = f_re[..., None] * b_im + f_im[..., None] * b_re
    same_group = jnp.arange(S5_OCT)[:, None, None, None] == jnp.arange(S5_OCT)[None, None, :, None]
    ow = S5_OCT * S5_STATE

    def block_diag(a):
        a = a.reshape(S5_NOCT, S5_OCT, a.shape[1], 1, a.shape[2])
        return jnp.where(same_group, a, 0.0).reshape(S5_NOCT, S5_OCT * a.shape[2], S5_OCT * a.shape[4])

    pad_k = S5_POW_PAD - (steps + 1)
    lanes = lambda p: jnp.pad(p.reshape(steps + 1, S5_NOCT, ow).transpose(1, 0, 2), ((0, 0), (0, pad_k), (0, 0)))
    rows = lambda p: jnp.pad(p.reshape(steps + 1, S5_NOCT, ow).transpose(1, 2, 0), ((0, 0), (0, 0), (0, pad_k)))
    bt_re, bt_im = block_diag(bbr.transpose(0, 2, 1)), block_diag(bbi.transpose(0, 2, 1))
    cb_re, cb_im = block_diag(c_re.transpose(0, 2, 1)), block_diag(c_im.transpose(0, 2, 1))
    cw = steps * LANE
    oct_spec = lambda a, b: pl.BlockSpec((1, a, b), lambda q: (q, 0, 0))
    t_mat, w_re, w_im, m_re, m_im = pl.pallas_call(
        _s5_operator_kernel,
        grid=(S5_NOCT,),
        in_specs=[oct_spec(LANE, ow), oct_spec(LANE, ow), oct_spec(ow, LANE), oct_spec(ow, LANE),
                  oct_spec(S5_POW_PAD, ow), oct_spec(S5_POW_PAD, ow), oct_spec(ow, S5_POW_PAD),
                  oct_spec(ow, S5_POW_PAD)],
        out_specs=[oct_spec(cw, cw), oct_spec(cw, ow), oct_spec(cw, ow), oct_spec(ow, cw), oct_spec(ow, cw)],
        out_shape=[jax.ShapeDtypeStruct((S5_NOCT, cw, cw), BF16)]
                  + [jax.ShapeDtypeStruct((S5_NOCT, cw, ow), BF16)] * 2
                  + [jax.ShapeDtypeStruct((S5_NOCT, ow, cw), BF16)] * 2,
        compiler_params=_params("arbitrary"),
        name="s5_operators",
    )(bt_re, bt_im, cb_re, cb_im, lanes(pw_r), lanes(pw_i), rows(pw_r), rows(pw_i))
    return {
        "t": t_mat, "w_re": w_re, "w_im": w_im, "m_re": m_re, "m_im": m_im,
        "a_re": pw_r[steps].reshape(1, S5_STATE_W), "a_im": pw_i[steps].reshape(1, S5_STATE_W),
    }


def _softmax_update_t(s, vt, m_ref, l_ref, acc_ref):
    m_prev = m_ref[...]
    m_new = jnp.maximum(m_prev, jnp.max(s, axis=0, keepdims=True))
    alpha = jnp.exp2(m_prev - m_new)
    p = jnp.exp2(s - m_new)
    l_ref[...] = alpha * l_ref[...] + jnp.sum(p, axis=0, keepdims=True)
    acc_ref[...] = alpha * acc_ref[...] + _dot(vt, p.astype(BF16))
    m_ref[...] = m_new


def _attn_prompt_kernel(qt_ref, k_ref, vt_ref, kmeta_ref, vmeta_t_ref, wuvt_ref, o_ref, m_ref, l_ref, acc_ref, s_ref,
                        *, tq):
    qi = pl.program_id(1)
    qt = jnp.concatenate([qt_ref[h] for h in range(MLA_HEADS)], axis=1)

    def keys(j):
        return pl.ds(pl.multiple_of(j * tq, tq), tq)

    def scores(j):
        return _dot(k_ref[0, keys(j), :], qt)

    s = _dot(kmeta_ref[...], qt)
    s = jnp.where(lax.broadcasted_iota(jnp.int32, s.shape, 0) < N_META, s, NEG_INF)
    m0 = jnp.max(s, axis=0, keepdims=True)
    p = jnp.exp2(s - m0)
    m_ref[...] = m0
    l_ref[...] = jnp.sum(p, axis=0, keepdims=True)
    acc_ref[...] = _dot(vmeta_t_ref[...], p.astype(BF16))

    s_ref[...] = scores(0)

    def body(kj, carry):
        s_next = scores(kj + 1)
        _softmax_update_t(s_ref[...], vt_ref[:, keys(kj)], m_ref, l_ref, acc_ref)
        s_ref[...] = s_next
        return carry

    lax.fori_loop(0, qi, body, 0)

    s = s_ref[...]
    t_k = lax.broadcasted_iota(jnp.int32, s.shape, 0)
    t_q = lax.broadcasted_iota(jnp.int32, s.shape, 1) % tq
    s = jnp.where(t_k <= t_q, s, NEG_INF)
    _softmax_update_t(s, vt_ref[:, keys(qi)], m_ref, l_ref, acc_ref)

    ot = (acc_ref[...] / l_ref[...]).astype(BF16)
    for h in range(MLA_HEADS):
        yt = _dot(wuvt_ref[h], ot[:, h * tq:(h + 1) * tq])
        o_ref[:, h * V_HEAD:(h + 1) * V_HEAD] = yt.T.astype(o_ref.dtype)


def _attn_prompt(qt, kcat, vt, kmeta, vmeta_t, w_uv_t, nb, tq):
    t = kcat.shape[1]
    nq = t // tq
    cols = MLA_HEADS * tq
    return pl.pallas_call(
        functools.partial(_attn_prompt_kernel, tq=tq),
        grid=(nb, nq),
        in_specs=[
            pl.BlockSpec((MLA_HEADS, QK_CAT, tq), lambda b, i: (0, 0, b * nq + i)),
            pl.BlockSpec((1, t, QK_CAT), lambda b, i: (b, 0, 0)),
            pl.BlockSpec((KV_RANK, t), lambda b, i: (0, b)),
            _const_spec(kmeta.shape),
            _const_spec(vmeta_t.shape),
            _const_spec(w_uv_t.shape),
        ],
        out_specs=pl.BlockSpec((tq, MLA_HEADS * V_HEAD), lambda b, i: (b * nq + i, 0)),
        out_shape=jax.ShapeDtypeStruct((nb * t, MLA_HEADS * V_HEAD), BF16),
        scratch_shapes=[pltpu.VMEM((1, cols), F32), pltpu.VMEM((1, cols), F32), pltpu.VMEM((KV_RANK, cols), F32),
                        pltpu.VMEM((tq, cols), F32)],
        compiler_params=_params("arbitrary", "arbitrary"),
        name="attn_prompt",
    )(qt, kcat, vt, kmeta, vmeta_t, w_uv_t)


def _attn_meta_kernel(q_ref, kmeta_ref, wuv_ref, o_ref):
    rows = MLA_HEADS * N_META
    q = q_ref[...].reshape(rows, QK_CAT)
    km = kmeta_ref[...]
    s = _dot_nt(q, km)
    t_q = lax.broadcasted_iota(jnp.int32, s.shape, 0) % N_META
    t_k = lax.broadcasted_iota(jnp.int32, s.shape, 1)
    s = jnp.where(t_k <= t_q, s, NEG_INF)
    p = jnp.exp(s - jnp.max(s, axis=1, keepdims=True))
    o = _dot(p.astype(BF16), km[:, :KV_RANK]) / jnp.sum(p, axis=1, keepdims=True)
    o = o.astype(BF16)
    for h in range(MLA_HEADS):
        o_ref[:, h * V_HEAD:(h + 1) * V_HEAD] = _dot(o[h * N_META:(h + 1) * N_META], wuv_ref[h]).astype(o_ref.dtype)


def _attn_meta(q, kmeta, w_uv):
    return pl.pallas_call(
        _attn_meta_kernel,
        out_shape=jax.ShapeDtypeStruct((N_META, MLA_HEADS * V_HEAD), BF16),
        name="attn_meta",
    )(q, kmeta, w_uv)


def _attn_sample_kernel(pt_ref, q_ref, knew_ref, wuv_ref, ckv_hbm, kr_hbm, o_ref, ckv_buf, kr_buf, sem,
                        *, n_pages, steps):
    b = pl.program_id(0)
    slot = b % 2

    def page_copies(seq, into, j):
        page = pt_ref[seq, j]
        return (pltpu.make_async_copy(ckv_hbm.at[0, page], ckv_buf.at[into, j], sem.at[into, 0]),
                pltpu.make_async_copy(kr_hbm.at[0, page], kr_buf.at[into, j], sem.at[into, 1]))

    def for_each_page(seq, into, act):
        def body(j, carry):
            for copy in page_copies(seq, into, j):
                act(copy)
            return carry
        lax.fori_loop(0, n_pages, body, 0)

    @pl.when(b == 0)
    def _():
        for_each_page(0, 0, lambda copy: copy.start())

    @pl.when(b + 1 < pl.num_programs(0))
    def _():
        for_each_page(b + 1, 1 - slot, lambda copy: copy.start())

    for_each_page(b, slot, lambda copy: copy.wait())

    rows = MLA_HEADS * steps
    q = q_ref[...].reshape(rows, QK_CAT).astype(BF16)
    q_lat = q[:, :KV_RANK]
    q_rope = q[:, KV_RANK:KV_RANK + QK_ROPE]
    groups = range(0, n_pages, PAGE_GROUP)

    def latent_keys(j0):
        return jnp.concatenate([ckv_buf[slot, j].astype(BF16) for j in range(j0, j0 + PAGE_GROUP)], axis=0)

    scores = []
    for j0 in groups:
        kr_t = jnp.concatenate([kr_buf[slot, j].astype(BF16) for j in range(j0, j0 + PAGE_GROUP)], axis=1)
        scores.append(_dot_nt(q_lat, latent_keys(j0)) + _dot(q_rope, kr_t))
    kn = jnp.concatenate([knew_ref[...], jnp.zeros((PAGE_SIZE - steps, QK_CAT), F32)], axis=0).astype(BF16)
    s_new = _dot_nt(q, kn)
    t_q = lax.broadcasted_iota(jnp.int32, s_new.shape, 0) % steps
    t_k = lax.broadcasted_iota(jnp.int32, s_new.shape, 1)
    scores.append(jnp.where(t_k <= t_q, s_new, NEG_INF))

    s = jnp.concatenate(scores, axis=1)
    p = jnp.exp(s - jnp.max(s, axis=1, keepdims=True))
    l = jnp.sum(p, axis=1, keepdims=True)
    p = p.astype(BF16)
    gk = PAGE_GROUP * PAGE_SIZE
    acc = [_dot(p[:, n_pages * PAGE_SIZE:], kn[:, :KV_RANK]), jnp.zeros((rows, KV_RANK), F32)]
    for i, j0 in enumerate(groups):
        acc[i % 2] = acc[i % 2] + _dot(p[:, i * gk:(i + 1) * gk], latent_keys(j0))
    o = ((acc[0] + acc[1]) / l).astype(BF16)
    for h in range(MLA_HEADS):
        o_ref[:, h * V_HEAD:(h + 1) * V_HEAD] = _dot(o[h * steps:(h + 1) * steps], wuv_ref[h])


def _attn_sample(page_table, q, knew, cache_ckv, cache_kr_t, w_uv, steps):
    nb, n_pages = page_table.shape
    assert n_pages % PAGE_GROUP == 0
    grid_spec = pltpu.PrefetchScalarGridSpec(
        num_scalar_prefetch=1,
        grid=(nb,),
        in_specs=[pl.BlockSpec((MLA_HEADS, steps, QK_CAT), lambda b, pt: (0, b, 0)),
                  pl.BlockSpec((steps, QK_CAT), lambda b, pt: (b, 0)),
                  pl.BlockSpec(w_uv.shape, lambda b, pt: (0, 0, 0)),
                  pl.BlockSpec(memory_space=pl.ANY),
                  pl.BlockSpec(memory_space=pl.ANY)],
        out_specs=pl.BlockSpec((steps, MLA_HEADS * V_HEAD), lambda b, pt: (b, 0)),
        scratch_shapes=[pltpu.VMEM((2, n_pages, PAGE_SIZE, KV_RANK), F32),
                        pltpu.VMEM((2, n_pages, QK_ROPE, PAGE_SIZE), F32),
                        pltpu.SemaphoreType.DMA((2, 2))],
    )
    return pl.pallas_call(
        functools.partial(_attn_sample_kernel, n_pages=n_pages, steps=steps),
        grid_spec=grid_spec,
        out_shape=jax.ShapeDtypeStruct((nb * steps, MLA_HEADS * V_HEAD), F32),
        compiler_params=_params("arbitrary"),
        name="attn_sample",
    )(page_table, q, knew, w_uv, cache_ckv, cache_kr_t)


def _gelu_tanh(x):
    return 0.5 * x * (1.0 + jnp.tanh(math.sqrt(2.0 / math.pi) * (x + 0.044715 * (x * x * x))))


def _sigmoid(x):
    return 1.0 / (1.0 + jnp.exp(-x))


def _post_kernel(x_ref, yt_ref, u_ref, ymla_ref, p1_ref, p2_ref, d_ref, wglu_ref, wout_ref, gffn_ref,
                 wgate_ref, wup_ref, convw_ref, convb_ref, wdown_ref, gfin_ref,
                 y_ref, gate_ref, ybuf, hbuf, carry_ref, *, tm, seq_steps):
    long_seq = seq_steps == 0
    for o in range(S5_NOCT):
        for t in range(S5_CHUNK):
            ybuf[o, pl.ds(t, tm // S5_CHUNK, stride=S5_CHUNK), :] = yt_ref[o, t]
    y_ssm = jnp.concatenate([ybuf[o] for o in range(S5_NOCT)], axis=1)
    y = y_ssm + d_ref[...] * u_ref[...]
    y = _gelu_tanh(y)
    y = y * _sigmoid(_dot(y.astype(BF16), wglu_ref[...]))
    mixed = _dot(y.astype(BF16), wout_ref[:S5_W, :]) + _dot(ymla_ref[...].astype(BF16), wout_ref[S5_W:, :])
    x1 = x_ref[...] + mixed
    xn = _rms(x1, gffn_ref[...]).astype(BF16)

    row = lax.broadcasted_iota(jnp.int32, (tm, FF_CHUNK), 0)
    if long_seq:
        @pl.when(pl.program_id(1) == 0)
        def _():
            carry_ref[...] = p1_ref[...]
        t_in = row
    else:
        t_in = row % seq_steps

    for c in range(D_FF // FF_CHUNK):
        sl = slice(c * FF_CHUNK, (c + 1) * FF_CHUNK)
        gate = _dot(xn, wgate_ref[:, sl])
        up = _dot(xn, wup_ref[:, sl])
        if long_seq:
            back1 = jnp.broadcast_to(carry_ref[7:8, sl], gate.shape)
            back2 = jnp.where(t_in == 0, carry_ref[6:7, sl], back1)
            carry_ref[:, sl] = gate[tm - 8:, :]
            gate_ref[:, sl] = gate[tm - 8:, :]
        else:
            back1 = p1_ref[:, sl]
            back2 = p2_ref[:, sl]
            gate_ref[:, sl] = gate
        prev1 = jnp.where(t_in >= 1, pltpu.roll(gate, 1, 0), back1)
        prev2 = jnp.where(t_in >= 2, pltpu.roll(gate, 2, 0), back2)
        conv = (convb_ref[:, sl] + convw_ref[0:1, sl] * prev2 + convw_ref[1:2, sl] * prev1
                + convw_ref[2:3, sl] * gate)
        hbuf[:, sl] = (conv * _sigmoid(conv) * up).astype(BF16)

    x2 = x1 + _dot(hbuf[...], wdown_ref[...])
    y_ref[...] = _rms(x2, gfin_ref[...])


def _post_call(x2d, yt, u, y_mla, p1, p2, w, nb, tm, seq_steps):
    rows = x2d.shape[0]
    nt = rows // (nb * tm)
    assert nb * nt * tm == rows and tm % S5_CHUNK == 0
    long_seq = seq_steps == 0
    row = lambda width: pl.BlockSpec((tm, width), lambda b, i: (b * nt + i, 0))
    yt_spec = pl.BlockSpec((S5_NOCT, S5_CHUNK, tm // S5_CHUNK, LANE), lambda b, i: (0, 0, b * nt + i, 0))
    if long_seq:
        hist = [_const_spec((8, D_FF)), _const_spec((8, D_FF))]
        gate_spec = pl.BlockSpec((None, 8, D_FF), lambda b, i: (b, 0, 0))
        gate_shape = jax.ShapeDtypeStruct((nb, 8, D_FF), F32)
    else:
        hist = [row(D_FF), row(D_FF)]
        gate_spec = row(D_FF)
        gate_shape = jax.ShapeDtypeStruct((rows, D_FF), F32)
    return pl.pallas_call(
        functools.partial(_post_kernel, tm=tm, seq_steps=seq_steps),
        grid=(nb, nt),
        in_specs=[row(D_MODEL), yt_spec, row(S5_W), row(MLA_HEADS * V_HEAD)] + hist + [
            _const_spec((1, S5_W)), _const_spec((S5_W, S5_W)), _const_spec((D_MODEL, D_MODEL)),
            _const_spec((1, D_MODEL)), _const_spec((D_MODEL, D_FF)), _const_spec((D_MODEL, D_FF)),
            _const_spec((CONV_W, D_FF)), _const_spec((1, D_FF)), _const_spec((D_FF, D_MODEL)),
            _const_spec((1, D_MODEL))],
        out_specs=[row(D_MODEL), gate_spec],
        out_shape=[jax.ShapeDtypeStruct((rows, D_MODEL), F32), gate_shape],
        scratch_shapes=[pltpu.VMEM((S5_NOCT, tm, LANE), F32), pltpu.VMEM((tm, D_FF), BF16),
                        pltpu.VMEM((8, D_FF), F32)],
        compiler_params=_params("arbitrary", "arbitrary"),
        name="post",
    )(x2d, yt, u, y_mla, p1, p2, w["s5_d"], w["w_glu"], w["w_out"], w["g_ffn"], w["w_gate"], w["w_up"],
      w["conv_w"], w["conv_b"], w["w_down"], w["g_final"])


def _rope_tables(pos):
    half = QK_ROPE // 2
    inv = ROPE_BASE ** (-jnp.arange(half, dtype=F32) / half)
    ang = pos.astype(F32)[:, None] * inv[None, :]
    cos, sin = jnp.cos(ang), jnp.sin(ang)
    pad = jnp.zeros((pos.shape[0], ROPE_PAD - QK_ROPE), F32)
    return jnp.concatenate([cos, cos, pad], axis=1), jnp.concatenate([-sin, sin, pad], axis=1)


def kernel(x_prompt, x_sample, cache_ckv, cache_kr, state_s5_re, state_s5_im, state_conv, page_table, meta_tokens, g_mix, w_in, g_q, w_uq, g_kv, w_uk, w_uv, s5_a_re, s5_a_im, s5_log_dt, s5_b_re, s5_b_im, s5_c_re, s5_c_im, s5_d, w_glu, w_out, g_ffn, w_gate, w_up, conv_w, conv_b, w_down, g_final):
    assert w_in.shape[0] == 1, "single-layer step"
    nb, seq, _ = x_prompt.shape
    db, steps, _ = x_sample.shape
    n_past = page_table.shape[1] * PAGE_SIZE
    assert seq % S5_CHUNK == 0 and N_META % S5_CHUNK == 0 and steps == S5_CHUNK

    w_in_pad = jnp.concatenate([w_in[0], jnp.zeros((D_MODEL, IN_PAD - w_in.shape[2]), F32)], axis=1)
    wq = w_uq[0].reshape(Q_RANK, MLA_HEADS, QK_NOPE + QK_ROPE)
    wq_rope = jnp.concatenate([wq[:, :, QK_NOPE:], jnp.zeros((Q_RANK, MLA_HEADS, ROPE_PAD - QK_ROPE), F32)], axis=2)
    wq_perm = jnp.concatenate([wq[:, :, :QK_NOPE].reshape(Q_RANK, -1), wq_rope.reshape(Q_RANK, -1)], axis=1)
    w = {
        "g_mix": g_mix[0].reshape(1, -1), "w_in": w_in_pad.astype(BF16),
        "g_q": g_q[0].reshape(1, -1), "w_uq": wq_perm.astype(BF16), "w_uq_t": wq_perm.T.astype(BF16),
        "w_uk_t": w_uk[0].transpose(1, 2, 0).astype(BF16),
        "w_uk": w_uk[0].transpose(1, 0, 2).astype(BF16),
        "g_kv": g_kv[0].reshape(1, -1),
        "s5_d": s5_d[0].reshape(1, -1), "w_glu": w_glu[0].astype(BF16), "w_out": w_out[0].astype(BF16),
        "g_ffn": g_ffn[0].reshape(1, -1), "w_gate": w_gate[0].astype(BF16), "w_up": w_up[0].astype(BF16),
        "conv_w": conv_w[0], "conv_b": conv_b[0].reshape(1, -1), "w_down": w_down[0].astype(BF16),
        "g_final": g_final.reshape(1, -1),
    }
    w_uv_h = w_uv[0].transpose(1, 0, 2).astype(BF16)
    ops = _s5_operators(s5_a_re[0], s5_a_im[0], s5_log_dt[0], s5_b_re[0], s5_b_im[0], s5_c_re[0], s5_c_im[0])

    cos_m, sin_m = _rope_tables(jnp.arange(N_META, dtype=jnp.int32))
    cos_p, sin_p = _rope_tables(N_META + jnp.arange(seq, dtype=jnp.int32))
    tm_p = 512
    tm_s = 512
    cos_s, sin_s = _rope_tables(n_past + jnp.arange(tm_s, dtype=jnp.int32) % steps)

    tiles_per_seq = seq // tm_p
    u_m, ut_m, q_m, kcat_m, ckv_m, kr_m = _pre_call(meta_tokens, cos_m, sin_m, lambda i: (0, 0), N_META, BF16, w)
    u_p, ut_p, qt_p, kcat_p, ckv_p, kr_p, vt_p = _pre_call(
        x_prompt.reshape(nb * seq, D_MODEL), cos_p, sin_p, lambda i: (i % tiles_per_seq, 0), tm_p, BF16, w,
        q_transposed=True)
    u_s, ut_s, q_s, kcat_s, ckv_s, kr_s = _pre_call(x_sample.reshape(db * steps, D_MODEL), cos_s, sin_s,
                                                    lambda i: (0, 0), tm_s, F32, w)

    zero_state = jnp.zeros((1, 1, S5_STATE_W), F32)
    yt_m, fm_re, fm_im = _s5_mixer(ut_m, zero_state, zero_state, ops, 1)
    yt_p, s5_re_p, s5_im_p = _s5_mixer(ut_p, jnp.broadcast_to(fm_re, (nb, 1, S5_STATE_W)),
                                       jnp.broadcast_to(fm_im, (nb, 1, S5_STATE_W)), ops, nb)
    yt_s, s5_re_s, s5_im_s = _s5_mixer(ut_s, state_s5_re[0].reshape(db, 1, S5_STATE_W),
                                       state_s5_im[0].reshape(db, 1, S5_STATE_W), ops, db)

    kmeta = jnp.concatenate([kcat_m, jnp.zeros((PAGE_SIZE - N_META, QK_CAT), BF16)], axis=0)
    ymla_m = _attn_meta(q_m, kmeta, w_uv_h)
    ymla_p = _attn_prompt(qt_p, kcat_p.reshape(nb, seq, QK_CAT), vt_p, kmeta, kmeta[:, :KV_RANK].T,
                          w_uv[0].transpose(1, 2, 0).astype(BF16), nb, tq=256)
    ymla_s = _attn_sample(page_table, q_s, kcat_s, cache_ckv, jnp.swapaxes(cache_kr, 2, 3), w_uv_h, steps)

    zeros_hist = jnp.zeros((8, D_FF), F32)
    _, gate_m = _post_call(meta_tokens, yt_m, u_m, ymla_m, zeros_hist, zeros_hist, w, 1, N_META, 0)
    y_p, gate_p = _post_call(x_prompt.reshape(nb * seq, D_MODEL), yt_p, u_p, ymla_p, gate_m[0], gate_m[0],
                             w, nb, tm_p, 0)
    sc = state_conv[0]
    back1 = jnp.pad(sc[:, 1:2], ((0, 0), (0, steps - 1), (0, 0))).reshape(db * steps, D_FF)
    back2 = jnp.pad(sc, ((0, 0), (0, steps - 2), (0, 0))).reshape(db * steps, D_FF)
    y_s, gate_s = _post_call(x_sample.reshape(db * steps, D_MODEL), yt_s, u_s, ymla_s, back1, back2,
                             w, 1, tm_s // 2, steps)

    def with_meta(meta_rows, tok_rows, width):
        return jnp.concatenate([jnp.broadcast_to(meta_rows[None], (nb, N_META, width)),
                                tok_rows.reshape(nb, seq, width)], axis=1)[None]

    return (y_p.reshape(nb, seq, D_MODEL), y_s.reshape(db, steps, D_MODEL),
            with_meta(ckv_m, ckv_p, KV_RANK), with_meta(kr_m, kr_p, QK_ROPE),
            s5_re_p.reshape(1, nb, S5_GROUPS, S5_STATE), s5_im_p.reshape(1, nb, S5_GROUPS, S5_STATE),
            gate_p[:, 8 - (CONV_W - 1):][None],
            ckv_s.reshape(1, db, steps, KV_RANK), kr_s.reshape(1, db, steps, QK_ROPE),
            s5_re_s.reshape(1, db, S5_GROUPS, S5_STATE), s5_im_s.reshape(1, db, S5_GROUPS, S5_STATE),
            gate_s.reshape(db, steps, D_FF)[:, steps - (CONV_W - 1):][None])
```

```python
import functools
import math

import jax
import jax.numpy as jnp
from jax import lax
from jax.experimental import pallas as pl
from jax.experimental.pallas import tpu as pltpu

F32 = jnp.float32
BF16 = jnp.bfloat16

D_MODEL = 1024
N_META = 16
S5_W = 512
S5_GROUP = 16
S5_GROUPS = 32
S5_STATE = 64
MLA_HEADS = 4
QK_NOPE = 128
QK_ROPE = 64
V_HEAD = 128
Q_RANK = 384
KV_RANK = 256
D_FF = 2816
CONV_W = 3
ROPE_BASE = 10000.0
EPS = 1e-6
PAGE_SIZE = 128
ATTN_SCALE = 1.0 / math.sqrt(QK_NOPE + QK_ROPE)
LOG2_E = math.log2(math.e)

LANE = 128
ROPE_PAD = LANE
QK_CAT = KV_RANK + ROPE_PAD
IN_PAD = S5_W + Q_RANK + KV_RANK + ROPE_PAD
S5_CHUNK = 8
S5_OCT = LANE // S5_GROUP
S5_NOCT = S5_GROUPS // S5_OCT
S5_STATE_W = S5_GROUPS * S5_STATE
S5_SCAN_W = 256
S5_POW_PAD = 16
FF_CHUNK = 256
PAGE_GROUP = 4
CHUNK_GROUPS = 4
VMEM_LIMIT = 56 * 1024 * 1024
NEG_INF = float("-inf")
HI = lax.Precision.HIGHEST


def _const_spec(shape):
    nd = len(shape)
    return pl.BlockSpec(shape, lambda *_: (0,) * nd, pipeline_mode=pl.Buffered(1))


def _params(*semantics):
    return pltpu.CompilerParams(dimension_semantics=semantics, vmem_limit_bytes=VMEM_LIMIT)


def _rms(x, g):
    return x * lax.rsqrt(jnp.mean(x * x, axis=-1, keepdims=True) + EPS) * g


def _rope_slab(x, cos, sin):
    lane = lax.broadcasted_iota(jnp.int32, x.shape, 1)
    half = QK_ROPE // 2
    swapped = jnp.where(lane < half, pltpu.roll(x, LANE - half, 1), pltpu.roll(x, half, 1))
    return x * cos + swapped * sin


def _dot(a, b):
    return jnp.dot(a, b, preferred_element_type=F32)


def _dot_nt(a, b):
    return lax.dot_general(a, b, (((1,), (1,)), ((), ())), preferred_element_type=F32)


def _pre_kernel(x_ref, cos_ref, sin_ref, cost_ref, sint_ref, gmix_ref, win_ref, gq_ref, wuq_ref, wuk_ref, gkv_ref,
                u_ref, ut_ref, q_ref, kcat_ref, ckv_ref, kr_ref, *rest, q_transposed):
    x = x_ref[...]
    xn = _rms(x, gmix_ref[...]).astype(BF16)
    z = _dot(xn, win_ref[...])
    u_ref[...] = z[:, :S5_W]
    slab_ref = rest[-1]
    chunks = u_ref.shape[0] // S5_CHUNK
    for o in range(S5_NOCT):
        slab_ref[o] = z[:, o * LANE:(o + 1) * LANE]
        for s in range(S5_CHUNK):
            ut_ref[o, s] = slab_ref[o, pl.ds(s, chunks, stride=S5_CHUNK), :].astype(BF16)
    cq = z[:, S5_W:S5_W + Q_RANK]
    ckv_raw = z[:, S5_W + Q_RANK:S5_W + Q_RANK + KV_RANK]
    kr_raw = z[:, S5_W + Q_RANK + KV_RANK:]
    cqn = _rms(cq, gq_ref[...]).astype(BF16)
    nope_w = MLA_HEADS * QK_NOPE
    half = QK_ROPE // 2
    if q_transposed:
        qt = _dot_nt(wuq_ref[...], cqn)
        cost = cost_ref[...]
        sint = sint_ref[...]
        for h in range(MLA_HEADS):
            qn = qt[h * QK_NOPE:(h + 1) * QK_NOPE].astype(BF16)
            q_lat = _dot(wuk_ref[h], qn) * (ATTN_SCALE * LOG2_E)
            xr = qt[nope_w + h * ROPE_PAD:nope_w + (h + 1) * ROPE_PAD]
            swapped = jnp.concatenate([xr[half:QK_ROPE], xr[:half], xr[QK_ROPE:]], axis=0)
            qr = (xr * cost + swapped * sint) * (ATTN_SCALE * LOG2_E)
            q_ref[h, :KV_RANK, :] = q_lat.astype(q_ref.dtype)
            q_ref[h, KV_RANK:, :] = qr.astype(q_ref.dtype)
    else:
        q = _dot(cqn, wuq_ref[...])
        for h in range(MLA_HEADS):
            qn = q[:, h * QK_NOPE:(h + 1) * QK_NOPE].astype(BF16)
            q_lat = _dot(qn, wuk_ref[h]) * ATTN_SCALE
            qr = _rope_slab(q[:, nope_w + h * ROPE_PAD:nope_w + (h + 1) * ROPE_PAD],
                            cos_ref[...], sin_ref[...]) * ATTN_SCALE
            q_ref[h, :, :KV_RANK] = q_lat.astype(q_ref.dtype)
            q_ref[h, :, KV_RANK:] = qr.astype(q_ref.dtype)
    ckv = _rms(ckv_raw, gkv_ref[...])
    kr = _rope_slab(kr_raw, cos_ref[...], sin_ref[...])
    ckv_ref[...] = ckv
    kr_ref[...] = kr[:, :QK_ROPE]
    kcat_ref[:, :KV_RANK] = ckv.astype(kcat_ref.dtype)
    kcat_ref[:, KV_RANK:] = kr.astype(kcat_ref.dtype)
    if q_transposed:
        rest[0][...] = ckv.T.astype(BF16)


def _pre_call(x2d, cos, sin, tab_map, tm, qdtype, w, q_transposed=False):
    rows = x2d.shape[0]
    assert rows % tm == 0 and tm % S5_CHUNK == 0
    row = lambda width: pl.BlockSpec((tm, width), lambda i: (i, 0))
    tab_map_t = lambda i: tab_map(i)[::-1]
    if q_transposed:
        wuq, wuk = w["w_uq_t"], w["w_uk"]
        q_spec = pl.BlockSpec((MLA_HEADS, QK_CAT, tm), lambda i: (0, 0, i))
        q_shape = jax.ShapeDtypeStruct((MLA_HEADS, QK_CAT, rows), qdtype)
        extra_specs = [pl.BlockSpec((KV_RANK, tm), lambda i: (0, i))]
        extra_shapes = [jax.ShapeDtypeStruct((KV_RANK, rows), BF16)]
    else:
        wuq, wuk = w["w_uq"], w["w_uk_t"]
        q_spec = pl.BlockSpec((MLA_HEADS, tm, QK_CAT), lambda i: (0, i, 0))
        q_shape = jax.ShapeDtypeStruct((MLA_HEADS, rows, QK_CAT), qdtype)
        extra_specs, extra_shapes = [], []
    return pl.pallas_call(
        functools.partial(_pre_kernel, q_transposed=q_transposed),
        grid=(rows // tm,),
        in_specs=[
            row(D_MODEL),
            pl.BlockSpec((tm, ROPE_PAD), tab_map),
            pl.BlockSpec((tm, ROPE_PAD), tab_map),
            pl.BlockSpec((ROPE_PAD, tm), tab_map_t),
            pl.BlockSpec((ROPE_PAD, tm), tab_map_t),
            _const_spec((1, D_MODEL)),
            _const_spec((D_MODEL, IN_PAD)),
            _const_spec((1, Q_RANK)),
            _const_spec(wuq.shape),
            _const_spec(wuk.shape),
            _const_spec((1, KV_RANK)),
        ],
        out_specs=[row(S5_W), pl.BlockSpec((S5_NOCT, S5_CHUNK, tm // S5_CHUNK, LANE), lambda i: (0, 0, i, 0)),
                   q_spec, row(QK_CAT), row(KV_RANK), row(QK_ROPE)] + extra_specs,
        out_shape=[
            jax.ShapeDtypeStruct((rows, S5_W), F32),
            jax.ShapeDtypeStruct((S5_NOCT, S5_CHUNK, rows // S5_CHUNK, LANE), BF16),
            q_shape,
            jax.ShapeDtypeStruct((rows, QK_CAT), qdtype),
            jax.ShapeDtypeStruct((rows, KV_RANK), F32),
            jax.ShapeDtypeStruct((rows, QK_ROPE), F32),
        ] + extra_shapes,
        scratch_shapes=[pltpu.VMEM((S5_NOCT, tm, LANE), F32)],
        compiler_params=_params("arbitrary"),
        name="pre",
    )(x2d, cos, sin, cos.T, sin.T, w["g_mix"], w["w_in"], w["g_q"], wuq, wuk, w["g_kv"])


def _chunk_inputs(ut_ref):
    return jnp.concatenate([ut_ref[s] for s in range(S5_CHUNK)], axis=1)


def _s5_state_in_kernel(ut_ref, wre_ref, wim_ref, sre_ref, sim_ref):
    u = _chunk_inputs(ut_ref)
    sre_ref[...] = _dot(u, wre_ref[0])
    sim_ref[...] = _dot(u, wim_ref[0])


def _s5_scan_kernel(sre_ref, sim_ref, h0re_ref, h0im_ref, are_ref, aim_ref, hre_ref, him_ref, fre_ref, fim_ref):
    nb, nc, width = sre_ref.shape
    ar = are_ref[...].reshape(1, 1, width)
    ai = aim_ref[...].reshape(1, 1, width)

    def body(c, carry):
        hr, hi = carry
        hre_ref[:, pl.ds(c, 1), :] = hr
        him_ref[:, pl.ds(c, 1), :] = hi
        sr = sre_ref[:, pl.ds(c, 1), :]
        si = sim_ref[:, pl.ds(c, 1), :]
        return ar * hr - ai * hi + sr, ar * hi + ai * hr + si

    hr, hi = lax.fori_loop(0, nc, body, (h0re_ref[...], h0im_ref[...]))
    fre_ref[...] = hr
    fim_ref[...] = hi


def _s5_out_kernel(ut_ref, hre_ref, him_ref, t_ref, mre_ref, mim_ref, yt_ref):
    y = (_dot(_chunk_inputs(ut_ref), t_ref[0]) + _dot(hre_ref[...].astype(BF16), mre_ref[0])
         + _dot(him_ref[...].astype(BF16), mim_ref[0]))
    for t in range(S5_CHUNK):
        yt_ref[t] = y[:, t * LANE:(t + 1) * LANE]


def _s5_mixer(ut, h0_re, h0_im, ops, nb):
    rows = ut.shape[2]
    nc = rows // nb
    tr = min(rows, 1024)
    assert rows % tr == 0
    ow = S5_OCT * S5_STATE
    cw = S5_CHUNK * LANE
    ut_spec = pl.BlockSpec((None, S5_CHUNK, tr, LANE), lambda q, r: (q, 0, r, 0))
    st_spec = pl.BlockSpec((tr, ow), lambda q, r: (r, q))
    oct_spec = lambda a, b: pl.BlockSpec((1, a, b), lambda q, r: (q, 0, 0))
    s_re, s_im = pl.pallas_call(
        _s5_state_in_kernel,
        grid=(S5_NOCT, rows // tr),
        in_specs=[ut_spec, oct_spec(cw, ow), oct_spec(cw, ow)],
        out_specs=[st_spec, st_spec],
        out_shape=[jax.ShapeDtypeStruct((rows, S5_STATE_W), F32)] * 2,
        compiler_params=_params("arbitrary", "arbitrary"),
        name="s5_state_in",
    )(ut, ops["w_re"], ops["w_im"])

    blk = pl.BlockSpec((nb, nc, S5_SCAN_W), lambda j: (0, 0, j))
    one = pl.BlockSpec((nb, 1, S5_SCAN_W), lambda j: (0, 0, j))
    vec = pl.BlockSpec((1, S5_SCAN_W), lambda j: (0, j))
    h_re, h_im, f_re, f_im = pl.pallas_call(
        _s5_scan_kernel,
        grid=(S5_STATE_W // S5_SCAN_W,),
        in_specs=[blk, blk, one, one, vec, vec],
        out_specs=[blk, blk, one, one],
        out_shape=[jax.ShapeDtypeStruct((nb, nc, S5_STATE_W), F32)] * 2
                  + [jax.ShapeDtypeStruct((nb, 1, S5_STATE_W), F32)] * 2,
        compiler_params=_params("arbitrary"),
        name="s5_scan",
    )(s_re.reshape(nb, nc, S5_STATE_W), s_im.reshape(nb, nc, S5_STATE_W), h0_re, h0_im, ops["a_re"], ops["a_im"])

    yt = pl.pallas_call(
        _s5_out_kernel,
        grid=(S5_NOCT, rows // tr),
        in_specs=[ut_spec, st_spec, st_spec, oct_spec(cw, cw), oct_spec(ow, cw), oct_spec(ow, cw)],
        out_specs=ut_spec,
        out_shape=jax.ShapeDtypeStruct(ut.shape, F32),
        compiler_params=_params("arbitrary", "arbitrary"),
        name="s5_out",
    )(ut, h_re.reshape(rows, S5_STATE_W), h_im.reshape(rows, S5_STATE_W), ops["t"], ops["m_re"], ops["m_im"])
    return yt, f_re, f_im


def _s5_operator_kernel(btr_ref, bti_ref, cr_ref, ci_ref, pl_re, pl_im, pr_re, pr_im,
                        t_ref, wre_ref, wim_ref, mre_ref, mim_ref):
    btr, bti = btr_ref[0], bti_ref[0]
    cr, ci = cr_ref[0], ci_ref[0]
    hi_dot = lambda a, b: jnp.dot(a, b, precision=HI, preferred_element_type=F32)
    kerns = []
    for k in range(S5_CHUNK + 1):
        ar, ai = pr_re[0, :, k:k + 1], pr_im[0, :, k:k + 1]
        cpr = cr * ar - ci * ai
        cpi = cr * ai + ci * ar
        if k < S5_CHUNK:
            kerns.append((hi_dot(btr, cpr) - hi_dot(bti, cpi)).astype(BF16))
            s = S5_CHUNK - 1 - k
            lr, li = pl_re[0, k:k + 1, :], pl_im[0, k:k + 1, :]
            wre_ref[0, s * LANE:(s + 1) * LANE, :] = (btr * lr - bti * li).astype(BF16)
            wim_ref[0, s * LANE:(s + 1) * LANE, :] = (btr * li + bti * lr).astype(BF16)
        if k >= 1:
            t = k - 1
            mre_ref[0, :, t * LANE:(t + 1) * LANE] = cpr.astype(BF16)
            mim_ref[0, :, t * LANE:(t + 1) * LANE] = (-cpi).astype(BF16)
    zero = jnp.zeros((LANE, LANE), BF16)
    for s in range(S5_CHUNK):
        for t in range(S5_CHUNK):
            t_ref[0, s * LANE:(s + 1) * LANE, t * LANE:(t + 1) * LANE] = kerns[t - s] if t >= s else zero


def _s5_operators(a_re, a_im, log_dt, b_re, b_im, c_re, c_im):
    steps = S5_CHUNK
    dt = jnp.exp(log_dt)[:, None]
    k = jnp.arange(steps + 1, dtype=F32)[:, None, None]
    mag = jnp.exp(k * (dt * a_re)[None])
    pw_r = mag * jnp.cos(k * (dt * a_im)[None])
    pw_i = mag * jnp.sin(k * (dt * a_im)[None])
    abr, abi = pw_r[1], pw_i[1]
    num_re, num_im = abr - 1.0, abi
    den = a_re * a_re + a_im * a_im
    f_re = (num_re * a_re + num_im * a_im) / den
    f_im = (num_im * a_re - num_re * a_im) / den
    bbr = f_re[..., None] * b_re - f_im[..., None] * b_im
    bbi = f_re[..., None] * b_im + f_im[..., None] * b_re
    same_group = jnp.arange(S5_OCT)[:, None, None, None] == jnp.arange(S5_OCT)[None, None, :, None]
    ow = S5_OCT * S5_STATE

    def block_diag(a):
        a = a.reshape(S5_NOCT, S5_OCT, a.shape[1], 1, a.shape[2])
        return jnp.where(same_group, a, 0.0).reshape(S5_NOCT, S5_OCT * a.shape[2], S5_OCT * a.shape[4])

    pad_k = S5_POW_PAD - (steps + 1)
    lanes = lambda p: jnp.pad(p.reshape(steps + 1, S5_NOCT, ow).transpose(1, 0, 2), ((0, 0), (0, pad_k), (0, 0)))
    rows = lambda p: jnp.pad(p.reshape(steps + 1, S5_NOCT, ow).transpose(1, 2, 0), ((0, 0), (0, 0), (0, pad_k)))
    bt_re, bt_im = block_diag(bbr.transpose(0, 2, 1)), block_diag(bbi.transpose(0, 2, 1))
    cb_re, cb_im = block_diag(c_re.transpose(0, 2, 1)), block_diag(c_im.transpose(0, 2, 1))
    cw = steps * LANE
    oct_spec = lambda a, b: pl.BlockSpec((1, a, b), lambda q: (q, 0, 0))
    t_mat, w_re, w_im, m_re, m_im = pl.pallas_call(
        _s5_operator_kernel,
        grid=(S5_NOCT,),
        in_specs=[oct_spec(LANE, ow), oct_spec(LANE, ow), oct_spec(ow, LANE), oct_spec(ow, LANE),
                  oct_spec(S5_POW_PAD, ow), oct_spec(S5_POW_PAD, ow), oct_spec(ow, S5_POW_PAD),
                  oct_spec(ow, S5_POW_PAD)],
        out_specs=[oct_spec(cw, cw), oct_spec(cw, ow), oct_spec(cw, ow), oct_spec(ow, cw), oct_spec(ow, cw)],
        out_shape=[jax.ShapeDtypeStruct((S5_NOCT, cw, cw), BF16)]
                  + [jax.ShapeDtypeStruct((S5_NOCT, cw, ow), BF16)] * 2
                  + [jax.ShapeDtypeStruct((S5_NOCT, ow, cw), BF16)] * 2,
        compiler_params=_params("arbitrary"),
        name="s5_operators",
    )(bt_re, bt_im, cb_re, cb_im, lanes(pw_r), lanes(pw_i), rows(pw_r), rows(pw_i))
    return {
        "t": t_mat, "w_re": w_re, "w_im": w_im, "m_re": m_re, "m_im": m_im,
        "a_re": pw_r[steps].reshape(1, S5_STATE_W), "a_im": pw_i[steps].reshape(1, S5_STATE_W),
    }


def _softmax_update_t(s, vt, m_ref, l_ref, acc_ref):
    m_prev = m_ref[...]
    m_new = jnp.maximum(m_prev, jnp.max(s, axis=0, keepdims=True))
    alpha = jnp.exp2(m_prev - m_new)
    p = jnp.exp2(s - m_new)
    l_ref[...] = alpha * l_ref[...] + jnp.sum(p, axis=0, keepdims=True)
    acc_ref[...] = alpha * acc_ref[...] + _dot(vt, p.astype(BF16))
    m_ref[...] = m_new


def _attn_prompt_kernel(qt_ref, k_ref, vt_ref, kmeta_ref, vmeta_t_ref, wuvt_ref, o_ref, m_ref, l_ref, acc_ref, s_ref,
                        *, tq):
    qi = pl.program_id(1)
    qt = jnp.concatenate([qt_ref[h] for h in range(MLA_HEADS)], axis=1)

    def keys(j):
        return pl.ds(pl.multiple_of(j * tq, tq), tq)

    def scores(j):
        return _dot(k_ref[0, keys(j), :], qt)

    s = _dot(kmeta_ref[...], qt)
    s = jnp.where(lax.broadcasted_iota(jnp.int32, s.shape, 0) < N_META, s, NEG_INF)
    m0 = jnp.max(s, axis=0, keepdims=True)
    p = jnp.exp2(s - m0)
    m_ref[...] = m0
    l_ref[...] = jnp.sum(p, axis=0, keepdims=True)
    acc_ref[...] = _dot(vmeta_t_ref[...], p.astype(BF16))

    s_ref[...] = scores(0)

    def body(kj, carry):
        s_next = scores(kj + 1)
        _softmax_update_t(s_ref[...], vt_ref[:, keys(kj)], m_ref, l_ref, acc_ref)
        s_ref[...] = s_next
        return carry

    lax.fori_loop(0, qi, body, 0)

    s = s_ref[...]
    t_k = lax.broadcasted_iota(jnp.int32, s.shape, 0)
    t_q = lax.broadcasted_iota(jnp.int32, s.shape, 1) % tq
    s = jnp.where(t_k <= t_q, s, NEG_INF)
    _softmax_update_t(s, vt_ref[:, keys(qi)], m_ref, l_ref, acc_ref)

    ot = (acc_ref[...] / l_ref[...]).astype(BF16)
    for h in range(MLA_HEADS):
        yt = _dot(wuvt_ref[h], ot[:, h * tq:(h + 1) * tq])
        o_ref[:, h * V_HEAD:(h + 1) * V_HEAD] = yt.T.astype(o_ref.dtype)


def _attn_prompt(qt, kcat, vt, kmeta, vmeta_t, w_uv_t, nb, tq):
    t = kcat.shape[1]
    nq = t // tq
    cols = MLA_HEADS * tq
    return pl.pallas_call(
        functools.partial(_attn_prompt_kernel, tq=tq),
        grid=(nb, nq),
        in_specs=[
            pl.BlockSpec((MLA_HEADS, QK_CAT, tq), lambda b, i: (0, 0, b * nq + i)),
            pl.BlockSpec((1, t, QK_CAT), lambda b, i: (b, 0, 0)),
            pl.BlockSpec((KV_RANK, t), lambda b, i: (0, b)),
            _const_spec(kmeta.shape),
            _const_spec(vmeta_t.shape),
            _const_spec(w_uv_t.shape),
        ],
        out_specs=pl.BlockSpec((tq, MLA_HEADS * V_HEAD), lambda b, i: (b * nq + i, 0)),
        out_shape=jax.ShapeDtypeStruct((nb * t, MLA_HEADS * V_HEAD), BF16),
        scratch_shapes=[pltpu.VMEM((1, cols), F32), pltpu.VMEM((1, cols), F32), pltpu.VMEM((KV_RANK, cols), F32),
                        pltpu.VMEM((tq, cols), F32)],
        compiler_params=_params("arbitrary", "arbitrary"),
        name="attn_prompt",
    )(qt, kcat, vt, kmeta, vmeta_t, w_uv_t)


def _attn_meta_kernel(q_ref, kmeta_ref, wuv_ref, o_ref):
    rows = MLA_HEADS * N_META
    q = q_ref[...].reshape(rows, QK_CAT)
    km = kmeta_ref[...]
    s = _dot_nt(q, km)
    t_q = lax.broadcasted_iota(jnp.int32, s.shape, 0) % N_META
    t_k = lax.broadcasted_iota(jnp.int32, s.shape, 1)
    s = jnp.where(t_k <= t_q, s, NEG_INF)
    p = jnp.exp(s - jnp.max(s, axis=1, keepdims=True))
    o = _dot(p.astype(BF16), km[:, :KV_RANK]) / jnp.sum(p, axis=1, keepdims=True)
    o = o.astype(BF16)
    for h in range(MLA_HEADS):
        o_ref[:, h * V_HEAD:(h + 1) * V_HEAD] = _dot(o[h * N_META:(h + 1) * N_META], wuv_ref[h]).astype(o_ref.dtype)


def _attn_meta(q, kmeta, w_uv):
    return pl.pallas_call(
        _attn_meta_kernel,
        out_shape=jax.ShapeDtypeStruct((N_META, MLA_HEADS * V_HEAD), BF16),
        name="attn_meta",
    )(q, kmeta, w_uv)


def _attn_sample_kernel(pt_ref, q_ref, knew_ref, wuv_ref, ckv_hbm, kr_hbm, o_ref, ckv_buf, kr_buf, sem,
                        *, n_pages, steps):
    b = pl.program_id(0)
    slot = b % 2

    def page_copies(seq, into, j):
        page = pt_ref[seq, j]
        return (pltpu.make_async_copy(ckv_hbm.at[0, page], ckv_buf.at[into, j], sem.at[into, 0]),
                pltpu.make_async_copy(kr_hbm.at[0, page], kr_buf.at[into, j], sem.at[into, 1]))

    def for_each_page(seq, into, act):
        def body(j, carry):
            for copy in page_copies(seq, into, j):
                act(copy)
            return carry
        lax.fori_loop(0, n_pages, body, 0)

    @pl.when(b == 0)
    def _():
        for_each_page(0, 0, lambda copy: copy.start())

    @pl.when(b + 1 < pl.num_programs(0))
    def _():
        for_each_page(b + 1, 1 - slot, lambda copy: copy.start())

    for_each_page(b, slot, lambda copy: copy.wait())

    rows = MLA_HEADS * steps
    q = q_ref[...].reshape(rows, QK_CAT).astype(BF16)
    q_lat = q[:, :KV_RANK]
    q_rope = q[:, KV_RANK:KV_RANK + QK_ROPE]
    groups = range(0, n_pages, PAGE_GROUP)

    def latent_keys(j0):
        return jnp.concatenate([ckv_buf[slot, j].astype(BF16) for j in range(j0, j0 + PAGE_GROUP)], axis=0)

    def chunk_scores(c):
        parts = []
        for j0 in groups[c * CHUNK_GROUPS:(c + 1) * CHUNK_GROUPS]:
            kr_t = jnp.concatenate([kr_buf[slot, j].astype(BF16) for j in range(j0, j0 + PAGE_GROUP)], axis=1)
            parts.append(_dot_nt(q_lat, latent_keys(j0)) + _dot(q_rope, kr_t))
        return jnp.concatenate(parts, axis=1)

    kn = jnp.concatenate([knew_ref[...], jnp.zeros((PAGE_SIZE - steps, QK_CAT), F32)], axis=0).astype(BF16)
    s_new = _dot_nt(q, kn)
    t_q = lax.broadcasted_iota(jnp.int32, s_new.shape, 0) % steps
    t_k = lax.broadcasted_iota(jnp.int32, s_new.shape, 1)
    s_new = jnp.where(t_k <= t_q, s_new, NEG_INF)

    gk = PAGE_GROUP * PAGE_SIZE
    n_chunks = len(groups) // CHUNK_GROUPS
    m = jnp.full((rows, 1), NEG_INF, F32)
    l = jnp.zeros((rows, 1), F32)
    acc = jnp.zeros((rows, KV_RANK), F32)
    s_cur = chunk_scores(0)
    for c in range(n_chunks + 1):
        s_next = chunk_scores(c + 1) if c + 1 < n_chunks else s_new
        m_new = jnp.maximum(m, jnp.max(s_cur, axis=1, keepdims=True))
        alpha = jnp.exp(m - m_new)
        p = jnp.exp(s_cur - m_new)
        l = alpha * l + jnp.sum(p, axis=1, keepdims=True)
        p = p.astype(BF16)
        if c < n_chunks:
            chunk_groups = groups[c * CHUNK_GROUPS:(c + 1) * CHUNK_GROUPS]
            pv = _dot(p[:, :gk], latent_keys(chunk_groups[0]))
            for i, j0 in enumerate(chunk_groups[1:], start=1):
                pv = pv + _dot(p[:, i * gk:(i + 1) * gk], latent_keys(j0))
        else:
            pv = _dot(p, kn[:, :KV_RANK])
        acc = alpha * acc + pv
        m = m_new
        s_cur = s_next
    o = (acc / l).astype(BF16)
    for h in range(MLA_HEADS):
        o_ref[:, h * V_HEAD:(h + 1) * V_HEAD] = _dot(o[h * steps:(h + 1) * steps], wuv_ref[h])


def _attn_sample(page_table, q, knew, cache_ckv, cache_kr_t, w_uv, steps):
    nb, n_pages = page_table.shape
    assert n_pages % (PAGE_GROUP * CHUNK_GROUPS) == 0
    grid_spec = pltpu.PrefetchScalarGridSpec(
        num_scalar_prefetch=1,
        grid=(nb,),
        in_specs=[pl.BlockSpec((MLA_HEADS, steps, QK_CAT), lambda b, pt: (0, b, 0)),
                  pl.BlockSpec((steps, QK_CAT), lambda b, pt: (b, 0)),
                  pl.BlockSpec(w_uv.shape, lambda b, pt: (0, 0, 0)),
                  pl.BlockSpec(memory_space=pl.ANY),
                  pl.BlockSpec(memory_space=pl.ANY)],
        out_specs=pl.BlockSpec((steps, MLA_HEADS * V_HEAD), lambda b, pt: (b, 0)),
        scratch_shapes=[pltpu.VMEM((2, n_pages, PAGE_SIZE, KV_RANK), F32),
                        pltpu.VMEM((2, n_pages, QK_ROPE, PAGE_SIZE), F32),
                        pltpu.SemaphoreType.DMA((2, 2))],
    )
    return pl.pallas_call(
        functools.partial(_attn_sample_kernel, n_pages=n_pages, steps=steps),
        grid_spec=grid_spec,
        out_shape=jax.ShapeDtypeStruct((nb * steps, MLA_HEADS * V_HEAD), F32),
        compiler_params=_params("arbitrary"),
        name="attn_sample",
    )(page_table, q, knew, w_uv, cache_ckv, cache_kr_t)


def _gelu_tanh(x):
    return 0.5 * x * (1.0 + jnp.tanh(math.sqrt(2.0 / math.pi) * (x + 0.044715 * (x * x * x))))


def _sigmoid(x):
    return 1.0 / (1.0 + jnp.exp(-x))


def _post_kernel(x_ref, yt_ref, u_ref, ymla_ref, p1_ref, p2_ref, d_ref, wglu_ref, wout_ref, gffn_ref,
                 wgate_ref, wup_ref, convw_ref, convb_ref, wdown_ref, gfin_ref,
                 y_ref, gate_ref, ybuf, hbuf, carry_ref, *, tm, seq_steps):
    long_seq = seq_steps == 0
    for o in range(S5_NOCT):
        for t in range(S5_CHUNK):
            ybuf[o, pl.ds(t, tm // S5_CHUNK, stride=S5_CHUNK), :] = yt_ref[o, t]
    y_ssm = jnp.concatenate([ybuf[o] for o in range(S5_NOCT)], axis=1)
    y = y_ssm + d_ref[...] * u_ref[...]
    y = _gelu_tanh(y)
    y = y * _sigmoid(_dot(y.astype(BF16), wglu_ref[...]))
    mixed = _dot(y.astype(BF16), wout_ref[:S5_W, :]) + _dot(ymla_ref[...].astype(BF16), wout_ref[S5_W:, :])
    x1 = x_ref[...] + mixed
    xn = _rms(x1, gffn_ref[...]).astype(BF16)

    row = lax.broadcasted_iota(jnp.int32, (tm, FF_CHUNK), 0)
    if long_seq:
        @pl.when(pl.program_id(1) == 0)
        def _():
            carry_ref[...] = p1_ref[...]
        t_in = row
    else:
        t_in = row % seq_steps

    for c in range(D_FF // FF_CHUNK):
        sl = slice(c * FF_CHUNK, (c + 1) * FF_CHUNK)
        gate = _dot(xn, wgate_ref[:, sl])
        up = _dot(xn, wup_ref[:, sl])
        if long_seq:
            back1 = jnp.broadcast_to(carry_ref[7:8, sl], gate.shape)
            back2 = jnp.where(t_in == 0, carry_ref[6:7, sl], back1)
            carry_ref[:, sl] = gate[tm - 8:, :]
            gate_ref[:, sl] = gate[tm - 8:, :]
        else:
            back1 = p1_ref[:, sl]
            back2 = p2_ref[:, sl]
            gate_ref[:, sl] = gate
        prev1 = jnp.where(t_in >= 1, pltpu.roll(gate, 1, 0), back1)
        prev2 = jnp.where(t_in >= 2, pltpu.roll(gate, 2, 0), back2)
        conv = (convb_ref[:, sl] + convw_ref[0:1, sl] * prev2 + convw_ref[1:2, sl] * prev1
                + convw_ref[2:3, sl] * gate)
        hbuf[:, sl] = (conv * _sigmoid(conv) * up).astype(BF16)

    x2 = x1 + _dot(hbuf[...], wdown_ref[...])
    y_ref[...] = _rms(x2, gfin_ref[...])


def _post_call(x2d, yt, u, y_mla, p1, p2, w, nb, tm, seq_steps):
    rows = x2d.shape[0]
    nt = rows // (nb * tm)
    assert nb * nt * tm == rows and tm % S5_CHUNK == 0
    long_seq = seq_steps == 0
    row = lambda width: pl.BlockSpec((tm, width), lambda b, i: (b * nt + i, 0))
    yt_spec = pl.BlockSpec((S5_NOCT, S5_CHUNK, tm // S5_CHUNK, LANE), lambda b, i: (0, 0, b * nt + i, 0))
    if long_seq:
        hist = [_const_spec((8, D_FF)), _const_spec((8, D_FF))]
        gate_spec = pl.BlockSpec((None, 8, D_FF), lambda b, i: (b, 0, 0))
        gate_shape = jax.ShapeDtypeStruct((nb, 8, D_FF), F32)
    else:
        hist = [row(D_FF), row(D_FF)]
        gate_spec = row(D_FF)
        gate_shape = jax.ShapeDtypeStruct((rows, D_FF), F32)
    return pl.pallas_call(
        functools.partial(_post_kernel, tm=tm, seq_steps=seq_steps),
        grid=(nb, nt),
        in_specs=[row(D_MODEL), yt_spec, row(S5_W), row(MLA_HEADS * V_HEAD)] + hist + [
            _const_spec((1, S5_W)), _const_spec((S5_W, S5_W)), _const_spec((D_MODEL, D_MODEL)),
            _const_spec((1, D_MODEL)), _const_spec((D_MODEL, D_FF)), _const_spec((D_MODEL, D_FF)),
            _const_spec((CONV_W, D_FF)), _const_spec((1, D_FF)), _const_spec((D_FF, D_MODEL)),
            _const_spec((1, D_MODEL))],
        out_specs=[row(D_MODEL), gate_spec],
        out_shape=[jax.ShapeDtypeStruct((rows, D_MODEL), F32), gate_shape],
        scratch_shapes=[pltpu.VMEM((S5_NOCT, tm, LANE), F32), pltpu.VMEM((tm, D_FF), BF16),
                        pltpu.VMEM((8, D_FF), F32)],
        compiler_params=_params("arbitrary", "arbitrary"),
        name="post",
    )(x2d, yt, u, y_mla, p1, p2, w["s5_d"], w["w_glu"], w["w_out"], w["g_ffn"], w["w_gate"], w["w_up"],
      w["conv_w"], w["conv_b"], w["w_down"], w["g_final"])


def _rope_tables(pos):
    half = QK_ROPE // 2
    inv = ROPE_BASE ** (-jnp.arange(half, dtype=F32) / half)
    ang = pos.astype(F32)[:, None] * inv[None, :]
    cos, sin = jnp.cos(ang), jnp.sin(ang)
    pad = jnp.zeros((pos.shape[0], ROPE_PAD - QK_ROPE), F32)
    return jnp.concatenate([cos, cos, pad], axis=1), jnp.concatenate([-sin, sin, pad], axis=1)


def kernel(x_prompt, x_sample, cache_ckv, cache_kr, state_s5_re, state_s5_im, state_conv, page_table, meta_tokens, g_mix, w_in, g_q, w_uq, g_kv, w_uk, w_uv, s5_a_re, s5_a_im, s5_log_dt, s5_b_re, s5_b_im, s5_c_re, s5_c_im, s5_d, w_glu, w_out, g_ffn, w_gate, w_up, conv_w, conv_b, w_down, g_final):
    assert w_in.shape[0] == 1, "single-layer step"
    nb, seq, _ = x_prompt.shape
    db, steps, _ = x_sample.shape
    n_past = page_table.shape[1] * PAGE_SIZE
    assert seq % S5_CHUNK == 0 and N_META % S5_CHUNK == 0 and steps == S5_CHUNK

    w_in_pad = jnp.concatenate([w_in[0], jnp.zeros((D_MODEL, IN_PAD - w_in.shape[2]), F32)], axis=1)
    wq = w_uq[0].reshape(Q_RANK, MLA_HEADS, QK_NOPE + QK_ROPE)
    wq_rope = jnp.concatenate([wq[:, :, QK_NOPE:], jnp.zeros((Q_RANK, MLA_HEADS, ROPE_PAD - QK_ROPE), F32)], axis=2)
    wq_perm = jnp.concatenate([wq[:, :, :QK_NOPE].reshape(Q_RANK, -1), wq_rope.reshape(Q_RANK, -1)], axis=1)
    w = {
        "g_mix": g_mix[0].reshape(1, -1), "w_in": w_in_pad.astype(BF16),
        "g_q": g_q[0].reshape(1, -1), "w_uq": wq_perm.astype(BF16), "w_uq_t": wq_perm.T.astype(BF16),
        "w_uk_t": w_uk[0].transpose(1, 2, 0).astype(BF16),
        "w_uk": w_uk[0].transpose(1, 0, 2).astype(BF16),
        "g_kv": g_kv[0].reshape(1, -1),
        "s5_d": s5_d[0].reshape(1, -1), "w_glu": w_glu[0].astype(BF16), "w_out": w_out[0].astype(BF16),
        "g_ffn": g_ffn[0].reshape(1, -1), "w_gate": w_gate[0].astype(BF16), "w_up": w_up[0].astype(BF16),
        "conv_w": conv_w[0], "conv_b": conv_b[0].reshape(1, -1), "w_down": w_down[0].astype(BF16),
        "g_final": g_final.reshape(1, -1),
    }
    w_uv_h = w_uv[0].transpose(1, 0, 2).astype(BF16)
    ops = _s5_operators(s5_a_re[0], s5_a_im[0], s5_log_dt[0], s5_b_re[0], s5_b_im[0], s5_c_re[0], s5_c_im[0])

    cos_m, sin_m = _rope_tables(jnp.arange(N_META, dtype=jnp.int32))
    cos_p, sin_p = _rope_tables(N_META + jnp.arange(seq, dtype=jnp.int32))
    tm_p = 512
    tm_s = 512
    cos_s, sin_s = _rope_tables(n_past + jnp.arange(tm_s, dtype=jnp.int32) % steps)

    tiles_per_seq = seq // tm_p
    u_m, ut_m, q_m, kcat_m, ckv_m, kr_m = _pre_call(meta_tokens, cos_m, sin_m, lambda i: (0, 0), N_META, BF16, w)
    u_p, ut_p, qt_p, kcat_p, ckv_p, kr_p, vt_p = _pre_call(
        x_prompt.reshape(nb * seq, D_MODEL), cos_p, sin_p, lambda i: (i % tiles_per_seq, 0), tm_p, BF16, w,
        q_transposed=True)
    u_s, ut_s, q_s, kcat_s, ckv_s, kr_s = _pre_call(x_sample.reshape(db * steps, D_MODEL), cos_s, sin_s,
                                                    lambda i: (0, 0), tm_s, F32, w)

    zero_state = jnp.zeros((1, 1, S5_STATE_W), F32)
    yt_m, fm_re, fm_im = _s5_mixer(ut_m, zero_state, zero_state, ops, 1)
    yt_p, s5_re_p, s5_im_p = _s5_mixer(ut_p, jnp.broadcast_to(fm_re, (nb, 1, S5_STATE_W)),
                                       jnp.broadcast_to(fm_im, (nb, 1, S5_STATE_W)), ops, nb)
    yt_s, s5_re_s, s5_im_s = _s5_mixer(ut_s, state_s5_re[0].reshape(db, 1, S5_STATE_W),
                                       state_s5_im[0].reshape(db, 1, S5_STATE_W), ops, db)

    kmeta = jnp.concatenate([kcat_m, jnp.zeros((PAGE_SIZE - N_META, QK_CAT), BF16)], axis=0)
    ymla_m = _attn_meta(q_m, kmeta, w_uv_h)
    ymla_p = _attn_prompt(qt_p, kcat_p.reshape(nb, seq, QK_CAT), vt_p, kmeta, kmeta[:, :KV_RANK].T,
                          w_uv[0].transpose(1, 2, 0).astype(BF16), nb, tq=256)
    ymla_s = _attn_sample(page_table, q_s, kcat_s, cache_ckv, jnp.swapaxes(cache_kr, 2, 3), w_uv_h, steps)

    zeros_hist = jnp.zeros((8, D_FF), F32)
    _, gate_m = _post_call(meta_tokens, yt_m, u_m, ymla_m, zeros_hist, zeros_hist, w, 1, N_META, 0)
    y_p, gate_p = _post_call(x_prompt.reshape(nb * seq, D_MODEL), yt_p, u_p, ymla_p, gate_m[0], gate_m[0],
                             w, nb, tm_p, 0)
    sc = state_conv[0]
    back1 = jnp.pad(sc[:, 1:2], ((0, 0), (0, steps - 1), (0, 0))).reshape(db * steps, D_FF)
    back2 = jnp.pad(sc, ((0, 0), (0, steps - 2), (0, 0))).reshape(db * steps, D_FF)
    y_s, gate_s = _post_call(x_sample.reshape(db * steps, D_MODEL), yt_s, u_s, ymla_s, back1, back2,
                             w, 1, tm_s // 2, steps)

    def with_meta(meta_rows, tok_rows, width):
        return jnp.concatenate([jnp.broadcast_to(meta_rows[None], (nb, N_META, width)),
                                tok_rows.reshape(nb, seq, width)], axis=1)[None]

    return (y_p.reshape(nb, seq, D_MODEL), y_s.reshape(db, steps, D_MODEL),
            with_meta(ckv_m, ckv_p, KV_RANK), with_meta(kr_m, kr_p, QK_ROPE),
            s5_re_p.reshape(1, nb, S5_GROUPS, S5_STATE), s5_im_p.reshape(1, nb, S5_GROUPS, S5_STATE),
            gate_p[:, 8 - (CONV_W - 1):][None],
            ckv_s.reshape(1, db, steps, KV_RANK), kr_s.reshape(1, db, steps, QK_ROPE),
            s5_re_s.reshape(1, db, S5_GROUPS, S5_STATE), s5_im_s.reshape(1, db, S5_GROUPS, S5_STATE),
            gate_s.reshape(db, steps, D_FF)[:, steps - (CONV_W - 1):][None])
```

```python
import functools
import math

import jax
import jax.numpy as jnp
from jax import lax
from jax.experimental import pallas as pl
from jax.experimental.pallas import tpu as pltpu

F32 = jnp.float32
BF16 = jnp.bfloat16

D_MODEL = 1024
N_META = 16
S5_W = 512
S5_GROUP = 16
S5_GROUPS = 32
S5_STATE = 64
MLA_HEADS = 4
QK_NOPE = 128
QK_ROPE = 64
V_HEAD = 128
Q_RANK = 384
KV_RANK = 256
D_FF = 2816
CONV_W = 3
ROPE_BASE = 10000.0
EPS = 1e-6
PAGE_SIZE = 128
ATTN_SCALE = 1.0 / math.sqrt(QK_NOPE + QK_ROPE)
LOG2_E = math.log2(math.e)

LANE = 128
ROPE_PAD = LANE
QK_CAT = KV_RANK + ROPE_PAD
IN_PAD = S5_W + Q_RANK + KV_RANK + ROPE_PAD
S5_CHUNK = 8
S5_OCT = LANE // S5_GROUP
S5_NOCT = S5_GROUPS // S5_OCT
S5_STATE_W = S5_GROUPS * S5_STATE
S5_SCAN_W = 512
S5_POW_PAD = 16
FF_CHUNK = 256
PAGE_GROUP = 4
CHUNK_GROUPS = 4
VMEM_LIMIT = 56 * 1024 * 1024
NEG_INF = float("-inf")
HI = lax.Precision.HIGHEST


def _const_spec(shape):
    nd = len(shape)
    return pl.BlockSpec(shape, lambda *_: (0,) * nd, pipeline_mode=pl.Buffered(1))


def _params(*semantics):
    return pltpu.CompilerParams(dimension_semantics=semantics, vmem_limit_bytes=VMEM_LIMIT)


def _rms(x, g):
    return x * lax.rsqrt(jnp.mean(x * x, axis=-1, keepdims=True) + EPS) * g


def _rope_slab(x, cos, sin):
    lane = lax.broadcasted_iota(jnp.int32, x.shape, 1)
    half = QK_ROPE // 2
    swapped = jnp.where(lane < half, pltpu.roll(x, LANE - half, 1), pltpu.roll(x, half, 1))
    return x * cos + swapped * sin


def _dot(a, b):
    return jnp.dot(a, b, preferred_element_type=F32)


def _dot_nt(a, b):
    return lax.dot_general(a, b, (((1,), (1,)), ((), ())), preferred_element_type=F32)


def _pre_kernel(x_ref, cos_ref, sin_ref, cost_ref, sint_ref, gmix_ref, win_ref, gq_ref, wuq_ref, wuk_ref, gkv_ref,
                u_ref, ut_ref, q_ref, kcat_ref, ckv_ref, kr_ref, *rest, q_transposed):
    x = x_ref[...]
    xn = _rms(x, gmix_ref[...]).astype(BF16)
    z = _dot(xn, win_ref[...])
    u_ref[...] = z[:, :S5_W]
    slab_ref = rest[-1]
    chunks = u_ref.shape[0] // S5_CHUNK
    for o in range(S5_NOCT):
        slab_ref[o] = z[:, o * LANE:(o + 1) * LANE]
        for s in range(S5_CHUNK):
            ut_ref[o, s] = slab_ref[o, pl.ds(s, chunks, stride=S5_CHUNK), :].astype(BF16)
    cq = z[:, S5_W:S5_W + Q_RANK]
    ckv_raw = z[:, S5_W + Q_RANK:S5_W + Q_RANK + KV_RANK]
    kr_raw = z[:, S5_W + Q_RANK + KV_RANK:]
    cqn = _rms(cq, gq_ref[...]).astype(BF16)
    nope_w = MLA_HEADS * QK_NOPE
    half = QK_ROPE // 2
    if q_transposed:
        qt = _dot_nt(wuq_ref[...], cqn)
        cost = cost_ref[...]
        sint = sint_ref[...]
        for h in range(MLA_HEADS):
            qn = qt[h * QK_NOPE:(h + 1) * QK_NOPE].astype(BF16)
            q_lat = _dot(wuk_ref[h], qn) * (ATTN_SCALE * LOG2_E)
            xr = qt[nope_w + h * ROPE_PAD:nope_w + (h + 1) * ROPE_PAD]
            swapped = jnp.concatenate([xr[half:QK_ROPE], xr[:half], xr[QK_ROPE:]], axis=0)
            qr = (xr * cost + swapped * sint) * (ATTN_SCALE * LOG2_E)
            q_ref[h, :KV_RANK, :] = q_lat.astype(q_ref.dtype)
            q_ref[h, KV_RANK:, :] = qr.astype(q_ref.dtype)
    else:
        q = _dot(cqn, wuq_ref[...])
        for h in range(MLA_HEADS):
            qn = q[:, h * QK_NOPE:(h + 1) * QK_NOPE].astype(BF16)
            q_lat = _dot(qn, wuk_ref[h]) * ATTN_SCALE
            qr = _rope_slab(q[:, nope_w + h * ROPE_PAD:nope_w + (h + 1) * ROPE_PAD],
                            cos_ref[...], sin_ref[...]) * ATTN_SCALE
            q_ref[h, :, :KV_RANK] = q_lat.astype(q_ref.dtype)
            q_ref[h, :, KV_RANK:] = qr.astype(q_ref.dtype)
    ckv = _rms(ckv_raw, gkv_ref[...])
    kr = _rope_slab(kr_raw, cos_ref[...], sin_ref[...])
    ckv_ref[...] = ckv
    kr_ref[...] = kr[:, :QK_ROPE]
    kcat_ref[:, :KV_RANK] = ckv.astype(kcat_ref.dtype)
    kcat_ref[:, KV_RANK:] = kr.astype(kcat_ref.dtype)
    if q_transposed:
        rest[0][...] = ckv.T.astype(BF16)


def _pre_call(x2d, cos, sin, tab_map, tm, qdtype, w, q_transposed=False):
    rows = x2d.shape[0]
    assert rows % tm == 0 and tm % S5_CHUNK == 0
    row = lambda width: pl.BlockSpec((tm, width), lambda i: (i, 0))
    tab_map_t = lambda i: tab_map(i)[::-1]
    if q_transposed:
        wuq, wuk = w["w_uq_t"], w["w_uk"]
        q_spec = pl.BlockSpec((MLA_HEADS, QK_CAT, tm), lambda i: (0, 0, i))
        q_shape = jax.ShapeDtypeStruct((MLA_HEADS, QK_CAT, rows), qdtype)
        extra_specs = [pl.BlockSpec((KV_RANK, tm), lambda i: (0, i))]
        extra_shapes = [jax.ShapeDtypeStruct((KV_RANK, rows), BF16)]
    else:
        wuq, wuk = w["w_uq"], w["w_uk_t"]
        q_spec = pl.BlockSpec((MLA_HEADS, tm, QK_CAT), lambda i: (0, i, 0))
        q_shape = jax.ShapeDtypeStruct((MLA_HEADS, rows, QK_CAT), qdtype)
        extra_specs, extra_shapes = [], []
    return pl.pallas_call(
        functools.partial(_pre_kernel, q_transposed=q_transposed),
        grid=(rows // tm,),
        in_specs=[
            row(D_MODEL),
            pl.BlockSpec((tm, ROPE_PAD), tab_map),
            pl.BlockSpec((tm, ROPE_PAD), tab_map),
            pl.BlockSpec((ROPE_PAD, tm), tab_map_t),
            pl.BlockSpec((ROPE_PAD, tm), tab_map_t),
            _const_spec((1, D_MODEL)),
            _const_spec((D_MODEL, IN_PAD)),
            _const_spec((1, Q_RANK)),
            _const_spec(wuq.shape),
            _const_spec(wuk.shape),
            _const_spec((1, KV_RANK)),
        ],
        out_specs=[row(S5_W), pl.BlockSpec((S5_NOCT, S5_CHUNK, tm // S5_CHUNK, LANE), lambda i: (0, 0, i, 0)),
                   q_spec, row(QK_CAT), row(KV_RANK), row(QK_ROPE)] + extra_specs,
        out_shape=[
            jax.ShapeDtypeStruct((rows, S5_W), F32),
            jax.ShapeDtypeStruct((S5_NOCT, S5_CHUNK, rows // S5_CHUNK, LANE), BF16),
            q_shape,
            jax.ShapeDtypeStruct((rows, QK_CAT), qdtype),
            jax.ShapeDtypeStruct((rows, KV_RANK), F32),
            jax.ShapeDtypeStruct((rows, QK_ROPE), F32),
        ] + extra_shapes,
        scratch_shapes=[pltpu.VMEM((S5_NOCT, tm, LANE), F32)],
        compiler_params=_params("arbitrary"),
        name="pre",
    )(x2d, cos, sin, cos.T, sin.T, w["g_mix"], w["w_in"], w["g_q"], wuq, wuk, w["g_kv"])


def _chunk_inputs(ut_ref):
    return jnp.concatenate([ut_ref[s] for s in range(S5_CHUNK)], axis=1)


def _s5_state_in_kernel(ut_ref, wre_ref, wim_ref, sre_ref, sim_ref):
    u = _chunk_inputs(ut_ref)
    sre_ref[...] = _dot(u, wre_ref[0])
    sim_ref[...] = _dot(u, wim_ref[0])


def _s5_scan_kernel(sre_ref, sim_ref, h0re_ref, h0im_ref, are_ref, aim_ref, hre_ref, him_ref, fre_ref, fim_ref):
    nb, nc, width = sre_ref.shape
    ar = are_ref[...].reshape(1, 1, width)
    ai = aim_ref[...].reshape(1, 1, width)

    def body(c, carry):
        hr, hi = carry
        hre_ref[:, pl.ds(c, 1), :] = hr
        him_ref[:, pl.ds(c, 1), :] = hi
        sr = sre_ref[:, pl.ds(c, 1), :]
        si = sim_ref[:, pl.ds(c, 1), :]
        return ar * hr - ai * hi + sr, ar * hi + ai * hr + si

    hr, hi = lax.fori_loop(0, nc, body, (h0re_ref[...], h0im_ref[...]))
    fre_ref[...] = hr
    fim_ref[...] = hi


def _s5_out_kernel(ut_ref, hre_ref, him_ref, t_ref, mre_ref, mim_ref, yt_ref):
    y = (_dot(_chunk_inputs(ut_ref), t_ref[0]) + _dot(hre_ref[...].astype(BF16), mre_ref[0])
         + _dot(him_ref[...].astype(BF16), mim_ref[0]))
    for t in range(S5_CHUNK):
        yt_ref[t] = y[:, t * LANE:(t + 1) * LANE]


def _s5_mixer(ut, h0_re, h0_im, ops, nb):
    rows = ut.shape[2]
    nc = rows // nb
    tr = min(rows, 1024)
    assert rows % tr == 0
    ow = S5_OCT * S5_STATE
    cw = S5_CHUNK * LANE
    ut_spec = pl.BlockSpec((None, S5_CHUNK, tr, LANE), lambda q, r: (q, 0, r, 0))
    st_spec = pl.BlockSpec((tr, ow), lambda q, r: (r, q))
    oct_spec = lambda a, b: pl.BlockSpec((1, a, b), lambda q, r: (q, 0, 0))
    s_re, s_im = pl.pallas_call(
        _s5_state_in_kernel,
        grid=(S5_NOCT, rows // tr),
        in_specs=[ut_spec, oct_spec(cw, ow), oct_spec(cw, ow)],
        out_specs=[st_spec, st_spec],
        out_shape=[jax.ShapeDtypeStruct((rows, S5_STATE_W), F32)] * 2,
        compiler_params=_params("arbitrary", "arbitrary"),
        name="s5_state_in",
    )(ut, ops["w_re"], ops["w_im"])

    blk = pl.BlockSpec((nb, nc, S5_SCAN_W), lambda j: (0, 0, j))
    one = pl.BlockSpec((nb, 1, S5_SCAN_W), lambda j: (0, 0, j))
    vec = pl.BlockSpec((1, S5_SCAN_W), lambda j: (0, j))
    h_re, h_im, f_re, f_im = pl.pallas_call(
        _s5_scan_kernel,
        grid=(S5_STATE_W // S5_SCAN_W,),
        in_specs=[blk, blk, one, one, vec, vec],
        out_specs=[blk, blk, one, one],
        out_shape=[jax.ShapeDtypeStruct((nb, nc, S5_STATE_W), F32)] * 2
                  + [jax.ShapeDtypeStruct((nb, 1, S5_STATE_W), F32)] * 2,
        compiler_params=_params("arbitrary"),
        name="s5_scan",
    )(s_re.reshape(nb, nc, S5_STATE_W), s_im.reshape(nb, nc, S5_STATE_W), h0_re, h0_im, ops["a_re"], ops["a_im"])

    yt = pl.pallas_call(
        _s5_out_kernel,
        grid=(S5_NOCT, rows // tr),
        in_specs=[ut_spec, st_spec, st_spec, oct_spec(cw, cw), oct_spec(ow, cw), oct_spec(ow, cw)],
        out_specs=ut_spec,
        out_shape=jax.ShapeDtypeStruct(ut.shape, F32),
        compiler_params=_params("arbitrary", "arbitrary"),
        name="s5_out",
    )(ut, h_re.reshape(rows, S5_STATE_W), h_im.reshape(rows, S5_STATE_W), ops["t"], ops["m_re"], ops["m_im"])
    return yt, f_re, f_im


def _s5_operator_kernel(btr_ref, bti_ref, cr_ref, ci_ref, pl_re, pl_im, pr_re, pr_im,
                        t_ref, wre_ref, wim_ref, mre_ref, mim_ref):
    btr, bti = btr_ref[0], bti_ref[0]
    cr, ci = cr_ref[0], ci_ref[0]
    hi_dot = lambda a, b: jnp.dot(a, b, precision=HI, preferred_element_type=F32)
    kerns = []
    for k in range(S5_CHUNK + 1):
        ar, ai = pr_re[0, :, k:k + 1], pr_im[0, :, k:k + 1]
        cpr = cr * ar - ci * ai
        cpi = cr * ai + ci * ar
        if k < S5_CHUNK:
            kerns.append((hi_dot(btr, cpr) - hi_dot(bti, cpi)).astype(BF16))
            s = S5_CHUNK - 1 - k
            lr, li = pl_re[0, k:k + 1, :], pl_im[0, k:k + 1, :]
            wre_ref[0, s * LANE:(s + 1) * LANE, :] = (btr * lr - bti * li).astype(BF16)
            wim_ref[0, s * LANE:(s + 1) * LANE, :] = (btr * li + bti * lr).astype(BF16)
        if k >= 1:
            t = k - 1
            mre_ref[0, :, t * LANE:(t + 1) * LANE] = cpr.astype(BF16)
            mim_ref[0, :, t * LANE:(t + 1) * LANE] = (-cpi).astype(BF16)
    zero = jnp.zeros((LANE, LANE), BF16)
    for s in range(S5_CHUNK):
        for t in range(S5_CHUNK):
            t_ref[0, s * LANE:(s + 1) * LANE, t * LANE:(t + 1) * LANE] = kerns[t - s] if t >= s else zero


def _s5_operators(a_re, a_im, log_dt, b_re, b_im, c_re, c_im):
    steps = S5_CHUNK
    dt = jnp.exp(log_dt)[:, None]
    k = jnp.arange(steps + 1, dtype=F32)[:, None, None]
    mag = jnp.exp(k * (dt * a_re)[None])
    pw_r = mag * jnp.cos(k * (dt * a_im)[None])
    pw_i = mag * jnp.sin(k * (dt * a_im)[None])
    abr, abi = pw_r[1], pw_i[1]
    num_re, num_im = abr - 1.0, abi
    den = a_re * a_re + a_im * a_im
    f_re = (num_re * a_re + num_im * a_im) / den
    f_im = (num_im * a_re - num_re * a_im) / den
    bbr = f_re[..., None] * b_re - f_im[..., None] * b_im
    bbi = f_re[..., None] * b_im + f_im[..., None] * b_re
    same_group = jnp.arange(S5_OCT)[:, None, None, None] == jnp.arange(S5_OCT)[None, None, :, None]
    ow = S5_OCT * S5_STATE

    def block_diag(a):
        a = a.reshape(S5_NOCT, S5_OCT, a.shape[1], 1, a.shape[2])
        return jnp.where(same_group, a, 0.0).reshape(S5_NOCT, S5_OCT * a.shape[2], S5_OCT * a.shape[4])

    pad_k = S5_POW_PAD - (steps + 1)
    lanes = lambda p: jnp.pad(p.reshape(steps + 1, S5_NOCT, ow).transpose(1, 0, 2), ((0, 0), (0, pad_k), (0, 0)))
    rows = lambda p: jnp.pad(p.reshape(steps + 1, S5_NOCT, ow).transpose(1, 2, 0), ((0, 0), (0, 0), (0, pad_k)))
    bt_re, bt_im = block_diag(bbr.transpose(0, 2, 1)), block_diag(bbi.transpose(0, 2, 1))
    cb_re, cb_im = block_diag(c_re.transpose(0, 2, 1)), block_diag(c_im.transpose(0, 2, 1))
    cw = steps * LANE
    oct_spec = lambda a, b: pl.BlockSpec((1, a, b), lambda q: (q, 0, 0))
    t_mat, w_re, w_im, m_re, m_im = pl.pallas_call(
        _s5_operator_kernel,
        grid=(S5_NOCT,),
        in_specs=[oct_spec(LANE, ow), oct_spec(LANE, ow), oct_spec(ow, LANE), oct_spec(ow, LANE),
                  oct_spec(S5_POW_PAD, ow), oct_spec(S5_POW_PAD, ow), oct_spec(ow, S5_POW_PAD),
                  oct_spec(ow, S5_POW_PAD)],
        out_specs=[oct_spec(cw, cw), oct_spec(cw, ow), oct_spec(cw, ow), oct_spec(ow, cw), oct_spec(ow, cw)],
        out_shape=[jax.ShapeDtypeStruct((S5_NOCT, cw, cw), BF16)]
                  + [jax.ShapeDtypeStruct((S5_NOCT, cw, ow), BF16)] * 2
                  + [jax.ShapeDtypeStruct((S5_NOCT, ow, cw), BF16)] * 2,
        compiler_params=_params("arbitrary"),
        name="s5_operators",
    )(bt_re, bt_im, cb_re, cb_im, lanes(pw_r), lanes(pw_i), rows(pw_r), rows(pw_i))
    return {
        "t": t_mat, "w_re": w_re, "w_im": w_im, "m_re": m_re, "m_im": m_im,
        "a_re": pw_r[steps].reshape(1, S5_STATE_W), "a_im": pw_i[steps].reshape(1, S5_STATE_W),
    }


def _softmax_update_t(s, vt, m_ref, l_ref, acc_ref):
    m_prev = m_ref[...]
    m_new = jnp.maximum(m_prev, jnp.max(s, axis=0, keepdims=True))
    alpha = jnp.exp2(m_prev - m_new)
    p = jnp.exp2(s - m_new)
    l_ref[...] = alpha * l_ref[...] + jnp.sum(p, axis=0, keepdims=True)
    acc_ref[...] = alpha * acc_ref[...] + _dot(vt, p.astype(BF16))
    m_ref[...] = m_new


def _attn_prompt_kernel(qt_ref, k_ref, vt_ref, kmeta_ref, vmeta_t_ref, wuvt_ref, o_ref, m_ref, l_ref, acc_ref, s_ref,
                        *, tq):
    qi = pl.program_id(1)
    qt = jnp.concatenate([qt_ref[h] for h in range(MLA_HEADS)], axis=1)

    def keys(j):
        return pl.ds(pl.multiple_of(j * tq, tq), tq)

    def scores(j):
        return _dot(k_ref[0, keys(j), :], qt)

    s = _dot(kmeta_ref[...], qt)
    m0 = jnp.max(s, axis=0, keepdims=True)
    p = jnp.exp2(s - m0)
    m_ref[...] = m0
    l_ref[...] = jnp.sum(p, axis=0, keepdims=True)
    acc_ref[...] = _dot(vmeta_t_ref[...], p.astype(BF16))

    s_ref[...] = scores(0)

    def body(kj, carry):
        s_next = scores(kj + 1)
        _softmax_update_t(s_ref[...], vt_ref[:, keys(kj)], m_ref, l_ref, acc_ref)
        s_ref[...] = s_next
        return carry

    lax.fori_loop(0, qi, body, 0)

    s = s_ref[...]
    t_k = lax.broadcasted_iota(jnp.int32, s.shape, 0)
    t_q = lax.broadcasted_iota(jnp.int32, s.shape, 1) % tq
    s = jnp.where(t_k <= t_q, s, NEG_INF)
    _softmax_update_t(s, vt_ref[:, keys(qi)], m_ref, l_ref, acc_ref)

    ot = (acc_ref[...] / l_ref[...]).astype(BF16)
    for h in range(MLA_HEADS):
        yt = _dot(wuvt_ref[h], ot[:, h * tq:(h + 1) * tq])
        o_ref[:, h * V_HEAD:(h + 1) * V_HEAD] = yt.T.astype(o_ref.dtype)


def _attn_prompt(qt, kcat, vt, kmeta, vmeta_t, w_uv_t, nb, tq):
    t = kcat.shape[1]
    nq = t // tq
    cols = MLA_HEADS * tq
    return pl.pallas_call(
        functools.partial(_attn_prompt_kernel, tq=tq),
        grid=(nb, nq),
        in_specs=[
            pl.BlockSpec((MLA_HEADS, QK_CAT, tq), lambda b, i: (0, 0, b * nq + i)),
            pl.BlockSpec((1, t, QK_CAT), lambda b, i: (b, 0, 0)),
            pl.BlockSpec((KV_RANK, t), lambda b, i: (0, b)),
            _const_spec(kmeta.shape),
            _const_spec(vmeta_t.shape),
            _const_spec(w_uv_t.shape),
        ],
        out_specs=pl.BlockSpec((tq, MLA_HEADS * V_HEAD), lambda b, i: (b * nq + i, 0)),
        out_shape=jax.ShapeDtypeStruct((nb * t, MLA_HEADS * V_HEAD), BF16),
        scratch_shapes=[pltpu.VMEM((1, cols), F32), pltpu.VMEM((1, cols), F32), pltpu.VMEM((KV_RANK, cols), F32),
                        pltpu.VMEM((tq, cols), F32)],
        compiler_params=_params("arbitrary", "arbitrary"),
        name="attn_prompt",
    )(qt, kcat, vt, kmeta, vmeta_t, w_uv_t)


def _attn_meta_kernel(q_ref, kmeta_ref, wuv_ref, o_ref):
    rows = MLA_HEADS * N_META
    q = q_ref[...].reshape(rows, QK_CAT)
    km = kmeta_ref[...]
    s = _dot_nt(q, km)
    t_q = lax.broadcasted_iota(jnp.int32, s.shape, 0) % N_META
    t_k = lax.broadcasted_iota(jnp.int32, s.shape, 1)
    s = jnp.where(t_k <= t_q, s, NEG_INF)
    p = jnp.exp(s - jnp.max(s, axis=1, keepdims=True))
    o = _dot(p.astype(BF16), km[:, :KV_RANK]) / jnp.sum(p, axis=1, keepdims=True)
    o = o.astype(BF16)
    for h in range(MLA_HEADS):
        o_ref[:, h * V_HEAD:(h + 1) * V_HEAD] = _dot(o[h * N_META:(h + 1) * N_META], wuv_ref[h]).astype(o_ref.dtype)


def _attn_meta(q, kmeta, w_uv):
    return pl.pallas_call(
        _attn_meta_kernel,
        out_shape=jax.ShapeDtypeStruct((N_META, MLA_HEADS * V_HEAD), BF16),
        name="attn_meta",
    )(q, kmeta, w_uv)


def _attn_sample_kernel(pt_ref, q_ref, knew_ref, wuv_ref, ckv_hbm, kr_hbm, o_ref, ckv_buf, kr_buf, sem,
                        *, n_pages, steps):
    b = pl.program_id(0)
    slot = b % 2

    def page_copies(seq, into, j):
        page = pt_ref[seq, j]
        return (pltpu.make_async_copy(ckv_hbm.at[0, page], ckv_buf.at[into, j], sem.at[into, 0]),
                pltpu.make_async_copy(kr_hbm.at[0, page], kr_buf.at[into, j], sem.at[into, 1]))

    def for_each_page(seq, into, act):
        def body(j, carry):
            for copy in page_copies(seq, into, j):
                act(copy)
            return carry
        lax.fori_loop(0, n_pages, body, 0, unroll=4)

    @pl.when(b == 0)
    def _():
        for_each_page(0, 0, lambda copy: copy.start())

    @pl.when(b + 1 < pl.num_programs(0))
    def _():
        for_each_page(b + 1, 1 - slot, lambda copy: copy.start())

    for_each_page(b, slot, lambda copy: copy.wait())

    rows = MLA_HEADS * steps
    q = q_ref[...].reshape(rows, QK_CAT).astype(BF16)
    q_lat = q[:, :KV_RANK]
    q_rope = q[:, KV_RANK:KV_RANK + QK_ROPE]
    groups = range(0, n_pages, PAGE_GROUP)

    def latent_keys(j0):
        return jnp.concatenate([ckv_buf[slot, j].astype(BF16) for j in range(j0, j0 + PAGE_GROUP)], axis=0)

    def chunk_scores(c):
        parts = []
        for j0 in groups[c * CHUNK_GROUPS:(c + 1) * CHUNK_GROUPS]:
            kr_t = jnp.concatenate([kr_buf[slot, j].astype(BF16) for j in range(j0, j0 + PAGE_GROUP)], axis=1)
            parts.append(_dot_nt(q_lat, latent_keys(j0)) + _dot(q_rope, kr_t))
        return jnp.concatenate(parts, axis=1)

    kn = jnp.concatenate([knew_ref[...], jnp.zeros((PAGE_SIZE - steps, QK_CAT), F32)], axis=0).astype(BF16)
    s_new = _dot_nt(q, kn)
    t_q = lax.broadcasted_iota(jnp.int32, s_new.shape, 0) % steps
    t_k = lax.broadcasted_iota(jnp.int32, s_new.shape, 1)
    s_new = jnp.where(t_k <= t_q, s_new, NEG_INF)

    gk = PAGE_GROUP * PAGE_SIZE
    n_chunks = len(groups) // CHUNK_GROUPS
    m = jnp.full((rows, 1), NEG_INF, F32)
    l = jnp.zeros((rows, 1), F32)
    acc = jnp.zeros((rows, KV_RANK), F32)
    s_cur = chunk_scores(0)
    for c in range(n_chunks + 1):
        s_next = chunk_scores(c + 1) if c + 1 < n_chunks else s_new
        m_new = jnp.maximum(m, jnp.max(s_cur, axis=1, keepdims=True))
        alpha = jnp.exp(m - m_new)
        p = jnp.exp(s_cur - m_new)
        l = alpha * l + jnp.sum(p, axis=1, keepdims=True)
        p = p.astype(BF16)
        if c < n_chunks:
            chunk_groups = groups[c * CHUNK_GROUPS:(c + 1) * CHUNK_GROUPS]
            pv = _dot(p[:, :gk], latent_keys(chunk_groups[0]))
            for i, j0 in enumerate(chunk_groups[1:], start=1):
                pv = pv + _dot(p[:, i * gk:(i + 1) * gk], latent_keys(j0))
        else:
            pv = _dot(p, kn[:, :KV_RANK])
        acc = alpha * acc + pv
        m = m_new
        s_cur = s_next
    o = (acc / l).astype(BF16)
    for h in range(MLA_HEADS):
        o_ref[:, h * V_HEAD:(h + 1) * V_HEAD] = _dot(o[h * steps:(h + 1) * steps], wuv_ref[h])


def _attn_sample(page_table, q, knew, cache_ckv, cache_kr_t, w_uv, steps):
    nb, n_pages = page_table.shape
    assert n_pages % (PAGE_GROUP * CHUNK_GROUPS) == 0
    grid_spec = pltpu.PrefetchScalarGridSpec(
        num_scalar_prefetch=1,
        grid=(nb,),
        in_specs=[pl.BlockSpec((MLA_HEADS, steps, QK_CAT), lambda b, pt: (0, b, 0)),
                  pl.BlockSpec((steps, QK_CAT), lambda b, pt: (b, 0)),
                  pl.BlockSpec(w_uv.shape, lambda b, pt: (0, 0, 0)),
                  pl.BlockSpec(memory_space=pl.ANY),
                  pl.BlockSpec(memory_space=pl.ANY)],
        out_specs=pl.BlockSpec((steps, MLA_HEADS * V_HEAD), lambda b, pt: (b, 0)),
        scratch_shapes=[pltpu.VMEM((2, n_pages, PAGE_SIZE, KV_RANK), F32),
                        pltpu.VMEM((2, n_pages, QK_ROPE, PAGE_SIZE), F32),
                        pltpu.SemaphoreType.DMA((2, 2))],
    )
    return pl.pallas_call(
        functools.partial(_attn_sample_kernel, n_pages=n_pages, steps=steps),
        grid_spec=grid_spec,
        out_shape=jax.ShapeDtypeStruct((nb * steps, MLA_HEADS * V_HEAD), F32),
        compiler_params=_params("arbitrary"),
        name="attn_sample",
    )(page_table, q, knew, w_uv, cache_ckv, cache_kr_t)


def _gelu_tanh(x):
    return 0.5 * x * (1.0 + jnp.tanh(math.sqrt(2.0 / math.pi) * (x + 0.044715 * (x * x * x))))


def _sigmoid(x):
    return 1.0 / (1.0 + jnp.exp(-x))


def _post_kernel(x_ref, yt_ref, u_ref, ymla_ref, p1_ref, p2_ref, d_ref, wglu_ref, wout_ref, gffn_ref,
                 wgate_ref, wup_ref, convw_ref, convb_ref, wdown_ref, gfin_ref,
                 y_ref, gate_ref, ybuf, hbuf, carry_ref, *, tm, seq_steps):
    long_seq = seq_steps == 0
    for o in range(S5_NOCT):
        for t in range(S5_CHUNK):
            ybuf[o, pl.ds(t, tm // S5_CHUNK, stride=S5_CHUNK), :] = yt_ref[o, t]
    y_ssm = jnp.concatenate([ybuf[o] for o in range(S5_NOCT)], axis=1)
    y = y_ssm + d_ref[...] * u_ref[...]
    y = _gelu_tanh(y)
    y = y * _sigmoid(_dot(y.astype(BF16), wglu_ref[...]))
    mixed = _dot(y.astype(BF16), wout_ref[:S5_W, :]) + _dot(ymla_ref[...].astype(BF16), wout_ref[S5_W:, :])
    x1 = x_ref[...] + mixed
    xn = _rms(x1, gffn_ref[...]).astype(BF16)

    row = lax.broadcasted_iota(jnp.int32, (tm, FF_CHUNK), 0)
    if long_seq:
        @pl.when(pl.program_id(1) == 0)
        def _():
            carry_ref[...] = p1_ref[...]
        t_in = row
    else:
        t_in = row % seq_steps

    for c in range(D_FF // FF_CHUNK):
        sl = slice(c * FF_CHUNK, (c + 1) * FF_CHUNK)
        gate = _dot(xn, wgate_ref[:, sl])
        up = _dot(xn, wup_ref[:, sl])
        if long_seq:
            back1 = jnp.broadcast_to(carry_ref[7:8, sl], gate.shape)
            back2 = jnp.where(t_in == 0, carry_ref[6:7, sl], back1)
            carry_ref[:, sl] = gate[tm - 8:, :]
            gate_ref[:, sl] = gate[tm - 8:, :]
        else:
            back1 = p1_ref[:, sl]
            back2 = p2_ref[:, sl]
            gate_ref[:, sl] = gate
        prev1 = jnp.where(t_in >= 1, pltpu.roll(gate, 1, 0), back1)
        prev2 = jnp.where(t_in >= 2, pltpu.roll(gate, 2, 0), back2)
        conv = (convb_ref[:, sl] + convw_ref[0:1, sl] * prev2 + convw_ref[1:2, sl] * prev1
                + convw_ref[2:3, sl] * gate)
        hbuf[:, sl] = (conv * _sigmoid(conv) * up).astype(BF16)

    x2 = x1 + _dot(hbuf[...], wdown_ref[...])
    y_ref[...] = _rms(x2, gfin_ref[...])


def _post_call(x2d, yt, u, y_mla, p1, p2, w, nb, tm, seq_steps):
    rows = x2d.shape[0]
    nt = rows // (nb * tm)
    assert nb * nt * tm == rows and tm % S5_CHUNK == 0
    long_seq = seq_steps == 0
    row = lambda width: pl.BlockSpec((tm, width), lambda b, i: (b * nt + i, 0))
    yt_spec = pl.BlockSpec((S5_NOCT, S5_CHUNK, tm // S5_CHUNK, LANE), lambda b, i: (0, 0, b * nt + i, 0))
    if long_seq:
        hist = [_const_spec((8, D_FF)), _const_spec((8, D_FF))]
        gate_spec = pl.BlockSpec((None, 8, D_FF), lambda b, i: (b, 0, 0))
        gate_shape = jax.ShapeDtypeStruct((nb, 8, D_FF), F32)
    else:
        hist = [row(D_FF), row(D_FF)]
        gate_spec = row(D_FF)
        gate_shape = jax.ShapeDtypeStruct((rows, D_FF), F32)
    return pl.pallas_call(
        functools.partial(_post_kernel, tm=tm, seq_steps=seq_steps),
        grid=(nb, nt),
        in_specs=[row(D_MODEL), yt_spec, row(S5_W), row(MLA_HEADS * V_HEAD)] + hist + [
            _const_spec((1, S5_W)), _const_spec((S5_W, S5_W)), _const_spec((D_MODEL, D_MODEL)),
            _const_spec((1, D_MODEL)), _const_spec((D_MODEL, D_FF)), _const_spec((D_MODEL, D_FF)),
            _const_spec((CONV_W, D_FF)), _const_spec((1, D_FF)), _const_spec((D_FF, D_MODEL)),
            _const_spec((1, D_MODEL))],
        out_specs=[row(D_MODEL), gate_spec],
        out_shape=[jax.ShapeDtypeStruct((rows, D_MODEL), F32), gate_shape],
        scratch_shapes=[pltpu.VMEM((S5_NOCT, tm, LANE), F32), pltpu.VMEM((tm, D_FF), BF16),
                        pltpu.VMEM((8, D_FF), F32)],
        compiler_params=_params("arbitrary", "arbitrary"),
        name="post",
    )(x2d, yt, u, y_mla, p1, p2, w["s5_d"], w["w_glu"], w["w_out"], w["g_ffn"], w["w_gate"], w["w_up"],
      w["conv_w"], w["conv_b"], w["w_down"], w["g_final"])


def _rope_tables(pos):
    half = QK_ROPE // 2
    inv = ROPE_BASE ** (-jnp.arange(half, dtype=F32) / half)
    ang = pos.astype(F32)[:, None] * inv[None, :]
    cos, sin = jnp.cos(ang), jnp.sin(ang)
    pad = jnp.zeros((pos.shape[0], ROPE_PAD - QK_ROPE), F32)
    return jnp.concatenate([cos, cos, pad], axis=1), jnp.concatenate([-sin, sin, pad], axis=1)


def kernel(x_prompt, x_sample, cache_ckv, cache_kr, state_s5_re, state_s5_im, state_conv, page_table, meta_tokens, g_mix, w_in, g_q, w_uq, g_kv, w_uk, w_uv, s5_a_re, s5_a_im, s5_log_dt, s5_b_re, s5_b_im, s5_c_re, s5_c_im, s5_d, w_glu, w_out, g_ffn, w_gate, w_up, conv_w, conv_b, w_down, g_final):
    assert w_in.shape[0] == 1, "single-layer step"
    nb, seq, _ = x_prompt.shape
    db, steps, _ = x_sample.shape
    n_past = page_table.shape[1] * PAGE_SIZE
    assert seq % S5_CHUNK == 0 and N_META % S5_CHUNK == 0 and steps == S5_CHUNK

    w_in_pad = jnp.concatenate([w_in[0], jnp.zeros((D_MODEL, IN_PAD - w_in.shape[2]), F32)], axis=1)
    wq = w_uq[0].reshape(Q_RANK, MLA_HEADS, QK_NOPE + QK_ROPE)
    wq_rope = jnp.concatenate([wq[:, :, QK_NOPE:], jnp.zeros((Q_RANK, MLA_HEADS, ROPE_PAD - QK_ROPE), F32)], axis=2)
    wq_perm = jnp.concatenate([wq[:, :, :QK_NOPE].reshape(Q_RANK, -1), wq_rope.reshape(Q_RANK, -1)], axis=1)
    w = {
        "g_mix": g_mix[0].reshape(1, -1), "w_in": w_in_pad.astype(BF16),
        "g_q": g_q[0].reshape(1, -1), "w_uq": wq_perm.astype(BF16), "w_uq_t": wq_perm.T.astype(BF16),
        "w_uk_t": w_uk[0].transpose(1, 2, 0).astype(BF16),
        "w_uk": w_uk[0].transpose(1, 0, 2).astype(BF16),
        "g_kv": g_kv[0].reshape(1, -1),
        "s5_d": s5_d[0].reshape(1, -1), "w_glu": w_glu[0].astype(BF16), "w_out": w_out[0].astype(BF16),
        "g_ffn": g_ffn[0].reshape(1, -1), "w_gate": w_gate[0].astype(BF16), "w_up": w_up[0].astype(BF16),
        "conv_w": conv_w[0], "conv_b": conv_b[0].reshape(1, -1), "w_down": w_down[0].astype(BF16),
        "g_final": g_final.reshape(1, -1),
    }
    w_uv_h = w_uv[0].transpose(1, 0, 2).astype(BF16)
    ops = _s5_operators(s5_a_re[0], s5_a_im[0], s5_log_dt[0], s5_b_re[0], s5_b_im[0], s5_c_re[0], s5_c_im[0])

    cos_m, sin_m = _rope_tables(jnp.arange(N_META, dtype=jnp.int32))
    cos_p, sin_p = _rope_tables(N_META + jnp.arange(seq, dtype=jnp.int32))
    tm_p = 512
    tm_s = 512
    cos_s, sin_s = _rope_tables(n_past + jnp.arange(tm_s, dtype=jnp.int32) % steps)

    tiles_per_seq = seq // tm_p
    u_m, ut_m, q_m, kcat_m, ckv_m, kr_m = _pre_call(meta_tokens, cos_m, sin_m, lambda i: (0, 0), N_META, BF16, w)
    u_p, ut_p, qt_p, kcat_p, ckv_p, kr_p, vt_p = _pre_call(
        x_prompt.reshape(nb * seq, D_MODEL), cos_p, sin_p, lambda i: (i % tiles_per_seq, 0), tm_p, BF16, w,
        q_transposed=True)
    u_s, ut_s, q_s, kcat_s, ckv_s, kr_s = _pre_call(x_sample.reshape(db * steps, D_MODEL), cos_s, sin_s,
                                                    lambda i: (0, 0), tm_s, F32, w)

    zero_state = jnp.zeros((1, 1, S5_STATE_W), F32)
    yt_m, fm_re, fm_im = _s5_mixer(ut_m, zero_state, zero_state, ops, 1)
    yt_p, s5_re_p, s5_im_p = _s5_mixer(ut_p, jnp.broadcast_to(fm_re, (nb, 1, S5_STATE_W)),
                                       jnp.broadcast_to(fm_im, (nb, 1, S5_STATE_W)), ops, nb)
    yt_s, s5_re_s, s5_im_s = _s5_mixer(ut_s, state_s5_re[0].reshape(db, 1, S5_STATE_W),
                                       state_s5_im[0].reshape(db, 1, S5_STATE_W), ops, db)

    kmeta = jnp.concatenate([kcat_m, jnp.zeros((PAGE_SIZE - N_META, QK_CAT), BF16)], axis=0)
    ymla_m = _attn_meta(q_m, kmeta, w_uv_h)
    ymla_p = _attn_prompt(qt_p, kcat_p.reshape(nb, seq, QK_CAT), vt_p, kcat_m, kcat_m[:, :KV_RANK].T,
                          w_uv[0].transpose(1, 2, 0).astype(BF16), nb, tq=256)
    ymla_s = _attn_sample(page_table, q_s, kcat_s, cache_ckv, jnp.swapaxes(cache_kr, 2, 3), w_uv_h, steps)

    zeros_hist = jnp.zeros((8, D_FF), F32)
    _, gate_m = _post_call(meta_tokens, yt_m, u_m, ymla_m, zeros_hist, zeros_hist, w, 1, N_META, 0)
    y_p, gate_p = _post_call(x_prompt.reshape(nb * seq, D_MODEL), yt_p, u_p, ymla_p, gate_m[0], gate_m[0],
                             w, nb, tm_p, 0)
    sc = state_conv[0]
    back1 = jnp.pad(sc[:, 1:2], ((0, 0), (0, steps - 1), (0, 0))).reshape(db * steps, D_FF)
    back2 = jnp.pad(sc, ((0, 0), (0, steps - 2), (0, 0))).reshape(db * steps, D_FF)
    y_s, gate_s = _post_call(x_sample.reshape(db * steps, D_MODEL), yt_s, u_s, ymla_s, back1, back2,
                             w, 1, tm_s // 2, steps)

    def with_meta(meta_rows, tok_rows, width):
        return jnp.concatenate([jnp.broadcast_to(meta_rows[None], (nb, N_META, width)),
                                tok_rows.reshape(nb, seq, width)], axis=1)[None]

    return (y_p.reshape(nb, seq, D_MODEL), y_s.reshape(db, steps, D_MODEL),
            with_meta(ckv_m, ckv_p, KV_RANK), with_meta(kr_m, kr_p, QK_ROPE),
            s5_re_p.reshape(1, nb, S5_GROUPS, S5_STATE), s5_im_p.reshape(1, nb, S5_GROUPS, S5_STATE),
            gate_p[:, 8 - (CONV_W - 1):][None],
            ckv_s.reshape(1, db, steps, KV_RANK), kr_s.reshape(1, db, steps, QK_ROPE),
            s5_re_s.reshape(1, db, S5_GROUPS, S5_STATE), s5_im_s.reshape(1, db, S5_GROUPS, S5_STATE),
            gate_s.reshape(db, steps, D_FF)[:, steps - (CONV_W - 1):][None])
```

```python
import functools
import math

import jax
import jax.numpy as jnp
from jax import lax
from jax.experimental import pallas as pl
from jax.experimental.pallas import tpu as pltpu

F32 = jnp.float32
BF16 = jnp.bfloat16

D_MODEL = 1024
N_META = 16
S5_W = 512
S5_GROUP = 16
S5_GROUPS = 32
S5_STATE = 64
MLA_HEADS = 4
QK_NOPE = 128
QK_ROPE = 64
V_HEAD = 128
Q_RANK = 384
KV_RANK = 256
D_FF = 2816
CONV_W = 3
ROPE_BASE = 10000.0
EPS = 1e-6
PAGE_SIZE = 128
ATTN_SCALE = 1.0 / math.sqrt(QK_NOPE + QK_ROPE)
LOG2_E = math.log2(math.e)

LANE = 128
ROPE_PAD = LANE
QK_CAT = KV_RANK + ROPE_PAD
IN_PAD = S5_W + Q_RANK + KV_RANK + ROPE_PAD
S5_CHUNK = 8
S5_OCT = LANE // S5_GROUP
S5_NOCT = S5_GROUPS // S5_OCT
S5_STATE_W = S5_GROUPS * S5_STATE
S5_SCAN_W = 512
S5_POW_PAD = 16
FF_CHUNK = 256
PAGE_GROUP = 4
CHUNK_GROUPS = 4
KEY_BLOCKS = 2
VMEM_LIMIT = 56 * 1024 * 1024
NEG_INF = float("-inf")
HI = lax.Precision.HIGHEST


def _const_spec(shape):
    nd = len(shape)
    return pl.BlockSpec(shape, lambda *_: (0,) * nd, pipeline_mode=pl.Buffered(1))


def _params(*semantics):
    return pltpu.CompilerParams(dimension_semantics=semantics, vmem_limit_bytes=VMEM_LIMIT)


def _rms(x, g):
    return x * lax.rsqrt(jnp.mean(x * x, axis=-1, keepdims=True) + EPS) * g


def _rope_slab(x, cos, sin):
    lane = lax.broadcasted_iota(jnp.int32, x.shape, 1)
    half = QK_ROPE // 2
    swapped = jnp.where(lane < half, pltpu.roll(x, LANE - half, 1), pltpu.roll(x, half, 1))
    return x * cos + swapped * sin


def _dot(a, b):
    return jnp.dot(a, b, preferred_element_type=F32)


def _dot_nt(a, b):
    return lax.dot_general(a, b, (((1,), (1,)), ((), ())), preferred_element_type=F32)


def _pre_kernel(x_ref, cos_ref, sin_ref, cost_ref, sint_ref, gmix_ref, win_ref, gq_ref, wuq_ref, wuk_ref, gkv_ref,
                u_ref, ut_ref, q_ref, kcat_ref, ckv_ref, kr_ref, *rest, q_transposed):
    x = x_ref[...]
    xn = _rms(x, gmix_ref[...]).astype(BF16)
    z = _dot(xn, win_ref[...])
    u_ref[...] = z[:, :S5_W]
    slab_ref = rest[-1]
    chunks = u_ref.shape[0] // S5_CHUNK
    for o in range(S5_NOCT):
        slab_ref[o] = z[:, o * LANE:(o + 1) * LANE]
        for s in range(S5_CHUNK):
            ut_ref[o, s] = slab_ref[o, pl.ds(s, chunks, stride=S5_CHUNK), :].astype(BF16)
    cq = z[:, S5_W:S5_W + Q_RANK]
    ckv_raw = z[:, S5_W + Q_RANK:S5_W + Q_RANK + KV_RANK]
    kr_raw = z[:, S5_W + Q_RANK + KV_RANK:]
    cqn = _rms(cq, gq_ref[...]).astype(BF16)
    nope_w = MLA_HEADS * QK_NOPE
    half = QK_ROPE // 2
    if q_transposed:
        qt = _dot_nt(wuq_ref[...], cqn)
        cost = cost_ref[...]
        sint = sint_ref[...]
        for h in range(MLA_HEADS):
            qn = qt[h * QK_NOPE:(h + 1) * QK_NOPE].astype(BF16)
            q_lat = _dot(wuk_ref[h], qn) * (ATTN_SCALE * LOG2_E)
            xr = qt[nope_w + h * ROPE_PAD:nope_w + (h + 1) * ROPE_PAD]
            swapped = jnp.concatenate([xr[half:QK_ROPE], xr[:half], xr[QK_ROPE:]], axis=0)
            qr = (xr * cost + swapped * sint) * (ATTN_SCALE * LOG2_E)
            q_ref[h, :KV_RANK, :] = q_lat.astype(q_ref.dtype)
            q_ref[h, KV_RANK:, :] = qr.astype(q_ref.dtype)
    else:
        q = _dot(cqn, wuq_ref[...])
        for h in range(MLA_HEADS):
            qn = q[:, h * QK_NOPE:(h + 1) * QK_NOPE].astype(BF16)
            q_lat = _dot(qn, wuk_ref[h]) * ATTN_SCALE
            qr = _rope_slab(q[:, nope_w + h * ROPE_PAD:nope_w + (h + 1) * ROPE_PAD],
                            cos_ref[...], sin_ref[...]) * ATTN_SCALE
            q_ref[h, :, :KV_RANK] = q_lat.astype(q_ref.dtype)
            q_ref[h, :, KV_RANK:] = qr.astype(q_ref.dtype)
    ckv = _rms(ckv_raw, gkv_ref[...])
    kr = _rope_slab(kr_raw, cos_ref[...], sin_ref[...])
    ckv_ref[...] = ckv
    kr_ref[...] = kr[:, :QK_ROPE]
    kcat_ref[:, :KV_RANK] = ckv.astype(kcat_ref.dtype)
    kcat_ref[:, KV_RANK:] = kr.astype(kcat_ref.dtype)
    if q_transposed:
        rest[0][...] = ckv.T.astype(BF16)


def _pre_call(x2d, cos, sin, tab_map, tm, qdtype, w, q_transposed=False):
    rows = x2d.shape[0]
    assert rows % tm == 0 and tm % S5_CHUNK == 0
    row = lambda width: pl.BlockSpec((tm, width), lambda i: (i, 0))
    tab_map_t = lambda i: tab_map(i)[::-1]
    if q_transposed:
        wuq, wuk = w["w_uq_t"], w["w_uk"]
        q_spec = pl.BlockSpec((MLA_HEADS, QK_CAT, tm), lambda i: (0, 0, i))
        q_shape = jax.ShapeDtypeStruct((MLA_HEADS, QK_CAT, rows), qdtype)
        extra_specs = [pl.BlockSpec((KV_RANK, tm), lambda i: (0, i))]
        extra_shapes = [jax.ShapeDtypeStruct((KV_RANK, rows), BF16)]
    else:
        wuq, wuk = w["w_uq"], w["w_uk_t"]
        q_spec = pl.BlockSpec((MLA_HEADS, tm, QK_CAT), lambda i: (0, i, 0))
        q_shape = jax.ShapeDtypeStruct((MLA_HEADS, rows, QK_CAT), qdtype)
        extra_specs, extra_shapes = [], []
    return pl.pallas_call(
        functools.partial(_pre_kernel, q_transposed=q_transposed),
        grid=(rows // tm,),
        in_specs=[
            row(D_MODEL),
            pl.BlockSpec((tm, ROPE_PAD), tab_map),
            pl.BlockSpec((tm, ROPE_PAD), tab_map),
            pl.BlockSpec((ROPE_PAD, tm), tab_map_t),
            pl.BlockSpec((ROPE_PAD, tm), tab_map_t),
            _const_spec((1, D_MODEL)),
            _const_spec((D_MODEL, IN_PAD)),
            _const_spec((1, Q_RANK)),
            _const_spec(wuq.shape),
            _const_spec(wuk.shape),
            _const_spec((1, KV_RANK)),
        ],
        out_specs=[row(S5_W), pl.BlockSpec((S5_NOCT, S5_CHUNK, tm // S5_CHUNK, LANE), lambda i: (0, 0, i, 0)),
                   q_spec, row(QK_CAT), row(KV_RANK), row(QK_ROPE)] + extra_specs,
        out_shape=[
            jax.ShapeDtypeStruct((rows, S5_W), F32),
            jax.ShapeDtypeStruct((S5_NOCT, S5_CHUNK, rows // S5_CHUNK, LANE), BF16),
            q_shape,
            jax.ShapeDtypeStruct((rows, QK_CAT), qdtype),
            jax.ShapeDtypeStruct((rows, KV_RANK), F32),
            jax.ShapeDtypeStruct((rows, QK_ROPE), F32),
        ] + extra_shapes,
        scratch_shapes=[pltpu.VMEM((S5_NOCT, tm, LANE), F32)],
        compiler_params=_params("arbitrary"),
        name="pre",
    )(x2d, cos, sin, cos.T, sin.T, w["g_mix"], w["w_in"], w["g_q"], wuq, wuk, w["g_kv"])


def _chunk_inputs(ut_ref):
    return jnp.concatenate([ut_ref[s] for s in range(S5_CHUNK)], axis=1)


def _s5_state_in_kernel(ut_ref, wre_ref, wim_ref, sre_ref, sim_ref):
    u = _chunk_inputs(ut_ref)
    sre_ref[...] = _dot(u, wre_ref[0])
    sim_ref[...] = _dot(u, wim_ref[0])


def _s5_scan_kernel(sre_ref, sim_ref, h0re_ref, h0im_ref, are_ref, aim_ref, hre_ref, him_ref, fre_ref, fim_ref):
    nb, nc, width = sre_ref.shape
    ar = are_ref[...].reshape(1, 1, width)
    ai = aim_ref[...].reshape(1, 1, width)

    def body(c, carry):
        hr, hi = carry
        hre_ref[:, pl.ds(c, 1), :] = hr
        him_ref[:, pl.ds(c, 1), :] = hi
        sr = sre_ref[:, pl.ds(c, 1), :]
        si = sim_ref[:, pl.ds(c, 1), :]
        return ar * hr - ai * hi + sr, ar * hi + ai * hr + si

    hr, hi = lax.fori_loop(0, nc, body, (h0re_ref[...], h0im_ref[...]))
    fre_ref[...] = hr
    fim_ref[...] = hi


def _s5_out_kernel(ut_ref, hre_ref, him_ref, t_ref, mre_ref, mim_ref, yt_ref):
    y = (_dot(_chunk_inputs(ut_ref), t_ref[0]) + _dot(hre_ref[...].astype(BF16), mre_ref[0])
         + _dot(him_ref[...].astype(BF16), mim_ref[0]))
    for t in range(S5_CHUNK):
        yt_ref[t] = y[:, t * LANE:(t + 1) * LANE]


def _s5_mixer(ut, h0_re, h0_im, ops, nb):
    rows = ut.shape[2]
    nc = rows // nb
    tr = min(rows, 1024)
    assert rows % tr == 0
    ow = S5_OCT * S5_STATE
    cw = S5_CHUNK * LANE
    ut_spec = pl.BlockSpec((None, S5_CHUNK, tr, LANE), lambda q, r: (q, 0, r, 0))
    st_spec = pl.BlockSpec((tr, ow), lambda q, r: (r, q))
    oct_spec = lambda a, b: pl.BlockSpec((1, a, b), lambda q, r: (q, 0, 0))
    s_re, s_im = pl.pallas_call(
        _s5_state_in_kernel,
        grid=(S5_NOCT, rows // tr),
        in_specs=[ut_spec, oct_spec(cw, ow), oct_spec(cw, ow)],
        out_specs=[st_spec, st_spec],
        out_shape=[jax.ShapeDtypeStruct((rows, S5_STATE_W), F32)] * 2,
        compiler_params=_params("arbitrary", "arbitrary"),
        name="s5_state_in",
    )(ut, ops["w_re"], ops["w_im"])

    blk = pl.BlockSpec((nb, nc, S5_SCAN_W), lambda j: (0, 0, j))
    one = pl.BlockSpec((nb, 1, S5_SCAN_W), lambda j: (0, 0, j))
    vec = pl.BlockSpec((1, S5_SCAN_W), lambda j: (0, j))
    h_re, h_im, f_re, f_im = pl.pallas_call(
        _s5_scan_kernel,
        grid=(S5_STATE_W // S5_SCAN_W,),
        in_specs=[blk, blk, one, one, vec, vec],
        out_specs=[blk, blk, one, one],
        out_shape=[jax.ShapeDtypeStruct((nb, nc, S5_STATE_W), F32)] * 2
                  + [jax.ShapeDtypeStruct((nb, 1, S5_STATE_W), F32)] * 2,
        compiler_params=_params("arbitrary"),
        name="s5_scan",
    )(s_re.reshape(nb, nc, S5_STATE_W), s_im.reshape(nb, nc, S5_STATE_W), h0_re, h0_im, ops["a_re"], ops["a_im"])

    yt = pl.pallas_call(
        _s5_out_kernel,
        grid=(S5_NOCT, rows // tr),
        in_specs=[ut_spec, st_spec, st_spec, oct_spec(cw, cw), oct_spec(ow, cw), oct_spec(ow, cw)],
        out_specs=ut_spec,
        out_shape=jax.ShapeDtypeStruct(ut.shape, F32),
        compiler_params=_params("arbitrary", "arbitrary"),
        name="s5_out",
    )(ut, h_re.reshape(rows, S5_STATE_W), h_im.reshape(rows, S5_STATE_W), ops["t"], ops["m_re"], ops["m_im"])
    return yt, f_re, f_im


def _s5_operator_kernel(btr_ref, bti_ref, cr_ref, ci_ref, pl_re, pl_im, pr_re, pr_im,
                        t_ref, wre_ref, wim_ref, mre_ref, mim_ref):
    btr, bti = btr_ref[0], bti_ref[0]
    cr, ci = cr_ref[0], ci_ref[0]
    hi_dot = lambda a, b: jnp.dot(a, b, precision=HI, preferred_element_type=F32)
    kerns = []
    for k in range(S5_CHUNK + 1):
        ar, ai = pr_re[0, :, k:k + 1], pr_im[0, :, k:k + 1]
        cpr = cr * ar - ci * ai
        cpi = cr * ai + ci * ar
        if k < S5_CHUNK:
            kerns.append((hi_dot(btr, cpr) - hi_dot(bti, cpi)).astype(BF16))
            s = S5_CHUNK - 1 - k
            lr, li = pl_re[0, k:k + 1, :], pl_im[0, k:k + 1, :]
            wre_ref[0, s * LANE:(s + 1) * LANE, :] = (btr * lr - bti * li).astype(BF16)
            wim_ref[0, s * LANE:(s + 1) * LANE, :] = (btr * li + bti * lr).astype(BF16)
        if k >= 1:
            t = k - 1
            mre_ref[0, :, t * LANE:(t + 1) * LANE] = cpr.astype(BF16)
            mim_ref[0, :, t * LANE:(t + 1) * LANE] = (-cpi).astype(BF16)
    zero = jnp.zeros((LANE, LANE), BF16)
    for s in range(S5_CHUNK):
        for t in range(S5_CHUNK):
            t_ref[0, s * LANE:(s + 1) * LANE, t * LANE:(t + 1) * LANE] = kerns[t - s] if t >= s else zero


def _s5_operators(a_re, a_im, log_dt, b_re, b_im, c_re, c_im):
    steps = S5_CHUNK
    dt = jnp.exp(log_dt)[:, None]
    k = jnp.arange(steps + 1, dtype=F32)[:, None, None]
    mag = jnp.exp(k * (dt * a_re)[None])
    pw_r = mag * jnp.cos(k * (dt * a_im)[None])
    pw_i = mag * jnp.sin(k * (dt * a_im)[None])
    abr, abi = pw_r[1], pw_i[1]
    num_re, num_im = abr - 1.0, abi
    den = a_re * a_re + a_im * a_im
    f_re = (num_re * a_re + num_im * a_im) / den
    f_im = (num_im * a_re - num_re * a_im) / den
    bbr = f_re[..., None] * b_re - f_im[..., None] * b_im
    bbi = f_re[..., None] * b_im + f_im[..., None] * b_re
    same_group = jnp.arange(S5_OCT)[:, None, None, None] == jnp.arange(S5_OCT)[None, None, :, None]
    ow = S5_OCT * S5_STATE

    def block_diag(a):
        a = a.reshape(S5_NOCT, S5_OCT, a.shape[1], 1, a.shape[2])
        return jnp.where(same_group, a, 0.0).reshape(S5_NOCT, S5_OCT * a.shape[2], S5_OCT * a.shape[4])

    pad_k = S5_POW_PAD - (steps + 1)
    lanes = lambda p: jnp.pad(p.reshape(steps + 1, S5_NOCT, ow).transpose(1, 0, 2), ((0, 0), (0, pad_k), (0, 0)))
    rows = lambda p: jnp.pad(p.reshape(steps + 1, S5_NOCT, ow).transpose(1, 2, 0), ((0, 0), (0, 0), (0, pad_k)))
    bt_re, bt_im = block_diag(bbr.transpose(0, 2, 1)), block_diag(bbi.transpose(0, 2, 1))
    cb_re, cb_im = block_diag(c_re.transpose(0, 2, 1)), block_diag(c_im.transpose(0, 2, 1))
    cw = steps * LANE
    oct_spec = lambda a, b: pl.BlockSpec((1, a, b), lambda q: (q, 0, 0))
    t_mat, w_re, w_im, m_re, m_im = pl.pallas_call(
        _s5_operator_kernel,
        grid=(S5_NOCT,),
        in_specs=[oct_spec(LANE, ow), oct_spec(LANE, ow), oct_spec(ow, LANE), oct_spec(ow, LANE),
                  oct_spec(S5_POW_PAD, ow), oct_spec(S5_POW_PAD, ow), oct_spec(ow, S5_POW_PAD),
                  oct_spec(ow, S5_POW_PAD)],
        out_specs=[oct_spec(cw, cw), oct_spec(cw, ow), oct_spec(cw, ow), oct_spec(ow, cw), oct_spec(ow, cw)],
        out_shape=[jax.ShapeDtypeStruct((S5_NOCT, cw, cw), BF16)]
                  + [jax.ShapeDtypeStruct((S5_NOCT, cw, ow), BF16)] * 2
                  + [jax.ShapeDtypeStruct((S5_NOCT, ow, cw), BF16)] * 2,
        compiler_params=_params("arbitrary"),
        name="s5_operators",
    )(bt_re, bt_im, cb_re, cb_im, lanes(pw_r), lanes(pw_i), rows(pw_r), rows(pw_i))
    return {
        "t": t_mat, "w_re": w_re, "w_im": w_im, "m_re": m_re, "m_im": m_im,
        "a_re": pw_r[steps].reshape(1, S5_STATE_W), "a_im": pw_i[steps].reshape(1, S5_STATE_W),
    }


def _softmax_update_t(s, vt, m_ref, l_ref, acc_ref):
    m_prev = m_ref[...]
    m_new = jnp.maximum(m_prev, jnp.max(s, axis=0, keepdims=True))
    alpha = jnp.exp2(m_prev - m_new)
    p = jnp.exp2(s - m_new)
    l_ref[...] = alpha * l_ref[...] + jnp.sum(p, axis=0, keepdims=True)
    acc_ref[...] = alpha * acc_ref[...] + _dot(vt, p.astype(BF16))
    m_ref[...] = m_new


def _attn_prompt_kernel(qt_ref, k_ref, vt_ref, kmeta_ref, vmeta_t_ref, wuvt_ref, o_ref, m_ref, l_ref, acc_ref, s_ref,
                        p_ref, alpha_ref, *, tq):
    qi = pl.program_id(1)
    qt = jnp.concatenate([qt_ref[h] for h in range(MLA_HEADS)], axis=1)
    tk = KEY_BLOCKS * tq
    n_full = qi // KEY_BLOCKS

    def keys(j):
        return pl.ds(pl.multiple_of(j * tk, tk), tk)

    def scores(j):
        return _dot(k_ref[0, keys(j), :], qt)

    s = _dot(kmeta_ref[...], qt)
    m0 = jnp.max(s, axis=0, keepdims=True)
    p = jnp.exp2(s - m0)
    m_ref[...] = m0
    l_ref[...] = jnp.sum(p, axis=0, keepdims=True)
    acc_ref[...] = _dot(vmeta_t_ref[...], p.astype(BF16))

    s_ref[...] = scores(0)
    p_ref[...] = jnp.zeros(p_ref.shape, BF16)
    alpha_ref[...] = jnp.ones(alpha_ref.shape, F32)

    def apply_pending(j):
        acc_ref[...] = alpha_ref[...] * acc_ref[...] + _dot(vt_ref[:, keys(j)], p_ref[...])

    def body(kj, carry):
        s_next = scores(kj + 1)
        apply_pending(jnp.maximum(kj - 1, 0))
        s = s_ref[...]
        m_prev = m_ref[...]
        m_new = jnp.maximum(m_prev, jnp.max(s, axis=0, keepdims=True))
        alpha = jnp.exp2(m_prev - m_new)
        p = jnp.exp2(s - m_new)
        l_ref[...] = alpha * l_ref[...] + jnp.sum(p, axis=0, keepdims=True)
        m_ref[...] = m_new
        alpha_ref[...] = alpha
        p_ref[...] = p.astype(BF16)
        s_ref[...] = s_next
        return carry

    lax.fori_loop(0, n_full, body, 0)
    apply_pending(jnp.maximum(n_full - 1, 0))

    s = s_ref[...]
    t_k = n_full * tk + lax.broadcasted_iota(jnp.int32, s.shape, 0)
    t_q = qi * tq + lax.broadcasted_iota(jnp.int32, s.shape, 1) % tq
    s = jnp.where(t_k <= t_q, s, NEG_INF)
    _softmax_update_t(s, vt_ref[:, keys(n_full)], m_ref, l_ref, acc_ref)

    ot = (acc_ref[...] / l_ref[...]).astype(BF16)
    for h in range(MLA_HEADS):
        yt = _dot(wuvt_ref[h], ot[:, h * tq:(h + 1) * tq])
        o_ref[:, h * V_HEAD:(h + 1) * V_HEAD] = yt.T.astype(o_ref.dtype)


def _attn_prompt(qt, kcat, vt, kmeta, vmeta_t, w_uv_t, nb, tq):
    t = kcat.shape[1]
    nq = t // tq
    assert t % (KEY_BLOCKS * tq) == 0
    cols = MLA_HEADS * tq
    return pl.pallas_call(
        functools.partial(_attn_prompt_kernel, tq=tq),
        grid=(nb, nq),
        in_specs=[
            pl.BlockSpec((MLA_HEADS, QK_CAT, tq), lambda b, i: (0, 0, b * nq + i)),
            pl.BlockSpec((1, t, QK_CAT), lambda b, i: (b, 0, 0)),
            pl.BlockSpec((KV_RANK, t), lambda b, i: (0, b)),
            _const_spec(kmeta.shape),
            _const_spec(vmeta_t.shape),
            _const_spec(w_uv_t.shape),
        ],
        out_specs=pl.BlockSpec((tq, MLA_HEADS * V_HEAD), lambda b, i: (b * nq + i, 0)),
        out_shape=jax.ShapeDtypeStruct((nb * t, MLA_HEADS * V_HEAD), BF16),
        scratch_shapes=[pltpu.VMEM((1, cols), F32), pltpu.VMEM((1, cols), F32), pltpu.VMEM((KV_RANK, cols), F32),
                        pltpu.VMEM((KEY_BLOCKS * tq, cols), F32), pltpu.VMEM((KEY_BLOCKS * tq, cols), BF16),
                        pltpu.VMEM((1, cols), F32)],
        compiler_params=_params("arbitrary", "arbitrary"),
        name="attn_prompt",
    )(qt, kcat, vt, kmeta, vmeta_t, w_uv_t)


def _attn_meta_kernel(q_ref, kmeta_ref, wuv_ref, o_ref):
    rows = MLA_HEADS * N_META
    q = q_ref[...].reshape(rows, QK_CAT)
    km = kmeta_ref[...]
    s = _dot_nt(q, km)
    t_q = lax.broadcasted_iota(jnp.int32, s.shape, 0) % N_META
    t_k = lax.broadcasted_iota(jnp.int32, s.shape, 1)
    s = jnp.where(t_k <= t_q, s, NEG_INF)
    p = jnp.exp(s - jnp.max(s, axis=1, keepdims=True))
    o = _dot(p.astype(BF16), km[:, :KV_RANK]) / jnp.sum(p, axis=1, keepdims=True)
    o = o.astype(BF16)
    for h in range(MLA_HEADS):
        o_ref[:, h * V_HEAD:(h + 1) * V_HEAD] = _dot(o[h * N_META:(h + 1) * N_META], wuv_ref[h]).astype(o_ref.dtype)


def _attn_meta(q, kmeta, w_uv):
    return pl.pallas_call(
        _attn_meta_kernel,
        out_shape=jax.ShapeDtypeStruct((N_META, MLA_HEADS * V_HEAD), BF16),
        name="attn_meta",
    )(q, kmeta, w_uv)


def _attn_sample_kernel(pt_ref, q_ref, knew_ref, wuv_ref, ckv_hbm, kr_hbm, o_ref, ckv_buf, kr_buf, sem,
                        *, n_pages, steps):
    b = pl.program_id(0)
    slot = b % 2

    def page_copies(seq, into, j):
        page = pt_ref[seq, j]
        return (pltpu.make_async_copy(ckv_hbm.at[0, page], ckv_buf.at[into, j], sem.at[into, 0]),
                pltpu.make_async_copy(kr_hbm.at[0, page], kr_buf.at[into, j], sem.at[into, 1]))

    def for_each_page(seq, into, act):
        def body(j, carry):
            for copy in page_copies(seq, into, j):
                act(copy)
            return carry
        lax.fori_loop(0, n_pages, body, 0, unroll=4)

    @pl.when(b == 0)
    def _():
        for_each_page(0, 0, lambda copy: copy.start())

    @pl.when(b + 1 < pl.num_programs(0))
    def _():
        for_each_page(b + 1, 1 - slot, lambda copy: copy.start())

    for_each_page(b, slot, lambda copy: copy.wait())

    rows = MLA_HEADS * steps
    q = q_ref[...].reshape(rows, QK_CAT).astype(BF16)
    q_lat = q[:, :KV_RANK]
    q_rope = q[:, KV_RANK:KV_RANK + QK_ROPE]
    groups = range(0, n_pages, PAGE_GROUP)

    def latent_keys(j0):
        return jnp.concatenate([ckv_buf[slot, j].astype(BF16) for j in range(j0, j0 + PAGE_GROUP)], axis=0)

    def chunk_scores(c):
        parts = []
        for j0 in groups[c * CHUNK_GROUPS:(c + 1) * CHUNK_GROUPS]:
            kr_t = jnp.concatenate([kr_buf[slot, j].astype(BF16) for j in range(j0, j0 + PAGE_GROUP)], axis=1)
            parts.append(_dot_nt(q_lat, latent_keys(j0)) + _dot(q_rope, kr_t))
        return jnp.concatenate(parts, axis=1)

    kn = jnp.concatenate([knew_ref[...], jnp.zeros((PAGE_SIZE - steps, QK_CAT), F32)], axis=0).astype(BF16)
    s_new = _dot_nt(q, kn)
    t_q = lax.broadcasted_iota(jnp.int32, s_new.shape, 0) % steps
    t_k = lax.broadcasted_iota(jnp.int32, s_new.shape, 1)
    s_new = jnp.where(t_k <= t_q, s_new, NEG_INF)

    gk = PAGE_GROUP * PAGE_SIZE
    n_chunks = len(groups) // CHUNK_GROUPS
    m = jnp.full((rows, 1), NEG_INF, F32)
    l = jnp.zeros((rows, 1), F32)
    acc = jnp.zeros((rows, KV_RANK), F32)
    s_cur = chunk_scores(0)
    for c in range(n_chunks + 1):
        s_next = chunk_scores(c + 1) if c + 1 < n_chunks else s_new
        m_new = jnp.maximum(m, jnp.max(s_cur, axis=1, keepdims=True))
        alpha = jnp.exp(m - m_new)
        p = jnp.exp(s_cur - m_new)
        l = alpha * l + jnp.sum(p, axis=1, keepdims=True)
        p = p.astype(BF16)
        if c < n_chunks:
            chunk_groups = groups[c * CHUNK_GROUPS:(c + 1) * CHUNK_GROUPS]
            pv = _dot(p[:, :gk], latent_keys(chunk_groups[0]))
            for i, j0 in enumerate(chunk_groups[1:], start=1):
                pv = pv + _dot(p[:, i * gk:(i + 1) * gk], latent_keys(j0))
        else:
            pv = _dot(p, kn[:, :KV_RANK])
        acc = alpha * acc + pv
        m = m_new
        s_cur = s_next
    o = (acc / l).astype(BF16)
    for h in range(MLA_HEADS):
        o_ref[:, h * V_HEAD:(h + 1) * V_HEAD] = _dot(o[h * steps:(h + 1) * steps], wuv_ref[h])


def _attn_sample(page_table, q, knew, cache_ckv, cache_kr_t, w_uv, steps):
    nb, n_pages = page_table.shape
    assert n_pages % (PAGE_GROUP * CHUNK_GROUPS) == 0
    grid_spec = pltpu.PrefetchScalarGridSpec(
        num_scalar_prefetch=1,
        grid=(nb,),
        in_specs=[pl.BlockSpec((MLA_HEADS, steps, QK_CAT), lambda b, pt: (0, b, 0)),
                  pl.BlockSpec((steps, QK_CAT), lambda b, pt: (b, 0)),
                  pl.BlockSpec(w_uv.shape, lambda b, pt: (0, 0, 0)),
                  pl.BlockSpec(memory_space=pl.ANY),
                  pl.BlockSpec(memory_space=pl.ANY)],
        out_specs=pl.BlockSpec((steps, MLA_HEADS * V_HEAD), lambda b, pt: (b, 0)),
        scratch_shapes=[pltpu.VMEM((2, n_pages, PAGE_SIZE, KV_RANK), F32),
                        pltpu.VMEM((2, n_pages, QK_ROPE, PAGE_SIZE), F32),
                        pltpu.SemaphoreType.DMA((2, 2))],
    )
    return pl.pallas_call(
        functools.partial(_attn_sample_kernel, n_pages=n_pages, steps=steps),
        grid_spec=grid_spec,
        out_shape=jax.ShapeDtypeStruct((nb * steps, MLA_HEADS * V_HEAD), F32),
        compiler_params=_params("arbitrary"),
        name="attn_sample",
    )(page_table, q, knew, w_uv, cache_ckv, cache_kr_t)


def _gelu_tanh(x):
    return 0.5 * x * (1.0 + jnp.tanh(math.sqrt(2.0 / math.pi) * (x + 0.044715 * (x * x * x))))


def _sigmoid(x):
    return 1.0 / (1.0 + jnp.exp(-x))


def _post_kernel(x_ref, yt_ref, u_ref, ymla_ref, p1_ref, p2_ref, d_ref, wglu_ref, wout_ref, gffn_ref,
                 wgate_ref, wup_ref, convw_ref, convb_ref, wdown_ref, gfin_ref,
                 y_ref, gate_ref, ybuf, hbuf, carry_ref, *, tm, seq_steps):
    long_seq = seq_steps == 0
    for o in range(S5_NOCT):
        for t in range(S5_CHUNK):
            ybuf[o, pl.ds(t, tm // S5_CHUNK, stride=S5_CHUNK), :] = yt_ref[o, t]
    y_ssm = jnp.concatenate([ybuf[o] for o in range(S5_NOCT)], axis=1)
    y = y_ssm + d_ref[...] * u_ref[...]
    y = _gelu_tanh(y)
    y = y * _sigmoid(_dot(y.astype(BF16), wglu_ref[...]))
    mixed = _dot(y.astype(BF16), wout_ref[:S5_W, :]) + _dot(ymla_ref[...].astype(BF16), wout_ref[S5_W:, :])
    x1 = x_ref[...] + mixed
    xn = _rms(x1, gffn_ref[...]).astype(BF16)

    row = lax.broadcasted_iota(jnp.int32, (tm, FF_CHUNK), 0)
    if long_seq:
        @pl.when(pl.program_id(1) == 0)
        def _():
            carry_ref[...] = p1_ref[...]
        t_in = row
    else:
        t_in = row % seq_steps

    for c in range(D_FF // FF_CHUNK):
        sl = slice(c * FF_CHUNK, (c + 1) * FF_CHUNK)
        gate = _dot(xn, wgate_ref[:, sl])
        up = _dot(xn, wup_ref[:, sl])
        if long_seq:
            back1 = jnp.broadcast_to(carry_ref[7:8, sl], gate.shape)
            back2 = jnp.where(t_in == 0, carry_ref[6:7, sl], back1)
            carry_ref[:, sl] = gate[tm - 8:, :]
            gate_ref[:, sl] = gate[tm - 8:, :]
        else:
            back1 = p1_ref[:, sl]
            back2 = p2_ref[:, sl]
            gate_ref[:, sl] = gate
        prev1 = jnp.where(t_in >= 1, pltpu.roll(gate, 1, 0), back1)
        prev2 = jnp.where(t_in >= 2, pltpu.roll(gate, 2, 0), back2)
        conv = (convb_ref[:, sl] + convw_ref[0:1, sl] * prev2 + convw_ref[1:2, sl] * prev1
                + convw_ref[2:3, sl] * gate)
        hbuf[:, sl] = (conv * _sigmoid(conv) * up).astype(BF16)

    x2 = x1 + _dot(hbuf[...], wdown_ref[...])
    y_ref[...] = _rms(x2, gfin_ref[...])


def _post_call(x2d, yt, u, y_mla, p1, p2, w, nb, tm, seq_steps):
    rows = x2d.shape[0]
    nt = rows // (nb * tm)
    assert nb * nt * tm == rows and tm % S5_CHUNK == 0
    long_seq = seq_steps == 0
    row = lambda width: pl.BlockSpec((tm, width), lambda b, i: (b * nt + i, 0))
    yt_spec = pl.BlockSpec((S5_NOCT, S5_CHUNK, tm // S5_CHUNK, LANE), lambda b, i: (0, 0, b * nt + i, 0))
    if long_seq:
        hist = [_const_spec((8, D_FF)), _const_spec((8, D_FF))]
        gate_spec = pl.BlockSpec((None, 8, D_FF), lambda b, i: (b, 0, 0))
        gate_shape = jax.ShapeDtypeStruct((nb, 8, D_FF), F32)
    else:
        hist = [row(D_FF), row(D_FF)]
        gate_spec = row(D_FF)
        gate_shape = jax.ShapeDtypeStruct((rows, D_FF), F32)
    return pl.pallas_call(
        functools.partial(_post_kernel, tm=tm, seq_steps=seq_steps),
        grid=(nb, nt),
        in_specs=[row(D_MODEL), yt_spec, row(S5_W), row(MLA_HEADS * V_HEAD)] + hist + [
            _const_spec((1, S5_W)), _const_spec((S5_W, S5_W)), _const_spec((D_MODEL, D_MODEL)),
            _const_spec((1, D_MODEL)), _const_spec((D_MODEL, D_FF)), _const_spec((D_MODEL, D_FF)),
            _const_spec((CONV_W, D_FF)), _const_spec((1, D_FF)), _const_spec((D_FF, D_MODEL)),
            _const_spec((1, D_MODEL))],
        out_specs=[row(D_MODEL), gate_spec],
        out_shape=[jax.ShapeDtypeStruct((rows, D_MODEL), F32), gate_shape],
        scratch_shapes=[pltpu.VMEM((S5_NOCT, tm, LANE), F32), pltpu.VMEM((tm, D_FF), BF16),
                        pltpu.VMEM((8, D_FF), F32)],
        compiler_params=_params("arbitrary", "arbitrary"),
        name="post",
    )(x2d, yt, u, y_mla, p1, p2, w["s5_d"], w["w_glu"], w["w_out"], w["g_ffn"], w["w_gate"], w["w_up"],
      w["conv_w"], w["conv_b"], w["w_down"], w["g_final"])


def _rope_tables(pos):
    half = QK_ROPE // 2
    inv = ROPE_BASE ** (-jnp.arange(half, dtype=F32) / half)
    ang = pos.astype(F32)[:, None] * inv[None, :]
    cos, sin = jnp.cos(ang), jnp.sin(ang)
    pad = jnp.zeros((pos.shape[0], ROPE_PAD - QK_ROPE), F32)
    return jnp.concatenate([cos, cos, pad], axis=1), jnp.concatenate([-sin, sin, pad], axis=1)


def kernel(x_prompt, x_sample, cache_ckv, cache_kr, state_s5_re, state_s5_im, state_conv, page_table, meta_tokens, g_mix, w_in, g_q, w_uq, g_kv, w_uk, w_uv, s5_a_re, s5_a_im, s5_log_dt, s5_b_re, s5_b_im, s5_c_re, s5_c_im, s5_d, w_glu, w_out, g_ffn, w_gate, w_up, conv_w, conv_b, w_down, g_final):
    assert w_in.shape[0] == 1, "single-layer step"
    nb, seq, _ = x_prompt.shape
    db, steps, _ = x_sample.shape
    n_past = page_table.shape[1] * PAGE_SIZE
    assert seq % S5_CHUNK == 0 and N_META % S5_CHUNK == 0 and steps == S5_CHUNK

    w_in_pad = jnp.concatenate([w_in[0], jnp.zeros((D_MODEL, IN_PAD - w_in.shape[2]), F32)], axis=1)
    wq = w_uq[0].reshape(Q_RANK, MLA_HEADS, QK_NOPE + QK_ROPE)
    wq_rope = jnp.concatenate([wq[:, :, QK_NOPE:], jnp.zeros((Q_RANK, MLA_HEADS, ROPE_PAD - QK_ROPE), F32)], axis=2)
    wq_perm = jnp.concatenate([wq[:, :, :QK_NOPE].reshape(Q_RANK, -1), wq_rope.reshape(Q_RANK, -1)], axis=1)
    w = {
        "g_mix": g_mix[0].reshape(1, -1), "w_in": w_in_pad.astype(BF16),
        "g_q": g_q[0].reshape(1, -1), "w_uq": wq_perm.astype(BF16), "w_uq_t": wq_perm.T.astype(BF16),
        "w_uk_t": w_uk[0].transpose(1, 2, 0).astype(BF16),
        "w_uk": w_uk[0].transpose(1, 0, 2).astype(BF16),
        "g_kv": g_kv[0].reshape(1, -1),
        "s5_d": s5_d[0].reshape(1, -1), "w_glu": w_glu[0].astype(BF16), "w_out": w_out[0].astype(BF16),
        "g_ffn": g_ffn[0].reshape(1, -1), "w_gate": w_gate[0].astype(BF16), "w_up": w_up[0].astype(BF16),
        "conv_w": conv_w[0], "conv_b": conv_b[0].reshape(1, -1), "w_down": w_down[0].astype(BF16),
        "g_final": g_final.reshape(1, -1),
    }
    w_uv_h = w_uv[0].transpose(1, 0, 2).astype(BF16)
    ops = _s5_operators(s5_a_re[0], s5_a_im[0], s5_log_dt[0], s5_b_re[0], s5_b_im[0], s5_c_re[0], s5_c_im[0])

    cos_m, sin_m = _rope_tables(jnp.arange(N_META, dtype=jnp.int32))
    cos_p, sin_p = _rope_tables(N_META + jnp.arange(seq, dtype=jnp.int32))
    tm_p = 512
    tm_s = 512
    cos_s, sin_s = _rope_tables(n_past + jnp.arange(tm_s, dtype=jnp.int32) % steps)

    tiles_per_seq = seq // tm_p
    u_m, ut_m, q_m, kcat_m, ckv_m, kr_m = _pre_call(meta_tokens, cos_m, sin_m, lambda i: (0, 0), N_META, BF16, w)
    u_p, ut_p, qt_p, kcat_p, ckv_p, kr_p, vt_p = _pre_call(
        x_prompt.reshape(nb * seq, D_MODEL), cos_p, sin_p, lambda i: (i % tiles_per_seq, 0), tm_p, BF16, w,
        q_transposed=True)
    u_s, ut_s, q_s, kcat_s, ckv_s, kr_s = _pre_call(x_sample.reshape(db * steps, D_MODEL), cos_s, sin_s,
                                                    lambda i: (0, 0), tm_s, F32, w)

    zero_state = jnp.zeros((1, 1, S5_STATE_W), F32)
    yt_m, fm_re, fm_im = _s5_mixer(ut_m, zero_state, zero_state, ops, 1)
    yt_p, s5_re_p, s5_im_p = _s5_mixer(ut_p, jnp.broadcast_to(fm_re, (nb, 1, S5_STATE_W)),
                                       jnp.broadcast_to(fm_im, (nb, 1, S5_STATE_W)), ops, nb)
    yt_s, s5_re_s, s5_im_s = _s5_mixer(ut_s, state_s5_re[0].reshape(db, 1, S5_STATE_W),
                                       state_s5_im[0].reshape(db, 1, S5_STATE_W), ops, db)

    kmeta = jnp.concatenate([kcat_m, jnp.zeros((PAGE_SIZE - N_META, QK_CAT), BF16)], axis=0)
    ymla_m = _attn_meta(q_m, kmeta, w_uv_h)
    ymla_p = _attn_prompt(qt_p, kcat_p.reshape(nb, seq, QK_CAT), vt_p, kcat_m, kcat_m[:, :KV_RANK].T,
                          w_uv[0].transpose(1, 2, 0).astype(BF16), nb, tq=256)
    ymla_s = _attn_sample(page_table, q_s, kcat_s, cache_ckv, jnp.swapaxes(cache_kr, 2, 3), w_uv_h, steps)

    zeros_hist = jnp.zeros((8, D_FF), F32)
    _, gate_m = _post_call(meta_tokens, yt_m, u_m, ymla_m, zeros_hist, zeros_hist, w, 1, N_META, 0)
    y_p, gate_p = _post_call(x_prompt.reshape(nb * seq, D_MODEL), yt_p, u_p, ymla_p, gate_m[0], gate_m[0],
                             w, nb, tm_p, 0)
    sc = state_conv[0]
    back1 = jnp.pad(sc[:, 1:2], ((0, 0), (0, steps - 1), (0, 0))).reshape(db * steps, D_FF)
    back2 = jnp.pad(sc, ((0, 0), (0, steps - 2), (0, 0))).reshape(db * steps, D_FF)
    y_s, gate_s = _post_call(x_sample.reshape(db * steps, D_MODEL), yt_s, u_s, ymla_s, back1, back2,
                             w, 1, tm_s // 2, steps)

    def with_meta(meta_rows, tok_rows, width):
        return jnp.concatenate([jnp.broadcast_to(meta_rows[None], (nb, N_META, width)),
                                tok_rows.reshape(nb, seq, width)], axis=1)[None]

    return (y_p.reshape(nb, seq, D_MODEL), y_s.reshape(db, steps, D_MODEL),
            with_meta(ckv_m, ckv_p, KV_RANK), with_meta(kr_m, kr_p, QK_ROPE),
            s5_re_p.reshape(1, nb, S5_GROUPS, S5_STATE), s5_im_p.reshape(1, nb, S5_GROUPS, S5_STATE),
            gate_p[:, 8 - (CONV_W - 1):][None],
            ckv_s.reshape(1, db, steps, KV_RANK), kr_s.reshape(1, db, steps, QK_ROPE),
            s5_re_s.reshape(1, db, S5_GROUPS, S5_STATE), s5_im_s.reshape(1, db, S5_GROUPS, S5_STATE),
            gate_s.reshape(db, steps, D_FF)[:, steps - (CONV_W - 1):][None])
```

```python
import functools
import math

import jax
import jax.numpy as jnp
from jax import lax
from jax.experimental import pallas as pl
from jax.experimental.pallas import tpu as pltpu

F32 = jnp.float32
BF16 = jnp.bfloat16

D_MODEL = 1024
N_META = 16
S5_W = 512
S5_GROUP = 16
S5_GROUPS = 32
S5_STATE = 64
MLA_HEADS = 4
QK_NOPE = 128
QK_ROPE = 64
V_HEAD = 128
Q_RANK = 384
KV_RANK = 256
D_FF = 2816
CONV_W = 3
ROPE_BASE = 10000.0
EPS = 1e-6
PAGE_SIZE = 128
ATTN_SCALE = 1.0 / math.sqrt(QK_NOPE + QK_ROPE)
LOG2_E = math.log2(math.e)

LANE = 128
ROPE_PAD = LANE
QK_CAT = KV_RANK + ROPE_PAD
IN_PAD = S5_W + Q_RANK + KV_RANK + ROPE_PAD
S5_CHUNK = 8
S5_OCT = LANE // S5_GROUP
S5_NOCT = S5_GROUPS // S5_OCT
S5_STATE_W = S5_GROUPS * S5_STATE
S5_SCAN_W = 512
S5_POW_PAD = 16
FF_CHUNK = 256
PAGE_GROUP = 4
CHUNK_GROUPS = 4
VMEM_LIMIT = 56 * 1024 * 1024
NEG_INF = float("-inf")
HI = lax.Precision.HIGHEST


def _const_spec(shape):
    nd = len(shape)
    return pl.BlockSpec(shape, lambda *_: (0,) * nd, pipeline_mode=pl.Buffered(1))


def _params(*semantics):
    return pltpu.CompilerParams(dimension_semantics=semantics, vmem_limit_bytes=VMEM_LIMIT)


def _rms(x, g):
    return x * lax.rsqrt(jnp.mean(x * x, axis=-1, keepdims=True) + EPS) * g


def _rope_slab(x, cos, sin):
    lane = lax.broadcasted_iota(jnp.int32, x.shape, 1)
    half = QK_ROPE // 2
    swapped = jnp.where(lane < half, pltpu.roll(x, LANE - half, 1), pltpu.roll(x, half, 1))
    return x * cos + swapped * sin


def _dot(a, b):
    return jnp.dot(a, b, preferred_element_type=F32)


def _dot_nt(a, b):
    return lax.dot_general(a, b, (((1,), (1,)), ((), ())), preferred_element_type=F32)


def _pre_kernel(x_ref, cos_ref, sin_ref, cost_ref, sint_ref, gmix_ref, win_ref, gq_ref, wuq_ref, wuk_ref, gkv_ref,
                u_ref, ut_ref, q_ref, kcat_ref, ckv_ref, kr_ref, *rest, q_transposed):
    x = x_ref[...]
    xn = _rms(x, gmix_ref[...]).astype(BF16)
    z = _dot(xn, win_ref[...])
    u_ref[...] = z[:, :S5_W]
    slab_ref = rest[-1]
    chunks = u_ref.shape[0] // S5_CHUNK
    for o in range(S5_NOCT):
        slab_ref[o] = z[:, o * LANE:(o + 1) * LANE]
        for s in range(S5_CHUNK):
            ut_ref[o, s] = slab_ref[o, pl.ds(s, chunks, stride=S5_CHUNK), :].astype(BF16)
    cq = z[:, S5_W:S5_W + Q_RANK]
    ckv_raw = z[:, S5_W + Q_RANK:S5_W + Q_RANK + KV_RANK]
    kr_raw = z[:, S5_W + Q_RANK + KV_RANK:]
    cqn = _rms(cq, gq_ref[...]).astype(BF16)
    nope_w = MLA_HEADS * QK_NOPE
    half = QK_ROPE // 2
    if q_transposed:
        qt = _dot_nt(wuq_ref[...], cqn)
        cost = cost_ref[...]
        sint = sint_ref[...]
        for h in range(MLA_HEADS):
            qn = qt[h * QK_NOPE:(h + 1) * QK_NOPE].astype(BF16)
            q_lat = _dot(wuk_ref[h], qn) * (ATTN_SCALE * LOG2_E)
            xr = qt[nope_w + h * ROPE_PAD:nope_w + (h + 1) * ROPE_PAD]
            swapped = jnp.concatenate([xr[half:QK_ROPE], xr[:half], xr[QK_ROPE:]], axis=0)
            qr = (xr * cost + swapped * sint) * (ATTN_SCALE * LOG2_E)
            q_ref[h, :KV_RANK, :] = q_lat.astype(q_ref.dtype)
            q_ref[h, KV_RANK:, :] = qr.astype(q_ref.dtype)
    else:
        q = _dot(cqn, wuq_ref[...])
        for h in range(MLA_HEADS):
            qn = q[:, h * QK_NOPE:(h + 1) * QK_NOPE].astype(BF16)
            q_lat = _dot(qn, wuk_ref[h]) * ATTN_SCALE
            qr = _rope_slab(q[:, nope_w + h * ROPE_PAD:nope_w + (h + 1) * ROPE_PAD],
                            cos_ref[...], sin_ref[...]) * ATTN_SCALE
            q_ref[h, :, :KV_RANK] = q_lat.astype(q_ref.dtype)
            q_ref[h, :, KV_RANK:] = qr.astype(q_ref.dtype)
    ckv = _rms(ckv_raw, gkv_ref[...])
    kr = _rope_slab(kr_raw, cos_ref[...], sin_ref[...])
    ckv_ref[...] = ckv
    kr_ref[...] = kr[:, :QK_ROPE]
    kcat_ref[:, :KV_RANK] = ckv.astype(kcat_ref.dtype)
    kcat_ref[:, KV_RANK:] = kr.astype(kcat_ref.dtype)
    if q_transposed:
        rest[0][...] = ckv.T.astype(BF16)


def _pre_call(x2d, cos, sin, tab_map, tm, qdtype, w, q_transposed=False):
    rows = x2d.shape[0]
    assert rows % tm == 0 and tm % S5_CHUNK == 0
    row = lambda width: pl.BlockSpec((tm, width), lambda i: (i, 0))
    tab_map_t = lambda i: tab_map(i)[::-1]
    if q_transposed:
        wuq, wuk = w["w_uq_t"], w["w_uk"]
        q_spec = pl.BlockSpec((MLA_HEADS, QK_CAT, tm), lambda i: (0, 0, i))
        q_shape = jax.ShapeDtypeStruct((MLA_HEADS, QK_CAT, rows), qdtype)
        extra_specs = [pl.BlockSpec((KV_RANK, tm), lambda i: (0, i))]
        extra_shapes = [jax.ShapeDtypeStruct((KV_RANK, rows), BF16)]
    else:
        wuq, wuk = w["w_uq"], w["w_uk_t"]
        q_spec = pl.BlockSpec((MLA_HEADS, tm, QK_CAT), lambda i: (0, i, 0))
        q_shape = jax.ShapeDtypeStruct((MLA_HEADS, rows, QK_CAT), qdtype)
        extra_specs, extra_shapes = [], []
    return pl.pallas_call(
        functools.partial(_pre_kernel, q_transposed=q_transposed),
        grid=(rows // tm,),
        in_specs=[
            row(D_MODEL),
            pl.BlockSpec((tm, ROPE_PAD), tab_map),
            pl.BlockSpec((tm, ROPE_PAD), tab_map),
            pl.BlockSpec((ROPE_PAD, tm), tab_map_t),
            pl.BlockSpec((ROPE_PAD, tm), tab_map_t),
            _const_spec((1, D_MODEL)),
            _const_spec((D_MODEL, IN_PAD)),
            _const_spec((1, Q_RANK)),
            _const_spec(wuq.shape),
            _const_spec(wuk.shape),
            _const_spec((1, KV_RANK)),
        ],
        out_specs=[row(S5_W), pl.BlockSpec((S5_NOCT, S5_CHUNK, tm // S5_CHUNK, LANE), lambda i: (0, 0, i, 0)),
                   q_spec, row(QK_CAT), row(KV_RANK), row(QK_ROPE)] + extra_specs,
        out_shape=[
            jax.ShapeDtypeStruct((rows, S5_W), F32),
            jax.ShapeDtypeStruct((S5_NOCT, S5_CHUNK, rows // S5_CHUNK, LANE), BF16),
            q_shape,
            jax.ShapeDtypeStruct((rows, QK_CAT), qdtype),
            jax.ShapeDtypeStruct((rows, KV_RANK), F32),
            jax.ShapeDtypeStruct((rows, QK_ROPE), F32),
        ] + extra_shapes,
        scratch_shapes=[pltpu.VMEM((S5_NOCT, tm, LANE), F32)],
        compiler_params=_params("arbitrary"),
        name="pre",
    )(x2d, cos, sin, cos.T, sin.T, w["g_mix"], w["w_in"], w["g_q"], wuq, wuk, w["g_kv"])


def _chunk_inputs(ut_ref):
    return jnp.concatenate([ut_ref[s] for s in range(S5_CHUNK)], axis=1)


def _s5_state_in_kernel(ut_ref, wre_ref, wim_ref, sre_ref, sim_ref):
    u = _chunk_inputs(ut_ref)
    sre_ref[...] = _dot(u, wre_ref[0])
    sim_ref[...] = _dot(u, wim_ref[0])


def _s5_scan_kernel(sre_ref, sim_ref, h0re_ref, h0im_ref, are_ref, aim_ref, hre_ref, him_ref, fre_ref, fim_ref):
    nb, nc, width = sre_ref.shape
    ar = are_ref[...].reshape(1, 1, width)
    ai = aim_ref[...].reshape(1, 1, width)

    def body(c, carry):
        hr, hi = carry
        hre_ref[:, pl.ds(c, 1), :] = hr
        him_ref[:, pl.ds(c, 1), :] = hi
        sr = sre_ref[:, pl.ds(c, 1), :]
        si = sim_ref[:, pl.ds(c, 1), :]
        return ar * hr - ai * hi + sr, ar * hi + ai * hr + si

    hr, hi = lax.fori_loop(0, nc, body, (h0re_ref[...], h0im_ref[...]))
    fre_ref[...] = hr
    fim_ref[...] = hi


def _s5_out_kernel(ut_ref, hre_ref, him_ref, t_ref, mre_ref, mim_ref, yt_ref):
    y = (_dot(_chunk_inputs(ut_ref), t_ref[0]) + _dot(hre_ref[...].astype(BF16), mre_ref[0])
         + _dot(him_ref[...].astype(BF16), mim_ref[0]))
    for t in range(S5_CHUNK):
        yt_ref[t] = y[:, t * LANE:(t + 1) * LANE]


def _s5_mixer(ut, h0_re, h0_im, ops, nb):
    rows = ut.shape[2]
    nc = rows // nb
    tr = min(rows, 1024)
    assert rows % tr == 0
    ow = S5_OCT * S5_STATE
    cw = S5_CHUNK * LANE
    ut_spec = pl.BlockSpec((None, S5_CHUNK, tr, LANE), lambda q, r: (q, 0, r, 0))
    st_spec = pl.BlockSpec((tr, ow), lambda q, r: (r, q))
    oct_spec = lambda a, b: pl.BlockSpec((1, a, b), lambda q, r: (q, 0, 0))
    s_re, s_im = pl.pallas_call(
        _s5_state_in_kernel,
        grid=(S5_NOCT, rows // tr),
        in_specs=[ut_spec, oct_spec(cw, ow), oct_spec(cw, ow)],
        out_specs=[st_spec, st_spec],
        out_shape=[jax.ShapeDtypeStruct((rows, S5_STATE_W), F32)] * 2,
        compiler_params=_params("arbitrary", "arbitrary"),
        name="s5_state_in",
    )(ut, ops["w_re"], ops["w_im"])

    blk = pl.BlockSpec((nb, nc, S5_SCAN_W), lambda j: (0, 0, j))
    one = pl.BlockSpec((nb, 1, S5_SCAN_W), lambda j: (0, 0, j))
    vec = pl.BlockSpec((1, S5_SCAN_W), lambda j: (0, j))
    h_re, h_im, f_re, f_im = pl.pallas_call(
        _s5_scan_kernel,
        grid=(S5_STATE_W // S5_SCAN_W,),
        in_specs=[blk, blk, one, one, vec, vec],
        out_specs=[blk, blk, one, one],
        out_shape=[jax.ShapeDtypeStruct((nb, nc, S5_STATE_W), F32)] * 2
                  + [jax.ShapeDtypeStruct((nb, 1, S5_STATE_W), F32)] * 2,
        compiler_params=_params("arbitrary"),
        name="s5_scan",
    )(s_re.reshape(nb, nc, S5_STATE_W), s_im.reshape(nb, nc, S5_STATE_W), h0_re, h0_im, ops["a_re"], ops["a_im"])

    yt = pl.pallas_call(
        _s5_out_kernel,
        grid=(S5_NOCT, rows // tr),
        in_specs=[ut_spec, st_spec, st_spec, oct_spec(cw, cw), oct_spec(ow, cw), oct_spec(ow, cw)],
        out_specs=ut_spec,
        out_shape=jax.ShapeDtypeStruct(ut.shape, F32),
        compiler_params=_params("arbitrary", "arbitrary"),
        name="s5_out",
    )(ut, h_re.reshape(rows, S5_STATE_W), h_im.reshape(rows, S5_STATE_W), ops["t"], ops["m_re"], ops["m_im"])
    return yt, f_re, f_im


def _s5_operator_kernel(btr_ref, bti_ref, cr_ref, ci_ref, pl_re, pl_im, pr_re, pr_im,
                        t_ref, wre_ref, wim_ref, mre_ref, mim_ref):
    btr, bti = btr_ref[0], bti_ref[0]
    cr, ci = cr_ref[0], ci_ref[0]
    hi_dot = lambda a, b: jnp.dot(a, b, precision=HI, preferred_element_type=F32)
    kerns = []
    for k in range(S5_CHUNK + 1):
        ar, ai = pr_re[0, :, k:k + 1], pr_im[0, :, k:k + 1]
        cpr = cr * ar - ci * ai
        cpi = cr * ai + ci * ar
        if k < S5_CHUNK:
            kerns.append((hi_dot(btr, cpr) - hi_dot(bti, cpi)).astype(BF16))
            s = S5_CHUNK - 1 - k
            lr, li = pl_re[0, k:k + 1, :], pl_im[0, k:k + 1, :]
            wre_ref[0, s * LANE:(s + 1) * LANE, :] = (btr * lr - bti * li).astype(BF16)
            wim_ref[0, s * LANE:(s + 1) * LANE, :] = (btr * li + bti * lr).astype(BF16)
        if k >= 1:
            t = k - 1
            mre_ref[0, :, t * LANE:(t + 1) * LANE] = cpr.astype(BF16)
            mim_ref[0, :, t * LANE:(t + 1) * LANE] = (-cpi).astype(BF16)
    zero = jnp.zeros((LANE, LANE), BF16)
    for s in range(S5_CHUNK):
        for t in range(S5_CHUNK):
            t_ref[0, s * LANE:(s + 1) * LANE, t * LANE:(t + 1) * LANE] = kerns[t - s] if t >= s else zero


def _s5_operators(a_re, a_im, log_dt, b_re, b_im, c_re, c_im):
    steps = S5_CHUNK
    dt = jnp.exp(log_dt)[:, None]
    k = jnp.arange(steps + 1, dtype=F32)[:, None, None]
    mag = jnp.exp(k * (dt * a_re)[None])
    pw_r = mag * jnp.cos(k * (dt * a_im)[None])
    pw_i = mag * jnp.sin(k * (dt * a_im)[None])
    abr, abi = pw_r[1], pw_i[1]
    num_re, num_im = abr - 1.0, abi
    den = a_re * a_re + a_im * a_im
    f_re = (num_re * a_re + num_im * a_im) / den
    f_im = (num_im * a_re - num_re * a_im) / den
    bbr = f_re[..., None] * b_re - f_im[..., None] * b_im
    bbi = f_re[..., None] * b_im + f_im[..., None] * b_re
    same_group = jnp.arange(S5_OCT)[:, None, None, None] == jnp.arange(S5_OCT)[None, None, :, None]
    ow = S5_OCT * S5_STATE

    def block_diag(a):
        a = a.reshape(S5_NOCT, S5_OCT, a.shape[1], 1, a.shape[2])
        return jnp.where(same_group, a, 0.0).reshape(S5_NOCT, S5_OCT * a.shape[2], S5_OCT * a.shape[4])

    pad_k = S5_POW_PAD - (steps + 1)
    lanes = lambda p: jnp.pad(p.reshape(steps + 1, S5_NOCT, ow).transpose(1, 0, 2), ((0, 0), (0, pad_k), (0, 0)))
    rows = lambda p: jnp.pad(p.reshape(steps + 1, S5_NOCT, ow).transpose(1, 2, 0), ((0, 0), (0, 0), (0, pad_k)))
    bt_re, bt_im = block_diag(bbr.transpose(0, 2, 1)), block_diag(bbi.transpose(0, 2, 1))
    cb_re, cb_im = block_diag(c_re.transpose(0, 2, 1)), block_diag(c_im.transpose(0, 2, 1))
    cw = steps * LANE
    oct_spec = lambda a, b: pl.BlockSpec((1, a, b), lambda q: (q, 0, 0))
    t_mat, w_re, w_im, m_re, m_im = pl.pallas_call(
        _s5_operator_kernel,
        grid=(S5_NOCT,),
        in_specs=[oct_spec(LANE, ow), oct_spec(LANE, ow), oct_spec(ow, LANE), oct_spec(ow, LANE),
                  oct_spec(S5_POW_PAD, ow), oct_spec(S5_POW_PAD, ow), oct_spec(ow, S5_POW_PAD),
                  oct_spec(ow, S5_POW_PAD)],
        out_specs=[oct_spec(cw, cw), oct_spec(cw, ow), oct_spec(cw, ow), oct_spec(ow, cw), oct_spec(ow, cw)],
        out_shape=[jax.ShapeDtypeStruct((S5_NOCT, cw, cw), BF16)]
                  + [jax.ShapeDtypeStruct((S5_NOCT, cw, ow), BF16)] * 2
                  + [jax.ShapeDtypeStruct((S5_NOCT, ow, cw), BF16)] * 2,
        compiler_params=_params("arbitrary"),
        name="s5_operators",
    )(bt_re, bt_im, cb_re, cb_im, lanes(pw_r), lanes(pw_i), rows(pw_r), rows(pw_i))
    return {
        "t": t_mat, "w_re": w_re, "w_im": w_im, "m_re": m_re, "m_im": m_im,
        "a_re": pw_r[steps].reshape(1, S5_STATE_W), "a_im": pw_i[steps].reshape(1, S5_STATE_W),
    }


def _softmax_update_t(s, vt, m_ref, l_ref, acc_ref):
    m_prev = m_ref[...]
    m_new = jnp.maximum(m_prev, jnp.max(s, axis=0, keepdims=True))
    alpha = jnp.exp2(m_prev - m_new)
    p = jnp.exp2(s - m_new)
    l_ref[...] = alpha * l_ref[...] + jnp.sum(p, axis=0, keepdims=True)
    acc_ref[...] = alpha * acc_ref[...] + _dot(vt, p.astype(BF16))
    m_ref[...] = m_new


def _attn_prompt_kernel(qt_ref, k_ref, vt_ref, kmeta_ref, vmeta_t_ref, wuvt_ref, o_ref, m_ref, l_ref, acc_ref, s_ref,
                        *, tq):
    qi = pl.program_id(1)
    qt = jnp.concatenate([qt_ref[h] for h in range(MLA_HEADS)], axis=1)

    def keys(j):
        return pl.ds(pl.multiple_of(j * tq, tq), tq)

    def scores(j):
        return _dot(k_ref[0, keys(j), :], qt)

    def update(s, j):
        _softmax_update_t(s, vt_ref[:, keys(j)], m_ref, l_ref, acc_ref)

    s = _dot(kmeta_ref[...], qt)
    m0 = jnp.max(s, axis=0, keepdims=True)
    p = jnp.exp2(s - m0)
    m_ref[...] = m0
    l_ref[...] = jnp.sum(p, axis=0, keepdims=True)
    acc_ref[...] = _dot(vmeta_t_ref[...], p.astype(BF16))

    s_ref[...] = scores(0)

    def pair_body(kp, carry):
        j = 2 * kp
        s_b = scores(j + 1)
        s_next = scores(j + 2)
        update(s_ref[...], j)
        update(s_b, j + 1)
        s_ref[...] = s_next
        return carry

    lax.fori_loop(0, qi // 2, pair_body, 0)

    @pl.when(qi % 2 == 1)
    def _():
        s_next = scores(qi)
        update(s_ref[...], qi - 1)
        s_ref[...] = s_next

    s = s_ref[...]
    t_k = lax.broadcasted_iota(jnp.int32, s.shape, 0)
    t_q = lax.broadcasted_iota(jnp.int32, s.shape, 1) % tq
    s = jnp.where(t_k <= t_q, s, NEG_INF)
    update(s, qi)

    ot = (acc_ref[...] / l_ref[...]).astype(BF16)
    for h in range(MLA_HEADS):
        yt = _dot(wuvt_ref[h], ot[:, h * tq:(h + 1) * tq])
        o_ref[:, h * V_HEAD:(h + 1) * V_HEAD] = yt.T.astype(o_ref.dtype)


def _attn_prompt(qt, kcat, vt, kmeta, vmeta_t, w_uv_t, nb, tq):
    t = kcat.shape[1]
    nq = t // tq
    cols = MLA_HEADS * tq
    return pl.pallas_call(
        functools.partial(_attn_prompt_kernel, tq=tq),
        grid=(nb, nq),
        in_specs=[
            pl.BlockSpec((MLA_HEADS, QK_CAT, tq), lambda b, i: (0, 0, b * nq + i)),
            pl.BlockSpec((1, t, QK_CAT), lambda b, i: (b, 0, 0)),
            pl.BlockSpec((KV_RANK, t), lambda b, i: (0, b)),
            _const_spec(kmeta.shape),
            _const_spec(vmeta_t.shape),
            _const_spec(w_uv_t.shape),
        ],
        out_specs=pl.BlockSpec((tq, MLA_HEADS * V_HEAD), lambda b, i: (b * nq + i, 0)),
        out_shape=jax.ShapeDtypeStruct((nb * t, MLA_HEADS * V_HEAD), BF16),
        scratch_shapes=[pltpu.VMEM((1, cols), F32), pltpu.VMEM((1, cols), F32), pltpu.VMEM((KV_RANK, cols), F32),
                        pltpu.VMEM((tq, cols), F32)],
        compiler_params=_params("arbitrary", "arbitrary"),
        name="attn_prompt",
    )(qt, kcat, vt, kmeta, vmeta_t, w_uv_t)


def _attn_meta_kernel(q_ref, kmeta_ref, wuv_ref, o_ref):
    rows = MLA_HEADS * N_META
    q = q_ref[...].reshape(rows, QK_CAT)
    km = kmeta_ref[...]
    s = _dot_nt(q, km)
    t_q = lax.broadcasted_iota(jnp.int32, s.shape, 0) % N_META
    t_k = lax.broadcasted_iota(jnp.int32, s.shape, 1)
    s = jnp.where(t_k <= t_q, s, NEG_INF)
    p = jnp.exp(s - jnp.max(s, axis=1, keepdims=True))
    o = _dot(p.astype(BF16), km[:, :KV_RANK]) / jnp.sum(p, axis=1, keepdims=True)
    o = o.astype(BF16)
    for h in range(MLA_HEADS):
        o_ref[:, h * V_HEAD:(h + 1) * V_HEAD] = _dot(o[h * N_META:(h + 1) * N_META], wuv_ref[h]).astype(o_ref.dtype)


def _attn_meta(q, kmeta, w_uv):
    return pl.pallas_call(
        _attn_meta_kernel,
        out_shape=jax.ShapeDtypeStruct((N_META, MLA_HEADS * V_HEAD), BF16),
        name="attn_meta",
    )(q, kmeta, w_uv)


def _attn_sample_kernel(pt_ref, q_ref, knew_ref, wuv_ref, ckv_hbm, kr_hbm, o_ref, ckv_buf, kr_buf, sem,
                        *, n_pages, steps):
    b = pl.program_id(0)
    slot = b % 2

    def page_copies(seq, into, j):
        page = pt_ref[seq, j]
        return (pltpu.make_async_copy(ckv_hbm.at[0, page], ckv_buf.at[into, j], sem.at[into, 0]),
                pltpu.make_async_copy(kr_hbm.at[0, page], kr_buf.at[into, j], sem.at[into, 1]))

    def for_each_page(seq, into, act):
        def body(j, carry):
            for copy in page_copies(seq, into, j):
                act(copy)
            return carry
        lax.fori_loop(0, n_pages, body, 0, unroll=4)

    @pl.when(b == 0)
    def _():
        for_each_page(0, 0, lambda copy: copy.start())

    @pl.when(b + 1 < pl.num_programs(0))
    def _():
        for_each_page(b + 1, 1 - slot, lambda copy: copy.start())

    for_each_page(b, slot, lambda copy: copy.wait())

    rows = MLA_HEADS * steps
    q = q_ref[...].reshape(rows, QK_CAT).astype(BF16)
    q_lat = q[:, :KV_RANK]
    q_rope = q[:, KV_RANK:KV_RANK + QK_ROPE]
    groups = range(0, n_pages, PAGE_GROUP)

    def latent_keys(j0):
        return jnp.concatenate([ckv_buf[slot, j].astype(BF16) for j in range(j0, j0 + PAGE_GROUP)], axis=0)

    def chunk_scores(c):
        parts = []
        for j0 in groups[c * CHUNK_GROUPS:(c + 1) * CHUNK_GROUPS]:
            kr_t = jnp.concatenate([kr_buf[slot, j].astype(BF16) for j in range(j0, j0 + PAGE_GROUP)], axis=1)
            parts.append(_dot_nt(q_lat, latent_keys(j0)) + _dot(q_rope, kr_t))
        return jnp.concatenate(parts, axis=1)

    kn = jnp.concatenate([knew_ref[...], jnp.zeros((PAGE_SIZE - steps, QK_CAT), F32)], axis=0).astype(BF16)
    s_new = _dot_nt(q, kn)
    t_q = lax.broadcasted_iota(jnp.int32, s_new.shape, 0) % steps
    t_k = lax.broadcasted_iota(jnp.int32, s_new.shape, 1)
    s_new = jnp.where(t_k <= t_q, s_new, NEG_INF)

    gk = PAGE_GROUP * PAGE_SIZE
    n_chunks = len(groups) // CHUNK_GROUPS
    m = jnp.full((rows, 1), NEG_INF, F32)
    l = jnp.zeros((rows, 1), F32)
    acc = jnp.zeros((rows, KV_RANK), F32)
    s_cur = chunk_scores(0)
    for c in range(n_chunks + 1):
        s_next = chunk_scores(c + 1) if c + 1 < n_chunks else s_new
        m_new = jnp.maximum(m, jnp.max(s_cur, axis=1, keepdims=True))
        alpha = jnp.exp(m - m_new)
        p = jnp.exp(s_cur - m_new)
        l = alpha * l + jnp.sum(p, axis=1, keepdims=True)
        p = p.astype(BF16)
        if c < n_chunks:
            chunk_groups = groups[c * CHUNK_GROUPS:(c + 1) * CHUNK_GROUPS]
            pv = _dot(p[:, :gk], latent_keys(chunk_groups[0]))
            for i, j0 in enumerate(chunk_groups[1:], start=1):
                pv = pv + _dot(p[:, i * gk:(i + 1) * gk], latent_keys(j0))
        else:
            pv = _dot(p, kn[:, :KV_RANK])
        acc = alpha * acc + pv
        m = m_new
        s_cur = s_next
    o = (acc / l).astype(BF16)
    for h in range(MLA_HEADS):
        o_ref[:, h * V_HEAD:(h + 1) * V_HEAD] = _dot(o[h * steps:(h + 1) * steps], wuv_ref[h])


def _attn_sample(page_table, q, knew, cache_ckv, cache_kr_t, w_uv, steps):
    nb, n_pages = page_table.shape
    assert n_pages % (PAGE_GROUP * CHUNK_GROUPS) == 0
    grid_spec = pltpu.PrefetchScalarGridSpec(
        num_scalar_prefetch=1,
        grid=(nb,),
        in_specs=[pl.BlockSpec((MLA_HEADS, steps, QK_CAT), lambda b, pt: (0, b, 0)),
                  pl.BlockSpec((steps, QK_CAT), lambda b, pt: (b, 0)),
                  pl.BlockSpec(w_uv.shape, lambda b, pt: (0, 0, 0)),
                  pl.BlockSpec(memory_space=pl.ANY),
                  pl.BlockSpec(memory_space=pl.ANY)],
        out_specs=pl.BlockSpec((steps, MLA_HEADS * V_HEAD), lambda b, pt: (b, 0)),
        scratch_shapes=[pltpu.VMEM((2, n_pages, PAGE_SIZE, KV_RANK), F32),
                        pltpu.VMEM((2, n_pages, QK_ROPE, PAGE_SIZE), F32),
                        pltpu.SemaphoreType.DMA((2, 2))],
    )
    return pl.pallas_call(
        functools.partial(_attn_sample_kernel, n_pages=n_pages, steps=steps),
        grid_spec=grid_spec,
        out_shape=jax.ShapeDtypeStruct((nb * steps, MLA_HEADS * V_HEAD), F32),
        compiler_params=_params("arbitrary"),
        name="attn_sample",
    )(page_table, q, knew, w_uv, cache_ckv, cache_kr_t)


def _gelu_tanh(x):
    return 0.5 * x * (1.0 + jnp.tanh(math.sqrt(2.0 / math.pi) * (x + 0.044715 * (x * x * x))))


def _sigmoid(x):
    return 1.0 / (1.0 + jnp.exp(-x))


def _post_kernel(x_ref, yt_ref, u_ref, ymla_ref, p1_ref, p2_ref, d_ref, wglu_ref, wout_ref, gffn_ref,
                 wgate_ref, wup_ref, convw_ref, convb_ref, wdown_ref, gfin_ref,
                 y_ref, gate_ref, ybuf, hbuf, carry_ref, *, tm, seq_steps):
    long_seq = seq_steps == 0
    for o in range(S5_NOCT):
        for t in range(S5_CHUNK):
            ybuf[o, pl.ds(t, tm // S5_CHUNK, stride=S5_CHUNK), :] = yt_ref[o, t]
    y_ssm = jnp.concatenate([ybuf[o] for o in range(S5_NOCT)], axis=1)
    y = y_ssm + d_ref[...] * u_ref[...]
    y = _gelu_tanh(y)
    y = y * _sigmoid(_dot(y.astype(BF16), wglu_ref[...]))
    mixed = _dot(y.astype(BF16), wout_ref[:S5_W, :]) + _dot(ymla_ref[...].astype(BF16), wout_ref[S5_W:, :])
    x1 = x_ref[...] + mixed
    xn = _rms(x1, gffn_ref[...]).astype(BF16)

    row = lax.broadcasted_iota(jnp.int32, (tm, FF_CHUNK), 0)
    if long_seq:
        @pl.when(pl.program_id(1) == 0)
        def _():
            carry_ref[...] = p1_ref[...]
        t_in = row
    else:
        t_in = row % seq_steps

    for c in range(D_FF // FF_CHUNK):
        sl = slice(c * FF_CHUNK, (c + 1) * FF_CHUNK)
        gate = _dot(xn, wgate_ref[:, sl])
        up = _dot(xn, wup_ref[:, sl])
        if long_seq:
            back1 = jnp.broadcast_to(carry_ref[7:8, sl], gate.shape)
            back2 = jnp.where(t_in == 0, carry_ref[6:7, sl], back1)
            carry_ref[:, sl] = gate[tm - 8:, :]
            gate_ref[:, sl] = gate[tm - 8:, :]
        else:
            back1 = p1_ref[:, sl]
            back2 = p2_ref[:, sl]
            gate_ref[:, sl] = gate
        prev1 = jnp.where(t_in >= 1, pltpu.roll(gate, 1, 0), back1)
        prev2 = jnp.where(t_in >= 2, pltpu.roll(gate, 2, 0), back2)
        conv = (convb_ref[:, sl] + convw_ref[0:1, sl] * prev2 + convw_ref[1:2, sl] * prev1
                + convw_ref[2:3, sl] * gate)
        hbuf[:, sl] = (conv * _sigmoid(conv) * up).astype(BF16)

    x2 = x1 + _dot(hbuf[...], wdown_ref[...])
    y_ref[...] = _rms(x2, gfin_ref[...])


def _post_call(x2d, yt, u, y_mla, p1, p2, w, nb, tm, seq_steps):
    rows = x2d.shape[0]
    nt = rows // (nb * tm)
    assert nb * nt * tm == rows and tm % S5_CHUNK == 0
    long_seq = seq_steps == 0
    row = lambda width: pl.BlockSpec((tm, width), lambda b, i: (b * nt + i, 0))
    yt_spec = pl.BlockSpec((S5_NOCT, S5_CHUNK, tm // S5_CHUNK, LANE), lambda b, i: (0, 0, b * nt + i, 0))
    if long_seq:
        hist = [_const_spec((8, D_FF)), _const_spec((8, D_FF))]
        gate_spec = pl.BlockSpec((None, 8, D_FF), lambda b, i: (b, 0, 0))
        gate_shape = jax.ShapeDtypeStruct((nb, 8, D_FF), F32)
    else:
        hist = [row(D_FF), row(D_FF)]
        gate_spec = row(D_FF)
        gate_shape = jax.ShapeDtypeStruct((rows, D_FF), F32)
    return pl.pallas_call(
        functools.partial(_post_kernel, tm=tm, seq_steps=seq_steps),
        grid=(nb, nt),
        in_specs=[row(D_MODEL), yt_spec, row(S5_W), row(MLA_HEADS * V_HEAD)] + hist + [
            _const_spec((1, S5_W)), _const_spec((S5_W, S5_W)), _const_spec((D_MODEL, D_MODEL)),
            _const_spec((1, D_MODEL)), _const_spec((D_MODEL, D_FF)), _const_spec((D_MODEL, D_FF)),
            _const_spec((CONV_W, D_FF)), _const_spec((1, D_FF)), _const_spec((D_FF, D_MODEL)),
            _const_spec((1, D_MODEL))],
        out_specs=[row(D_MODEL), gate_spec],
        out_shape=[jax.ShapeDtypeStruct((rows, D_MODEL), F32), gate_shape],
        scratch_shapes=[pltpu.VMEM((S5_NOCT, tm, LANE), F32), pltpu.VMEM((tm, D_FF), BF16),
                        pltpu.VMEM((8, D_FF), F32)],
        compiler_params=_params("arbitrary", "arbitrary"),
        name="post",
    )(x2d, yt, u, y_mla, p1, p2, w["s5_d"], w["w_glu"], w["w_out"], w["g_ffn"], w["w_gate"], w["w_up"],
      w["conv_w"], w["conv_b"], w["w_down"], w["g_final"])


def _rope_tables(pos):
    half = QK_ROPE // 2
    inv = ROPE_BASE ** (-jnp.arange(half, dtype=F32) / half)
    ang = pos.astype(F32)[:, None] * inv[None, :]
    cos, sin = jnp.cos(ang), jnp.sin(ang)
    pad = jnp.zeros((pos.shape[0], ROPE_PAD - QK_ROPE), F32)
    return jnp.concatenate([cos, cos, pad], axis=1), jnp.concatenate([-sin, sin, pad], axis=1)


def kernel(x_prompt, x_sample, cache_ckv, cache_kr, state_s5_re, state_s5_im, state_conv, page_table, meta_tokens, g_mix, w_in, g_q, w_uq, g_kv, w_uk, w_uv, s5_a_re, s5_a_im, s5_log_dt, s5_b_re, s5_b_im, s5_c_re, s5_c_im, s5_d, w_glu, w_out, g_ffn, w_gate, w_up, conv_w, conv_b, w_down, g_final):
    assert w_in.shape[0] == 1, "single-layer step"
    nb, seq, _ = x_prompt.shape
    db, steps, _ = x_sample.shape
    n_past = page_table.shape[1] * PAGE_SIZE
    assert seq % S5_CHUNK == 0 and N_META % S5_CHUNK == 0 and steps == S5_CHUNK

    w_in_pad = jnp.concatenate([w_in[0], jnp.zeros((D_MODEL, IN_PAD - w_in.shape[2]), F32)], axis=1)
    wq = w_uq[0].reshape(Q_RANK, MLA_HEADS, QK_NOPE + QK_ROPE)
    wq_rope = jnp.concatenate([wq[:, :, QK_NOPE:], jnp.zeros((Q_RANK, MLA_HEADS, ROPE_PAD - QK_ROPE), F32)], axis=2)
    wq_perm = jnp.concatenate([wq[:, :, :QK_NOPE].reshape(Q_RANK, -1), wq_rope.reshape(Q_RANK, -1)], axis=1)
    w = {
        "g_mix": g_mix[0].reshape(1, -1), "w_in": w_in_pad.astype(BF16),
        "g_q": g_q[0].reshape(1, -1), "w_uq": wq_perm.astype(BF16), "w_uq_t": wq_perm.T.astype(BF16),
        "w_uk_t": w_uk[0].transpose(1, 2, 0).astype(BF16),
        "w_uk": w_uk[0].transpose(1, 0, 2).astype(BF16),
        "g_kv": g_kv[0].reshape(1, -1),
        "s5_d": s5_d[0].reshape(1, -1), "w_glu": w_glu[0].astype(BF16), "w_out": w_out[0].astype(BF16),
        "g_ffn": g_ffn[0].reshape(1, -1), "w_gate": w_gate[0].astype(BF16), "w_up": w_up[0].astype(BF16),
        "conv_w": conv_w[0], "conv_b": conv_b[0].reshape(1, -1), "w_down": w_down[0].astype(BF16),
        "g_final": g_final.reshape(1, -1),
    }
    w_uv_h = w_uv[0].transpose(1, 0, 2).astype(BF16)
    ops = _s5_operators(s5_a_re[0], s5_a_im[0], s5_log_dt[0], s5_b_re[0], s5_b_im[0], s5_c_re[0], s5_c_im[0])

    cos_m, sin_m = _rope_tables(jnp.arange(N_META, dtype=jnp.int32))
    cos_p, sin_p = _rope_tables(N_META + jnp.arange(seq, dtype=jnp.int32))
    tm_p = 512
    tm_s = 512
    cos_s, sin_s = _rope_tables(n_past + jnp.arange(tm_s, dtype=jnp.int32) % steps)

    tiles_per_seq = seq // tm_p
    u_m, ut_m, q_m, kcat_m, ckv_m, kr_m = _pre_call(meta_tokens, cos_m, sin_m, lambda i: (0, 0), N_META, BF16, w)
    u_p, ut_p, qt_p, kcat_p, ckv_p, kr_p, vt_p = _pre_call(
        x_prompt.reshape(nb * seq, D_MODEL), cos_p, sin_p, lambda i: (i % tiles_per_seq, 0), tm_p, BF16, w,
        q_transposed=True)
    u_s, ut_s, q_s, kcat_s, ckv_s, kr_s = _pre_call(x_sample.reshape(db * steps, D_MODEL), cos_s, sin_s,
                                                    lambda i: (0, 0), tm_s, F32, w)

    zero_state = jnp.zeros((1, 1, S5_STATE_W), F32)
    yt_m, fm_re, fm_im = _s5_mixer(ut_m, zero_state, zero_state, ops, 1)
    yt_p, s5_re_p, s5_im_p = _s5_mixer(ut_p, jnp.broadcast_to(fm_re, (nb, 1, S5_STATE_W)),
                                       jnp.broadcast_to(fm_im, (nb, 1, S5_STATE_W)), ops, nb)
    yt_s, s5_re_s, s5_im_s = _s5_mixer(ut_s, state_s5_re[0].reshape(db, 1, S5_STATE_W),
                                       state_s5_im[0].reshape(db, 1, S5_STATE_W), ops, db)

    kmeta = jnp.concatenate([kcat_m, jnp.zeros((PAGE_SIZE - N_META, QK_CAT), BF16)], axis=0)
    ymla_m = _attn_meta(q_m, kmeta, w_uv_h)
    ymla_p = _attn_prompt(qt_p, kcat_p.reshape(nb, seq, QK_CAT), vt_p, kcat_m, kcat_m[:, :KV_RANK].T,
                          w_uv[0].transpose(1, 2, 0).astype(BF16), nb, tq=256)
    ymla_s = _attn_sample(page_table, q_s, kcat_s, cache_ckv, jnp.swapaxes(cache_kr, 2, 3), w_uv_h, steps)

    zeros_hist = jnp.zeros((8, D_FF), F32)
    _, gate_m = _post_call(meta_tokens, yt_m, u_m, ymla_m, zeros_hist, zeros_hist, w, 1, N_META, 0)
    y_p, gate_p = _post_call(x_prompt.reshape(nb * seq, D_MODEL), yt_p, u_p, ymla_p, gate_m[0], gate_m[0],
                             w, nb, tm_p, 0)
    sc = state_conv[0]
    back1 = jnp.pad(sc[:, 1:2], ((0, 0), (0, steps - 1), (0, 0))).reshape(db * steps, D_FF)
    back2 = jnp.pad(sc, ((0, 0), (0, steps - 2), (0, 0))).reshape(db * steps, D_FF)
    y_s, gate_s = _post_call(x_sample.reshape(db * steps, D_MODEL), yt_s, u_s, ymla_s, back1, back2,
                             w, 1, tm_s // 2, steps)

    def with_meta(meta_rows, tok_rows, width):
        return jnp.concatenate([jnp.broadcast_to(meta_rows[None], (nb, N_META, width)),
                                tok_rows.reshape(nb, seq, width)], axis=1)[None]

    return (y_p.reshape(nb, seq, D_MODEL), y_s.reshape(db, steps, D_MODEL),
            with_meta(ckv_m, ckv_p, KV_RANK), with_meta(kr_m, kr_p, QK_ROPE),
            s5_re_p.reshape(1, nb, S5_GROUPS, S5_STATE), s5_im_p.reshape(1, nb, S5_GROUPS, S5_STATE),
            gate_p[:, 8 - (CONV_W - 1):][None],
            ckv_s.reshape(1, db, steps, KV_RANK), kr_s.reshape(1, db, steps, QK_ROPE),
            s5_re_s.reshape(1, db, S5_GROUPS, S5_STATE), s5_im_s.reshape(1, db, S5_GROUPS, S5_STATE),
            gate_s.reshape(db, steps, D_FF)[:, steps - (CONV_W - 1):][None])
```

```python
import functools
import math

import jax
import jax.numpy as jnp
from jax import lax
from jax.experimental import pallas as pl
from jax.experimental.pallas import tpu as pltpu

F32 = jnp.float32
BF16 = jnp.bfloat16

D_MODEL = 1024
N_META = 16
S5_W = 512
S5_GROUP = 16
S5_GROUPS = 32
S5_STATE = 64
MLA_HEADS = 4
QK_NOPE = 128
QK_ROPE = 64
V_HEAD = 128
Q_RANK = 384
KV_RANK = 256
D_FF = 2816
CONV_W = 3
ROPE_BASE = 10000.0
EPS = 1e-6
PAGE_SIZE = 128
ATTN_SCALE = 1.0 / math.sqrt(QK_NOPE + QK_ROPE)
LOG2_E = math.log2(math.e)

LANE = 128
ROPE_PAD = LANE
QK_CAT = KV_RANK + ROPE_PAD
IN_PAD = S5_W + Q_RANK + KV_RANK + ROPE_PAD
S5_CHUNK = 8
S5_OCT = LANE // S5_GROUP
S5_NOCT = S5_GROUPS // S5_OCT
S5_STATE_W = S5_GROUPS * S5_STATE
S5_SCAN_W = 512
S5_POW_PAD = 16
FF_CHUNK = 256
PAGE_GROUP = 4
CHUNK_GROUPS = 4
VMEM_LIMIT = 56 * 1024 * 1024
NEG_INF = float("-inf")
HI = lax.Precision.HIGHEST


def _const_spec(shape):
    nd = len(shape)
    return pl.BlockSpec(shape, lambda *_: (0,) * nd, pipeline_mode=pl.Buffered(1))


def _params(*semantics):
    return pltpu.CompilerParams(dimension_semantics=semantics, vmem_limit_bytes=VMEM_LIMIT)


def _rms(x, g):
    return x * lax.rsqrt(jnp.mean(x * x, axis=-1, keepdims=True) + EPS) * g


def _rope_slab(x, cos, sin):
    lane = lax.broadcasted_iota(jnp.int32, x.shape, 1)
    half = QK_ROPE // 2
    swapped = jnp.where(lane < half, pltpu.roll(x, LANE - half, 1), pltpu.roll(x, half, 1))
    return x * cos + swapped * sin


def _dot(a, b):
    return jnp.dot(a, b, preferred_element_type=F32)


def _dot_nt(a, b):
    return lax.dot_general(a, b, (((1,), (1,)), ((), ())), preferred_element_type=F32)


def _pre_kernel(x_ref, cos_ref, sin_ref, cost_ref, sint_ref, gmix_ref, win_ref, gq_ref, wuq_ref, wuk_ref, gkv_ref,
                u_ref, ut_ref, q_ref, kcat_ref, ckv_ref, kr_ref, *rest, q_transposed):
    x = x_ref[...]
    xn = _rms(x, gmix_ref[...]).astype(BF16)
    z = _dot(xn, win_ref[...])
    u_ref[...] = z[:, :S5_W]
    slab_ref = rest[-1]
    chunks = u_ref.shape[0] // S5_CHUNK
    for o in range(S5_NOCT):
        slab_ref[o] = z[:, o * LANE:(o + 1) * LANE]
        for s in range(S5_CHUNK):
            ut_ref[o, s] = slab_ref[o, pl.ds(s, chunks, stride=S5_CHUNK), :].astype(BF16)
    cq = z[:, S5_W:S5_W + Q_RANK]
    ckv_raw = z[:, S5_W + Q_RANK:S5_W + Q_RANK + KV_RANK]
    kr_raw = z[:, S5_W + Q_RANK + KV_RANK:]
    cqn = _rms(cq, gq_ref[...]).astype(BF16)
    nope_w = MLA_HEADS * QK_NOPE
    half = QK_ROPE // 2
    if q_transposed:
        qt = _dot_nt(wuq_ref[...], cqn)
        cost = cost_ref[...]
        sint = sint_ref[...]
        for h in range(MLA_HEADS):
            qn = qt[h * QK_NOPE:(h + 1) * QK_NOPE].astype(BF16)
            q_lat = _dot(wuk_ref[h], qn) * (ATTN_SCALE * LOG2_E)
            xr = qt[nope_w + h * ROPE_PAD:nope_w + (h + 1) * ROPE_PAD]
            swapped = jnp.concatenate([xr[half:QK_ROPE], xr[:half], xr[QK_ROPE:]], axis=0)
            qr = (xr * cost + swapped * sint) * (ATTN_SCALE * LOG2_E)
            q_ref[h, :KV_RANK, :] = q_lat.astype(q_ref.dtype)
            q_ref[h, KV_RANK:, :] = qr.astype(q_ref.dtype)
    else:
        q = _dot(cqn, wuq_ref[...])
        for h in range(MLA_HEADS):
            qn = q[:, h * QK_NOPE:(h + 1) * QK_NOPE].astype(BF16)
            q_lat = _dot(qn, wuk_ref[h]) * ATTN_SCALE
            qr = _rope_slab(q[:, nope_w + h * ROPE_PAD:nope_w + (h + 1) * ROPE_PAD],
                            cos_ref[...], sin_ref[...]) * ATTN_SCALE
            q_ref[h, :, :KV_RANK] = q_lat.astype(q_ref.dtype)
            q_ref[h, :, KV_RANK:] = qr.astype(q_ref.dtype)
    ckv = _rms(ckv_raw, gkv_ref[...])
    kr = _rope_slab(kr_raw, cos_ref[...], sin_ref[...])
    ckv_ref[...] = ckv
    kr_ref[...] = kr[:, :QK_ROPE]
    kcat_ref[:, :KV_RANK] = ckv.astype(kcat_ref.dtype)
    kcat_ref[:, KV_RANK:] = kr.astype(kcat_ref.dtype)
    if q_transposed:
        rest[0][...] = ckv.T.astype(BF16)


def _pre_call(x2d, cos, sin, tab_map, tm, qdtype, w, q_transposed=False):
    rows = x2d.shape[0]
    assert rows % tm == 0 and tm % S5_CHUNK == 0
    row = lambda width: pl.BlockSpec((tm, width), lambda i: (i, 0))
    tab_map_t = lambda i: tab_map(i)[::-1]
    if q_transposed:
        wuq, wuk = w["w_uq_t"], w["w_uk"]
        q_spec = pl.BlockSpec((MLA_HEADS, QK_CAT, tm), lambda i: (0, 0, i))
        q_shape = jax.ShapeDtypeStruct((MLA_HEADS, QK_CAT, rows), qdtype)
        extra_specs = [pl.BlockSpec((KV_RANK, tm), lambda i: (0, i))]
        extra_shapes = [jax.ShapeDtypeStruct((KV_RANK, rows), BF16)]
    else:
        wuq, wuk = w["w_uq"], w["w_uk_t"]
        q_spec = pl.BlockSpec((MLA_HEADS, tm, QK_CAT), lambda i: (0, i, 0))
        q_shape = jax.ShapeDtypeStruct((MLA_HEADS, rows, QK_CAT), qdtype)
        extra_specs, extra_shapes = [], []
    return pl.pallas_call(
        functools.partial(_pre_kernel, q_transposed=q_transposed),
        grid=(rows // tm,),
        in_specs=[
            row(D_MODEL),
            pl.BlockSpec((tm, ROPE_PAD), tab_map),
            pl.BlockSpec((tm, ROPE_PAD), tab_map),
            pl.BlockSpec((ROPE_PAD, tm), tab_map_t),
            pl.BlockSpec((ROPE_PAD, tm), tab_map_t),
            _const_spec((1, D_MODEL)),
            _const_spec((D_MODEL, IN_PAD)),
            _const_spec((1, Q_RANK)),
            _const_spec(wuq.shape),
            _const_spec(wuk.shape),
            _const_spec((1, KV_RANK)),
        ],
        out_specs=[row(S5_W), pl.BlockSpec((S5_NOCT, S5_CHUNK, tm // S5_CHUNK, LANE), lambda i: (0, 0, i, 0)),
                   q_spec, row(QK_CAT), row(KV_RANK), row(QK_ROPE)] + extra_specs,
        out_shape=[
            jax.ShapeDtypeStruct((rows, S5_W), F32),
            jax.ShapeDtypeStruct((S5_NOCT, S5_CHUNK, rows // S5_CHUNK, LANE), BF16),
            q_shape,
            jax.ShapeDtypeStruct((rows, QK_CAT), qdtype),
            jax.ShapeDtypeStruct((rows, KV_RANK), F32),
            jax.ShapeDtypeStruct((rows, QK_ROPE), F32),
        ] + extra_shapes,
        scratch_shapes=[pltpu.VMEM((S5_NOCT, tm, LANE), F32)],
        compiler_params=_params("arbitrary"),
        name="pre",
    )(x2d, cos, sin, cos.T, sin.T, w["g_mix"], w["w_in"], w["g_q"], wuq, wuk, w["g_kv"])


def _chunk_inputs(ut_ref):
    return jnp.concatenate([ut_ref[s] for s in range(S5_CHUNK)], axis=1)


def _s5_state_in_kernel(ut_ref, wre_ref, wim_ref, sre_ref, sim_ref):
    u = _chunk_inputs(ut_ref)
    sre_ref[...] = _dot(u, wre_ref[0])
    sim_ref[...] = _dot(u, wim_ref[0])


def _s5_scan_kernel(sre_ref, sim_ref, h0re_ref, h0im_ref, are_ref, aim_ref, hre_ref, him_ref, fre_ref, fim_ref):
    nb, nc, width = sre_ref.shape
    ar = are_ref[...].reshape(1, 1, width)
    ai = aim_ref[...].reshape(1, 1, width)

    def body(c, carry):
        hr, hi = carry
        hre_ref[:, pl.ds(c, 1), :] = hr
        him_ref[:, pl.ds(c, 1), :] = hi
        sr = sre_ref[:, pl.ds(c, 1), :]
        si = sim_ref[:, pl.ds(c, 1), :]
        return ar * hr - ai * hi + sr, ar * hi + ai * hr + si

    hr, hi = lax.fori_loop(0, nc, body, (h0re_ref[...], h0im_ref[...]))
    fre_ref[...] = hr
    fim_ref[...] = hi


def _s5_out_kernel(ut_ref, hre_ref, him_ref, t_ref, mre_ref, mim_ref, yt_ref):
    y = (_dot(_chunk_inputs(ut_ref), t_ref[0]) + _dot(hre_ref[...].astype(BF16), mre_ref[0])
         + _dot(him_ref[...].astype(BF16), mim_ref[0]))
    for t in range(S5_CHUNK):
        yt_ref[t] = y[:, t * LANE:(t + 1) * LANE]


def _s5_mixer(ut, h0_re, h0_im, ops, nb):
    rows = ut.shape[2]
    nc = rows // nb
    tr = min(rows, 1024)
    assert rows % tr == 0
    ow = S5_OCT * S5_STATE
    cw = S5_CHUNK * LANE
    ut_spec = pl.BlockSpec((None, S5_CHUNK, tr, LANE), lambda q, r: (q, 0, r, 0))
    st_spec = pl.BlockSpec((tr, ow), lambda q, r: (r, q))
    oct_spec = lambda a, b: pl.BlockSpec((1, a, b), lambda q, r: (q, 0, 0))
    s_re, s_im = pl.pallas_call(
        _s5_state_in_kernel,
        grid=(S5_NOCT, rows // tr),
        in_specs=[ut_spec, oct_spec(cw, ow), oct_spec(cw, ow)],
        out_specs=[st_spec, st_spec],
        out_shape=[jax.ShapeDtypeStruct((rows, S5_STATE_W), F32)] * 2,
        compiler_params=_params("arbitrary", "arbitrary"),
        name="s5_state_in",
    )(ut, ops["w_re"], ops["w_im"])

    blk = pl.BlockSpec((nb, nc, S5_SCAN_W), lambda j: (0, 0, j))
    one = pl.BlockSpec((nb, 1, S5_SCAN_W), lambda j: (0, 0, j))
    vec = pl.BlockSpec((1, S5_SCAN_W), lambda j: (0, j))
    h_re, h_im, f_re, f_im = pl.pallas_call(
        _s5_scan_kernel,
        grid=(S5_STATE_W // S5_SCAN_W,),
        in_specs=[blk, blk, one, one, vec, vec],
        out_specs=[blk, blk, one, one],
        out_shape=[jax.ShapeDtypeStruct((nb, nc, S5_STATE_W), F32)] * 2
                  + [jax.ShapeDtypeStruct((nb, 1, S5_STATE_W), F32)] * 2,
        compiler_params=_params("arbitrary"),
        name="s5_scan",
    )(s_re.reshape(nb, nc, S5_STATE_W), s_im.reshape(nb, nc, S5_STATE_W), h0_re, h0_im, ops["a_re"], ops["a_im"])

    yt = pl.pallas_call(
        _s5_out_kernel,
        grid=(S5_NOCT, rows // tr),
        in_specs=[ut_spec, st_spec, st_spec, oct_spec(cw, cw), oct_spec(ow, cw), oct_spec(ow, cw)],
        out_specs=ut_spec,
        out_shape=jax.ShapeDtypeStruct(ut.shape, F32),
        compiler_params=_params("arbitrary", "arbitrary"),
        name="s5_out",
    )(ut, h_re.reshape(rows, S5_STATE_W), h_im.reshape(rows, S5_STATE_W), ops["t"], ops["m_re"], ops["m_im"])
    return yt, f_re, f_im


def _s5_operator_kernel(btr_ref, bti_ref, cr_ref, ci_ref, pl_re, pl_im, pr_re, pr_im,
                        t_ref, wre_ref, wim_ref, mre_ref, mim_ref):
    btr, bti = btr_ref[0], bti_ref[0]
    cr, ci = cr_ref[0], ci_ref[0]
    hi_dot = lambda a, b: jnp.dot(a, b, precision=HI, preferred_element_type=F32)
    kerns = []
    for k in range(S5_CHUNK + 1):
        ar, ai = pr_re[0, :, k:k + 1], pr_im[0, :, k:k + 1]
        cpr = cr * ar - ci * ai
        cpi = cr * ai + ci * ar
        if k < S5_CHUNK:
            kerns.append((hi_dot(btr, cpr) - hi_dot(bti, cpi)).astype(BF16))
            s = S5_CHUNK - 1 - k
            lr, li = pl_re[0, k:k + 1, :], pl_im[0, k:k + 1, :]
            wre_ref[0, s * LANE:(s + 1) * LANE, :] = (btr * lr - bti * li).astype(BF16)
            wim_ref[0, s * LANE:(s + 1) * LANE, :] = (btr * li + bti * lr).astype(BF16)
        if k >= 1:
            t = k - 1
            mre_ref[0, :, t * LANE:(t + 1) * LANE] = cpr.astype(BF16)
            mim_ref[0, :, t * LANE:(t + 1) * LANE] = (-cpi).astype(BF16)
    zero = jnp.zeros((LANE, LANE), BF16)
    for s in range(S5_CHUNK):
        for t in range(S5_CHUNK):
            t_ref[0, s * LANE:(s + 1) * LANE, t * LANE:(t + 1) * LANE] = kerns[t - s] if t >= s else zero


def _s5_operators(a_re, a_im, log_dt, b_re, b_im, c_re, c_im):
    steps = S5_CHUNK
    dt = jnp.exp(log_dt)[:, None]
    k = jnp.arange(steps + 1, dtype=F32)[:, None, None]
    mag = jnp.exp(k * (dt * a_re)[None])
    pw_r = mag * jnp.cos(k * (dt * a_im)[None])
    pw_i = mag * jnp.sin(k * (dt * a_im)[None])
    abr, abi = pw_r[1], pw_i[1]
    num_re, num_im = abr - 1.0, abi
    den = a_re * a_re + a_im * a_im
    f_re = (num_re * a_re + num_im * a_im) / den
    f_im = (num_im * a_re - num_re * a_im) / den
    bbr = f_re[..., None] * b_re - f_im[..., None] * b_im
    bbi = f_re[..., None] * b_im + f_im[..., None] * b_re
    same_group = jnp.arange(S5_OCT)[:, None, None, None] == jnp.arange(S5_OCT)[None, None, :, None]
    ow = S5_OCT * S5_STATE

    def block_diag(a):
        a = a.reshape(S5_NOCT, S5_OCT, a.shape[1], 1, a.shape[2])
        return jnp.where(same_group, a, 0.0).reshape(S5_NOCT, S5_OCT * a.shape[2], S5_OCT * a.shape[4])

    pad_k = S5_POW_PAD - (steps + 1)
    lanes = lambda p: jnp.pad(p.reshape(steps + 1, S5_NOCT, ow).transpose(1, 0, 2), ((0, 0), (0, pad_k), (0, 0)))
    rows = lambda p: jnp.pad(p.reshape(steps + 1, S5_NOCT, ow).transpose(1, 2, 0), ((0, 0), (0, 0), (0, pad_k)))
    bt_re, bt_im = block_diag(bbr.transpose(0, 2, 1)), block_diag(bbi.transpose(0, 2, 1))
    cb_re, cb_im = block_diag(c_re.transpose(0, 2, 1)), block_diag(c_im.transpose(0, 2, 1))
    cw = steps * LANE
    oct_spec = lambda a, b: pl.BlockSpec((1, a, b), lambda q: (q, 0, 0))
    t_mat, w_re, w_im, m_re, m_im = pl.pallas_call(
        _s5_operator_kernel,
        grid=(S5_NOCT,),
        in_specs=[oct_spec(LANE, ow), oct_spec(LANE, ow), oct_spec(ow, LANE), oct_spec(ow, LANE),
                  oct_spec(S5_POW_PAD, ow), oct_spec(S5_POW_PAD, ow), oct_spec(ow, S5_POW_PAD),
                  oct_spec(ow, S5_POW_PAD)],
        out_specs=[oct_spec(cw, cw), oct_spec(cw, ow), oct_spec(cw, ow), oct_spec(ow, cw), oct_spec(ow, cw)],
        out_shape=[jax.ShapeDtypeStruct((S5_NOCT, cw, cw), BF16)]
                  + [jax.ShapeDtypeStruct((S5_NOCT, cw, ow), BF16)] * 2
                  + [jax.ShapeDtypeStruct((S5_NOCT, ow, cw), BF16)] * 2,
        compiler_params=_params("arbitrary"),
        name="s5_operators",
    )(bt_re, bt_im, cb_re, cb_im, lanes(pw_r), lanes(pw_i), rows(pw_r), rows(pw_i))
    return {
        "t": t_mat, "w_re": w_re, "w_im": w_im, "m_re": m_re, "m_im": m_im,
        "a_re": pw_r[steps].reshape(1, S5_STATE_W), "a_im": pw_i[steps].reshape(1, S5_STATE_W),
    }


def _softmax_update_t(s, vt, m_ref, l_ref, acc_ref):
    m_prev = m_ref[...]
    m_new = jnp.maximum(m_prev, jnp.max(s, axis=0, keepdims=True))
    alpha = jnp.exp2(m_prev - m_new)
    p = jnp.exp2(s - m_new)
    l_ref[...] = alpha * l_ref[...] + jnp.sum(p, axis=0, keepdims=True)
    acc_ref[...] = alpha * acc_ref[...] + _dot(vt, p.astype(BF16))
    m_ref[...] = m_new


def _attn_prompt_kernel(qt_ref, k_ref, vt_ref, kmeta_ref, vmeta_t_ref, wuvt_ref, o_ref, m_ref, l_ref, acc_ref, s_ref,
                        *, tq):
    qi = pl.program_id(1)
    qt = jnp.concatenate([qt_ref[h] for h in range(MLA_HEADS)], axis=1)

    def keys(j):
        return pl.ds(pl.multiple_of(j * tq, tq), tq)

    def scores(j):
        return _dot(k_ref[0, keys(j), :], qt)

    def update(s, j):
        _softmax_update_t(s, vt_ref[:, keys(j)], m_ref, l_ref, acc_ref)

    s = _dot(kmeta_ref[...], qt)
    m0 = jnp.max(s, axis=0, keepdims=True)
    p = jnp.exp2(s - m0)
    m_ref[...] = m0
    l_ref[...] = jnp.sum(p, axis=0, keepdims=True)
    acc_ref[...] = _dot(vmeta_t_ref[...], p.astype(BF16))

    s_ref[...] = scores(0)

    def pair_body(kp, carry):
        j = 2 * kp
        s_b = scores(j + 1)
        s_next = scores(j + 2)
        update(s_ref[...], j)
        update(s_b, j + 1)
        s_ref[...] = s_next
        return carry

    lax.fori_loop(0, qi // 2, pair_body, 0)

    @pl.when(qi % 2 == 1)
    def _():
        s_next = scores(qi)
        update(s_ref[...], qi - 1)
        s_ref[...] = s_next

    s = s_ref[...]
    t_k = lax.broadcasted_iota(jnp.int32, s.shape, 0)
    t_q = lax.broadcasted_iota(jnp.int32, s.shape, 1) % tq
    s = jnp.where(t_k <= t_q, s, NEG_INF)
    update(s, qi)

    ot = (acc_ref[...] / l_ref[...]).astype(BF16)
    for h in range(MLA_HEADS):
        yt = _dot(wuvt_ref[h], ot[:, h * tq:(h + 1) * tq])
        o_ref[:, h * V_HEAD:(h + 1) * V_HEAD] = yt.T.astype(o_ref.dtype)


def _attn_prompt(qt, kcat, vt, kmeta, vmeta_t, w_uv_t, nb, tq):
    t = kcat.shape[1]
    nq = t // tq
    cols = MLA_HEADS * tq
    return pl.pallas_call(
        functools.partial(_attn_prompt_kernel, tq=tq),
        grid=(nb, nq),
        in_specs=[
            pl.BlockSpec((MLA_HEADS, QK_CAT, tq), lambda b, i: (0, 0, b * nq + i)),
            pl.BlockSpec((1, t, QK_CAT), lambda b, i: (b, 0, 0)),
            pl.BlockSpec((KV_RANK, t), lambda b, i: (0, b)),
            _const_spec(kmeta.shape),
            _const_spec(vmeta_t.shape),
            _const_spec(w_uv_t.shape),
        ],
        out_specs=pl.BlockSpec((tq, MLA_HEADS * V_HEAD), lambda b, i: (b * nq + i, 0)),
        out_shape=jax.ShapeDtypeStruct((nb * t, MLA_HEADS * V_HEAD), BF16),
        scratch_shapes=[pltpu.VMEM((1, cols), F32), pltpu.VMEM((1, cols), F32), pltpu.VMEM((KV_RANK, cols), F32),
                        pltpu.VMEM((tq, cols), F32)],
        compiler_params=_params("arbitrary", "arbitrary"),
        name="attn_prompt",
    )(qt, kcat, vt, kmeta, vmeta_t, w_uv_t)


def _attn_meta_kernel(q_ref, kmeta_ref, wuv_ref, o_ref):
    rows = MLA_HEADS * N_META
    q = q_ref[...].reshape(rows, QK_CAT)
    km = kmeta_ref[...]
    s = _dot_nt(q, km)
    t_q = lax.broadcasted_iota(jnp.int32, s.shape, 0) % N_META
    t_k = lax.broadcasted_iota(jnp.int32, s.shape, 1)
    s = jnp.where(t_k <= t_q, s, NEG_INF)
    p = jnp.exp(s - jnp.max(s, axis=1, keepdims=True))
    o = _dot(p.astype(BF16), km[:, :KV_RANK]) / jnp.sum(p, axis=1, keepdims=True)
    o = o.astype(BF16)
    for h in range(MLA_HEADS):
        o_ref[:, h * V_HEAD:(h + 1) * V_HEAD] = _dot(o[h * N_META:(h + 1) * N_META], wuv_ref[h]).astype(o_ref.dtype)


def _attn_meta(q, kmeta, w_uv):
    return pl.pallas_call(
        _attn_meta_kernel,
        out_shape=jax.ShapeDtypeStruct((N_META, MLA_HEADS * V_HEAD), BF16),
        name="attn_meta",
    )(q, kmeta, w_uv)


def _attn_sample_kernel(pt_ref, q_ref, knew_ref, wuv_ref, ckv_hbm, kr_hbm, o_ref, ckv_buf, kr_buf, sem,
                        *, n_pages, steps):
    b = pl.program_id(0)
    slot = b % 2

    def page_copies(seq, into, j):
        page = pt_ref[seq, j]
        return (pltpu.make_async_copy(ckv_hbm.at[0, page], ckv_buf.at[into, j], sem.at[into, 0]),
                pltpu.make_async_copy(kr_hbm.at[0, page], kr_buf.at[into, j], sem.at[into, 1]))

    def for_each_page(seq, into, act):
        def body(j, carry):
            for copy in page_copies(seq, into, j):
                act(copy)
            return carry
        lax.fori_loop(0, n_pages, body, 0, unroll=4)

    @pl.when(b == 0)
    def _():
        for_each_page(0, 0, lambda copy: copy.start())

    @pl.when(b + 1 < pl.num_programs(0))
    def _():
        for_each_page(b + 1, 1 - slot, lambda copy: copy.start())

    pltpu.make_async_copy(ckv_hbm.at[0, pl.ds(0, n_pages)], ckv_buf.at[slot], sem.at[slot, 0]).wait()
    pltpu.make_async_copy(kr_hbm.at[0, pl.ds(0, n_pages)], kr_buf.at[slot], sem.at[slot, 1]).wait()

    rows = MLA_HEADS * steps
    q = q_ref[...].reshape(rows, QK_CAT).astype(BF16)
    q_lat = q[:, :KV_RANK]
    q_rope = q[:, KV_RANK:KV_RANK + QK_ROPE]
    groups = range(0, n_pages, PAGE_GROUP)

    def latent_keys(j0):
        return jnp.concatenate([ckv_buf[slot, j].astype(BF16) for j in range(j0, j0 + PAGE_GROUP)], axis=0)

    def chunk_scores(c):
        parts = []
        for j0 in groups[c * CHUNK_GROUPS:(c + 1) * CHUNK_GROUPS]:
            kr_t = jnp.concatenate([kr_buf[slot, j].astype(BF16) for j in range(j0, j0 + PAGE_GROUP)], axis=1)
            parts.append(_dot_nt(q_lat, latent_keys(j0)) + _dot(q_rope, kr_t))
        return jnp.concatenate(parts, axis=1)

    kn = jnp.concatenate([knew_ref[...], jnp.zeros((PAGE_SIZE - steps, QK_CAT), F32)], axis=0).astype(BF16)
    s_new = _dot_nt(q, kn)
    t_q = lax.broadcasted_iota(jnp.int32, s_new.shape, 0) % steps
    t_k = lax.broadcasted_iota(jnp.int32, s_new.shape, 1)
    s_new = jnp.where(t_k <= t_q, s_new, NEG_INF)

    gk = PAGE_GROUP * PAGE_SIZE
    n_chunks = len(groups) // CHUNK_GROUPS
    m = jnp.full((rows, 1), NEG_INF, F32)
    l = jnp.zeros((rows, 1), F32)
    acc = jnp.zeros((rows, KV_RANK), F32)
    s_cur = chunk_scores(0)
    for c in range(n_chunks + 1):
        s_next = chunk_scores(c + 1) if c + 1 < n_chunks else s_new
        m_new = jnp.maximum(m, jnp.max(s_cur, axis=1, keepdims=True))
        alpha = jnp.exp(m - m_new)
        p = jnp.exp(s_cur - m_new)
        l = alpha * l + jnp.sum(p, axis=1, keepdims=True)
        p = p.astype(BF16)
        if c < n_chunks:
            chunk_groups = groups[c * CHUNK_GROUPS:(c + 1) * CHUNK_GROUPS]
            pv = _dot(p[:, :gk], latent_keys(chunk_groups[0]))
            for i, j0 in enumerate(chunk_groups[1:], start=1):
                pv = pv + _dot(p[:, i * gk:(i + 1) * gk], latent_keys(j0))
        else:
            pv = _dot(p, kn[:, :KV_RANK])
        acc = alpha * acc + pv
        m = m_new
        s_cur = s_next
    o = (acc / l).astype(BF16)
    for h in range(MLA_HEADS):
        o_ref[:, h * V_HEAD:(h + 1) * V_HEAD] = _dot(o[h * steps:(h + 1) * steps], wuv_ref[h])


def _attn_sample(page_table, q, knew, cache_ckv, cache_kr_t, w_uv, steps):
    nb, n_pages = page_table.shape
    assert n_pages % (PAGE_GROUP * CHUNK_GROUPS) == 0
    grid_spec = pltpu.PrefetchScalarGridSpec(
        num_scalar_prefetch=1,
        grid=(nb,),
        in_specs=[pl.BlockSpec((MLA_HEADS, steps, QK_CAT), lambda b, pt: (0, b, 0)),
                  pl.BlockSpec((steps, QK_CAT), lambda b, pt: (b, 0)),
                  pl.BlockSpec(w_uv.shape, lambda b, pt: (0, 0, 0)),
                  pl.BlockSpec(memory_space=pl.ANY),
                  pl.BlockSpec(memory_space=pl.ANY)],
        out_specs=pl.BlockSpec((steps, MLA_HEADS * V_HEAD), lambda b, pt: (b, 0)),
        scratch_shapes=[pltpu.VMEM((2, n_pages, PAGE_SIZE, KV_RANK), F32),
                        pltpu.VMEM((2, n_pages, QK_ROPE, PAGE_SIZE), F32),
                        pltpu.SemaphoreType.DMA((2, 2))],
    )
    return pl.pallas_call(
        functools.partial(_attn_sample_kernel, n_pages=n_pages, steps=steps),
        grid_spec=grid_spec,
        out_shape=jax.ShapeDtypeStruct((nb * steps, MLA_HEADS * V_HEAD), F32),
        compiler_params=_params("arbitrary"),
        name="attn_sample",
    )(page_table, q, knew, w_uv, cache_ckv, cache_kr_t)


def _gelu_tanh(x):
    return 0.5 * x * (1.0 + jnp.tanh(math.sqrt(2.0 / math.pi) * (x + 0.044715 * (x * x * x))))


def _sigmoid(x):
    return 1.0 / (1.0 + jnp.exp(-x))


def _post_kernel(x_ref, yt_ref, u_ref, ymla_ref, p1_ref, p2_ref, d_ref, wglu_ref, wout_ref, gffn_ref,
                 wgate_ref, wup_ref, convw_ref, convb_ref, wdown_ref, gfin_ref,
                 y_ref, gate_ref, ybuf, hbuf, carry_ref, *, tm, seq_steps):
    long_seq = seq_steps == 0
    for o in range(S5_NOCT):
        for t in range(S5_CHUNK):
            ybuf[o, pl.ds(t, tm // S5_CHUNK, stride=S5_CHUNK), :] = yt_ref[o, t]
    y_ssm = jnp.concatenate([ybuf[o] for o in range(S5_NOCT)], axis=1)
    y = y_ssm + d_ref[...] * u_ref[...]
    y = _gelu_tanh(y)
    y = y * _sigmoid(_dot(y.astype(BF16), wglu_ref[...]))
    mixed = _dot(y.astype(BF16), wout_ref[:S5_W, :]) + _dot(ymla_ref[...].astype(BF16), wout_ref[S5_W:, :])
    x1 = x_ref[...] + mixed
    xn = _rms(x1, gffn_ref[...]).astype(BF16)

    row = lax.broadcasted_iota(jnp.int32, (tm, FF_CHUNK), 0)
    if long_seq:
        @pl.when(pl.program_id(1) == 0)
        def _():
            carry_ref[...] = p1_ref[...]
        t_in = row
    else:
        t_in = row % seq_steps

    for c in range(D_FF // FF_CHUNK):
        sl = slice(c * FF_CHUNK, (c + 1) * FF_CHUNK)
        gate = _dot(xn, wgate_ref[:, sl])
        up = _dot(xn, wup_ref[:, sl])
        if long_seq:
            back1 = jnp.broadcast_to(carry_ref[7:8, sl], gate.shape)
            back2 = jnp.where(t_in == 0, carry_ref[6:7, sl], back1)
            carry_ref[:, sl] = gate[tm - 8:, :]
            gate_ref[:, sl] = gate[tm - 8:, :]
        else:
            back1 = p1_ref[:, sl]
            back2 = p2_ref[:, sl]
            gate_ref[:, sl] = gate
        prev1 = jnp.where(t_in >= 1, pltpu.roll(gate, 1, 0), back1)
        prev2 = jnp.where(t_in >= 2, pltpu.roll(gate, 2, 0), back2)
        conv = (convb_ref[:, sl] + convw_ref[0:1, sl] * prev2 + convw_ref[1:2, sl] * prev1
                + convw_ref[2:3, sl] * gate)
        hbuf[:, sl] = (conv * _sigmoid(conv) * up).astype(BF16)

    x2 = x1 + _dot(hbuf[...], wdown_ref[...])
    y_ref[...] = _rms(x2, gfin_ref[...])


def _post_call(x2d, yt, u, y_mla, p1, p2, w, nb, tm, seq_steps):
    rows = x2d.shape[0]
    nt = rows // (nb * tm)
    assert nb * nt * tm == rows and tm % S5_CHUNK == 0
    long_seq = seq_steps == 0
    row = lambda width: pl.BlockSpec((tm, width), lambda b, i: (b * nt + i, 0))
    yt_spec = pl.BlockSpec((S5_NOCT, S5_CHUNK, tm // S5_CHUNK, LANE), lambda b, i: (0, 0, b * nt + i, 0))
    if long_seq:
        hist = [_const_spec((8, D_FF)), _const_spec((8, D_FF))]
        gate_spec = pl.BlockSpec((None, 8, D_FF), lambda b, i: (b, 0, 0))
        gate_shape = jax.ShapeDtypeStruct((nb, 8, D_FF), F32)
    else:
        hist = [row(D_FF), row(D_FF)]
        gate_spec = row(D_FF)
        gate_shape = jax.ShapeDtypeStruct((rows, D_FF), F32)
    return pl.pallas_call(
        functools.partial(_post_kernel, tm=tm, seq_steps=seq_steps),
        grid=(nb, nt),
        in_specs=[row(D_MODEL), yt_spec, row(S5_W), row(MLA_HEADS * V_HEAD)] + hist + [
            _const_spec((1, S5_W)), _const_spec((S5_W, S5_W)), _const_spec((D_MODEL, D_MODEL)),
            _const_spec((1, D_MODEL)), _const_spec((D_MODEL, D_FF)), _const_spec((D_MODEL, D_FF)),
            _const_spec((CONV_W, D_FF)), _const_spec((1, D_FF)), _const_spec((D_FF, D_MODEL)),
            _const_spec((1, D_MODEL))],
        out_specs=[row(D_MODEL), gate_spec],
        out_shape=[jax.ShapeDtypeStruct((rows, D_MODEL), F32), gate_shape],
        scratch_shapes=[pltpu.VMEM((S5_NOCT, tm, LANE), F32), pltpu.VMEM((tm, D_FF), BF16),
                        pltpu.VMEM((8, D_FF), F32)],
        compiler_params=_params("arbitrary", "arbitrary"),
        name="post",
    )(x2d, yt, u, y_mla, p1, p2, w["s5_d"], w["w_glu"], w["w_out"], w["g_ffn"], w["w_gate"], w["w_up"],
      w["conv_w"], w["conv_b"], w["w_down"], w["g_final"])


def _rope_tables(pos):
    half = QK_ROPE // 2
    inv = ROPE_BASE ** (-jnp.arange(half, dtype=F32) / half)
    ang = pos.astype(F32)[:, None] * inv[None, :]
    cos, sin = jnp.cos(ang), jnp.sin(ang)
    pad = jnp.zeros((pos.shape[0], ROPE_PAD - QK_ROPE), F32)
    return jnp.concatenate([cos, cos, pad], axis=1), jnp.concatenate([-sin, sin, pad], axis=1)


def kernel(x_prompt, x_sample, cache_ckv, cache_kr, state_s5_re, state_s5_im, state_conv, page_table, meta_tokens, g_mix, w_in, g_q, w_uq, g_kv, w_uk, w_uv, s5_a_re, s5_a_im, s5_log_dt, s5_b_re, s5_b_im, s5_c_re, s5_c_im, s5_d, w_glu, w_out, g_ffn, w_gate, w_up, conv_w, conv_b, w_down, g_final):
    assert w_in.shape[0] == 1, "single-layer step"
    nb, seq, _ = x_prompt.shape
    db, steps, _ = x_sample.shape
    n_past = page_table.shape[1] * PAGE_SIZE
    assert seq % S5_CHUNK == 0 and N_META % S5_CHUNK == 0 and steps == S5_CHUNK

    w_in_pad = jnp.concatenate([w_in[0], jnp.zeros((D_MODEL, IN_PAD - w_in.shape[2]), F32)], axis=1)
    wq = w_uq[0].reshape(Q_RANK, MLA_HEADS, QK_NOPE + QK_ROPE)
    wq_rope = jnp.concatenate([wq[:, :, QK_NOPE:], jnp.zeros((Q_RANK, MLA_HEADS, ROPE_PAD - QK_ROPE), F32)], axis=2)
    wq_perm = jnp.concatenate([wq[:, :, :QK_NOPE].reshape(Q_RANK, -1), wq_rope.reshape(Q_RANK, -1)], axis=1)
    w = {
        "g_mix": g_mix[0].reshape(1, -1), "w_in": w_in_pad.astype(BF16),
        "g_q": g_q[0].reshape(1, -1), "w_uq": wq_perm.astype(BF16), "w_uq_t": wq_perm.T.astype(BF16),
        "w_uk_t": w_uk[0].transpose(1, 2, 0).astype(BF16),
        "w_uk": w_uk[0].transpose(1, 0, 2).astype(BF16),
        "g_kv": g_kv[0].reshape(1, -1),
        "s5_d": s5_d[0].reshape(1, -1), "w_glu": w_glu[0].astype(BF16), "w_out": w_out[0].astype(BF16),
        "g_ffn": g_ffn[0].reshape(1, -1), "w_gate": w_gate[0].astype(BF16), "w_up": w_up[0].astype(BF16),
        "conv_w": conv_w[0], "conv_b": conv_b[0].reshape(1, -1), "w_down": w_down[0].astype(BF16),
        "g_final": g_final.reshape(1, -1),
    }
    w_uv_h = w_uv[0].transpose(1, 0, 2).astype(BF16)
    ops = _s5_operators(s5_a_re[0], s5_a_im[0], s5_log_dt[0], s5_b_re[0], s5_b_im[0], s5_c_re[0], s5_c_im[0])

    cos_m, sin_m = _rope_tables(jnp.arange(N_META, dtype=jnp.int32))
    cos_p, sin_p = _rope_tables(N_META + jnp.arange(seq, dtype=jnp.int32))
    tm_p = 512
    tm_s = 512
    cos_s, sin_s = _rope_tables(n_past + jnp.arange(tm_s, dtype=jnp.int32) % steps)

    tm_pre = 1024
    tiles_per_seq = seq // tm_pre
    u_m, ut_m, q_m, kcat_m, ckv_m, kr_m = _pre_call(meta_tokens, cos_m, sin_m, lambda i: (0, 0), N_META, BF16, w)
    u_p, ut_p, qt_p, kcat_p, ckv_p, kr_p, vt_p = _pre_call(
        x_prompt.reshape(nb * seq, D_MODEL), cos_p, sin_p, lambda i: (i % tiles_per_seq, 0), tm_pre, BF16, w,
        q_transposed=True)
    u_s, ut_s, q_s, kcat_s, ckv_s, kr_s = _pre_call(x_sample.reshape(db * steps, D_MODEL), cos_s, sin_s,
                                                    lambda i: (0, 0), tm_s, F32, w)

    zero_state = jnp.zeros((1, 1, S5_STATE_W), F32)
    yt_m, fm_re, fm_im = _s5_mixer(ut_m, zero_state, zero_state, ops, 1)
    yt_p, s5_re_p, s5_im_p = _s5_mixer(ut_p, jnp.broadcast_to(fm_re, (nb, 1, S5_STATE_W)),
                                       jnp.broadcast_to(fm_im, (nb, 1, S5_STATE_W)), ops, nb)
    yt_s, s5_re_s, s5_im_s = _s5_mixer(ut_s, state_s5_re[0].reshape(db, 1, S5_STATE_W),
                                       state_s5_im[0].reshape(db, 1, S5_STATE_W), ops, db)

    kmeta = jnp.concatenate([kcat_m, jnp.zeros((PAGE_SIZE - N_META, QK_CAT), BF16)], axis=0)
    ymla_m = _attn_meta(q_m, kmeta, w_uv_h)
    ymla_p = _attn_prompt(qt_p, kcat_p.reshape(nb, seq, QK_CAT), vt_p, kcat_m, kcat_m[:, :KV_RANK].T,
                          w_uv[0].transpose(1, 2, 0).astype(BF16), nb, tq=256)
    ymla_s = _attn_sample(page_table, q_s, kcat_s, cache_ckv, jnp.swapaxes(cache_kr, 2, 3), w_uv_h, steps)

    zeros_hist = jnp.zeros((8, D_FF), F32)
    _, gate_m = _post_call(meta_tokens, yt_m, u_m, ymla_m, zeros_hist, zeros_hist, w, 1, N_META, 0)
    y_p, gate_p = _post_call(x_prompt.reshape(nb * seq, D_MODEL), yt_p, u_p, ymla_p, gate_m[0], gate_m[0],
                             w, nb, tm_p, 0)
    sc = state_conv[0]
    back1 = jnp.pad(sc[:, 1:2], ((0, 0), (0, steps - 1), (0, 0))).reshape(db * steps, D_FF)
    back2 = jnp.pad(sc, ((0, 0), (0, steps - 2), (0, 0))).reshape(db * steps, D_FF)
    y_s, gate_s = _post_call(x_sample.reshape(db * steps, D_MODEL), yt_s, u_s, ymla_s, back1, back2,
                             w, 1, tm_s // 2, steps)

    def with_meta(meta_rows, tok_rows, width):
        return jnp.concatenate([jnp.broadcast_to(meta_rows[None], (nb, N_META, width)),
                                tok_rows.reshape(nb, seq, width)], axis=1)[None]

    return (y_p.reshape(nb, seq, D_MODEL), y_s.reshape(db, steps, D_MODEL),
            with_meta(ckv_m, ckv_p, KV_RANK), with_meta(kr_m, kr_p, QK_ROPE),
            s5_re_p.reshape(1, nb, S5_GROUPS, S5_STATE), s5_im_p.reshape(1, nb, S5_GROUPS, S5_STATE),
            gate_p[:, 8 - (CONV_W - 1):][None],
            ckv_s.reshape(1, db, steps, KV_RANK), kr_s.reshape(1, db, steps, QK_ROPE),
            s5_re_s.reshape(1, db, S5_GROUPS, S5_STATE), s5_im_s.reshape(1, db, S5_GROUPS, S5_STATE),
            gate_s.reshape(db, steps, D_FF)[:, steps - (CONV_W - 1):][None])
```

```python
import functools
import math

import jax
import jax.numpy as jnp
from jax import lax
from jax.experimental import pallas as pl
from jax.experimental.pallas import tpu as pltpu

F32 = jnp.float32
BF16 = jnp.bfloat16

D_MODEL = 1024
N_META = 16
S5_W = 512
S5_GROUP = 16
S5_GROUPS = 32
S5_STATE = 64
MLA_HEADS = 4
QK_NOPE = 128
QK_ROPE = 64
V_HEAD = 128
Q_RANK = 384
KV_RANK = 256
D_FF = 2816
CONV_W = 3
ROPE_BASE = 10000.0
EPS = 1e-6
PAGE_SIZE = 128
ATTN_SCALE = 1.0 / math.sqrt(QK_NOPE + QK_ROPE)
LOG2_E = math.log2(math.e)

LANE = 128
ROPE_PAD = LANE
QK_CAT = KV_RANK + ROPE_PAD
IN_PAD = S5_W + Q_RANK + KV_RANK + ROPE_PAD
S5_CHUNK = 8
S5_OCT = LANE // S5_GROUP
S5_NOCT = S5_GROUPS // S5_OCT
S5_STATE_W = S5_GROUPS * S5_STATE
S5_SCAN_W = 512
S5_POW_PAD = 16
FF_CHUNK = 256
PAGE_GROUP = 4
CHUNK_GROUPS = 4
VMEM_LIMIT = 56 * 1024 * 1024
NEG_INF = float("-inf")
HI = lax.Precision.HIGHEST


def _const_spec(shape):
    nd = len(shape)
    return pl.BlockSpec(shape, lambda *_: (0,) * nd, pipeline_mode=pl.Buffered(1))


def _params(*semantics):
    return pltpu.CompilerParams(dimension_semantics=semantics, vmem_limit_bytes=VMEM_LIMIT)


def _rms(x, g):
    return x * lax.rsqrt(jnp.mean(x * x, axis=-1, keepdims=True) + EPS) * g


def _rope_slab(x, cos, sin):
    lane = lax.broadcasted_iota(jnp.int32, x.shape, 1)
    half = QK_ROPE // 2
    swapped = jnp.where(lane < half, pltpu.roll(x, LANE - half, 1), pltpu.roll(x, half, 1))
    return x * cos + swapped * sin


def _dot(a, b):
    return jnp.dot(a, b, preferred_element_type=F32)


def _dot_nt(a, b):
    return lax.dot_general(a, b, (((1,), (1,)), ((), ())), preferred_element_type=F32)


def _pre_kernel(x_ref, cos_ref, sin_ref, cost_ref, sint_ref, gmix_ref, win_ref, gq_ref, wuq_ref, wuk_ref, gkv_ref,
                u_ref, ut_ref, q_ref, kcat_ref, ckv_ref, kr_ref, *rest, q_transposed):
    x = x_ref[...]
    xn = _rms(x, gmix_ref[...]).astype(BF16)
    z = _dot(xn, win_ref[...])
    u_ref[...] = z[:, :S5_W]
    slab_ref = rest[-1]
    chunks = u_ref.shape[0] // S5_CHUNK
    for o in range(S5_NOCT):
        slab_ref[o] = z[:, o * LANE:(o + 1) * LANE]
        for s in range(S5_CHUNK):
            ut_ref[o, s] = slab_ref[o, pl.ds(s, chunks, stride=S5_CHUNK), :].astype(BF16)
    cq = z[:, S5_W:S5_W + Q_RANK]
    ckv_raw = z[:, S5_W + Q_RANK:S5_W + Q_RANK + KV_RANK]
    kr_raw = z[:, S5_W + Q_RANK + KV_RANK:]
    cqn = _rms(cq, gq_ref[...]).astype(BF16)
    nope_w = MLA_HEADS * QK_NOPE
    half = QK_ROPE // 2
    if q_transposed:
        qt = _dot_nt(wuq_ref[...], cqn)
        cost = cost_ref[...]
        sint = sint_ref[...]
        for h in range(MLA_HEADS):
            qn = qt[h * QK_NOPE:(h + 1) * QK_NOPE].astype(BF16)
            q_lat = _dot(wuk_ref[h], qn) * (ATTN_SCALE * LOG2_E)
            xr = qt[nope_w + h * ROPE_PAD:nope_w + (h + 1) * ROPE_PAD]
            swapped = jnp.concatenate([xr[half:QK_ROPE], xr[:half], xr[QK_ROPE:]], axis=0)
            qr = (xr * cost + swapped * sint) * (ATTN_SCALE * LOG2_E)
            q_ref[h, :KV_RANK, :] = q_lat.astype(q_ref.dtype)
            q_ref[h, KV_RANK:, :] = qr.astype(q_ref.dtype)
    else:
        q = _dot(cqn, wuq_ref[...])
        for h in range(MLA_HEADS):
            qn = q[:, h * QK_NOPE:(h + 1) * QK_NOPE].astype(BF16)
            q_lat = _dot(qn, wuk_ref[h]) * ATTN_SCALE
            qr = _rope_slab(q[:, nope_w + h * ROPE_PAD:nope_w + (h + 1) * ROPE_PAD],
                            cos_ref[...], sin_ref[...]) * ATTN_SCALE
            q_ref[h, :, :KV_RANK] = q_lat.astype(q_ref.dtype)
            q_ref[h, :, KV_RANK:] = qr.astype(q_ref.dtype)
    ckv = _rms(ckv_raw, gkv_ref[...])
    kr = _rope_slab(kr_raw, cos_ref[...], sin_ref[...])
    ckv_ref[...] = ckv
    kr_ref[...] = kr[:, :QK_ROPE]
    kcat_ref[:, :KV_RANK] = ckv.astype(kcat_ref.dtype)
    kcat_ref[:, KV_RANK:] = kr.astype(kcat_ref.dtype)
    if q_transposed:
        rest[0][...] = ckv.T.astype(BF16)


def _pre_call(x2d, cos, sin, tab_map, tm, qdtype, w, q_transposed=False):
    rows = x2d.shape[0]
    assert rows % tm == 0 and tm % S5_CHUNK == 0
    row = lambda width: pl.BlockSpec((tm, width), lambda i: (i, 0))
    tab_map_t = lambda i: tab_map(i)[::-1]
    if q_transposed:
        wuq, wuk = w["w_uq_t"], w["w_uk"]
        q_spec = pl.BlockSpec((MLA_HEADS, QK_CAT, tm), lambda i: (0, 0, i))
        q_shape = jax.ShapeDtypeStruct((MLA_HEADS, QK_CAT, rows), qdtype)
        extra_specs = [pl.BlockSpec((KV_RANK, tm), lambda i: (0, i))]
        extra_shapes = [jax.ShapeDtypeStruct((KV_RANK, rows), BF16)]
    else:
        wuq, wuk = w["w_uq"], w["w_uk_t"]
        q_spec = pl.BlockSpec((MLA_HEADS, tm, QK_CAT), lambda i: (0, i, 0))
        q_shape = jax.ShapeDtypeStruct((MLA_HEADS, rows, QK_CAT), qdtype)
        extra_specs, extra_shapes = [], []
    return pl.pallas_call(
        functools.partial(_pre_kernel, q_transposed=q_transposed),
        grid=(rows // tm,),
        in_specs=[
            row(D_MODEL),
            pl.BlockSpec((tm, ROPE_PAD), tab_map),
            pl.BlockSpec((tm, ROPE_PAD), tab_map),
            pl.BlockSpec((ROPE_PAD, tm), tab_map_t),
            pl.BlockSpec((ROPE_PAD, tm), tab_map_t),
            _const_spec((1, D_MODEL)),
            _const_spec((D_MODEL, IN_PAD)),
            _const_spec((1, Q_RANK)),
            _const_spec(wuq.shape),
            _const_spec(wuk.shape),
            _const_spec((1, KV_RANK)),
        ],
        out_specs=[row(S5_W), pl.BlockSpec((S5_NOCT, S5_CHUNK, tm // S5_CHUNK, LANE), lambda i: (0, 0, i, 0)),
                   q_spec, row(QK_CAT), row(KV_RANK), row(QK_ROPE)] + extra_specs,
        out_shape=[
            jax.ShapeDtypeStruct((rows, S5_W), F32),
            jax.ShapeDtypeStruct((S5_NOCT, S5_CHUNK, rows // S5_CHUNK, LANE), BF16),
            q_shape,
            jax.ShapeDtypeStruct((rows, QK_CAT), qdtype),
            jax.ShapeDtypeStruct((rows, KV_RANK), F32),
            jax.ShapeDtypeStruct((rows, QK_ROPE), F32),
        ] + extra_shapes,
        scratch_shapes=[pltpu.VMEM((S5_NOCT, tm, LANE), F32)],
        compiler_params=_params("arbitrary"),
        name="pre",
    )(x2d, cos, sin, cos.T, sin.T, w["g_mix"], w["w_in"], w["g_q"], wuq, wuk, w["g_kv"])


def _chunk_inputs(ut_ref):
    return jnp.concatenate([ut_ref[s] for s in range(S5_CHUNK)], axis=1)


def _s5_state_in_kernel(ut_ref, wre_ref, wim_ref, sre_ref, sim_ref):
    u = _chunk_inputs(ut_ref)
    sre_ref[...] = _dot(u, wre_ref[0])
    sim_ref[...] = _dot(u, wim_ref[0])


def _s5_scan_kernel(sre_ref, sim_ref, h0re_ref, h0im_ref, are_ref, aim_ref, hre_ref, him_ref, fre_ref, fim_ref):
    nb, nc, width = sre_ref.shape
    ar = are_ref[...].reshape(1, 1, width)
    ai = aim_ref[...].reshape(1, 1, width)

    def body(c, carry):
        hr, hi = carry
        hre_ref[:, pl.ds(c, 1), :] = hr
        him_ref[:, pl.ds(c, 1), :] = hi
        sr = sre_ref[:, pl.ds(c, 1), :]
        si = sim_ref[:, pl.ds(c, 1), :]
        return ar * hr - ai * hi + sr, ar * hi + ai * hr + si

    hr, hi = lax.fori_loop(0, nc, body, (h0re_ref[...], h0im_ref[...]))
    fre_ref[...] = hr
    fim_ref[...] = hi


def _s5_out_kernel(ut_ref, hre_ref, him_ref, t_ref, mre_ref, mim_ref, yt_ref):
    y = (_dot(_chunk_inputs(ut_ref), t_ref[0]) + _dot(hre_ref[...].astype(BF16), mre_ref[0])
         + _dot(him_ref[...].astype(BF16), mim_ref[0]))
    for t in range(S5_CHUNK):
        yt_ref[t] = y[:, t * LANE:(t + 1) * LANE]


def _s5_mixer(ut, h0_re, h0_im, ops, nb):
    rows = ut.shape[2]
    nc = rows // nb
    tr = min(rows, 1024)
    assert rows % tr == 0
    ow = S5_OCT * S5_STATE
    cw = S5_CHUNK * LANE
    ut_spec = pl.BlockSpec((None, S5_CHUNK, tr, LANE), lambda q, r: (q, 0, r, 0))
    st_spec = pl.BlockSpec((tr, ow), lambda q, r: (r, q))
    oct_spec = lambda a, b: pl.BlockSpec((1, a, b), lambda q, r: (q, 0, 0))
    s_re, s_im = pl.pallas_call(
        _s5_state_in_kernel,
        grid=(S5_NOCT, rows // tr),
        in_specs=[ut_spec, oct_spec(cw, ow), oct_spec(cw, ow)],
        out_specs=[st_spec, st_spec],
        out_shape=[jax.ShapeDtypeStruct((rows, S5_STATE_W), F32)] * 2,
        compiler_params=_params("arbitrary", "arbitrary"),
        name="s5_state_in",
    )(ut, ops["w_re"], ops["w_im"])

    blk = pl.BlockSpec((nb, nc, S5_SCAN_W), lambda j: (0, 0, j))
    one = pl.BlockSpec((nb, 1, S5_SCAN_W), lambda j: (0, 0, j))
    vec = pl.BlockSpec((1, S5_SCAN_W), lambda j: (0, j))
    h_re, h_im, f_re, f_im = pl.pallas_call(
        _s5_scan_kernel,
        grid=(S5_STATE_W // S5_SCAN_W,),
        in_specs=[blk, blk, one, one, vec, vec],
        out_specs=[blk, blk, one, one],
        out_shape=[jax.ShapeDtypeStruct((nb, nc, S5_STATE_W), F32)] * 2
                  + [jax.ShapeDtypeStruct((nb, 1, S5_STATE_W), F32)] * 2,
        compiler_params=_params("arbitrary"),
        name="s5_scan",
    )(s_re.reshape(nb, nc, S5_STATE_W), s_im.reshape(nb, nc, S5_STATE_W), h0_re, h0_im, ops["a_re"], ops["a_im"])

    yt = pl.pallas_call(
        _s5_out_kernel,
        grid=(S5_NOCT, rows // tr),
        in_specs=[ut_spec, st_spec, st_spec, oct_spec(cw, cw), oct_spec(ow, cw), oct_spec(ow, cw)],
        out_specs=ut_spec,
        out_shape=jax.ShapeDtypeStruct(ut.shape, F32),
        compiler_params=_params("arbitrary", "arbitrary"),
        name="s5_out",
    )(ut, h_re.reshape(rows, S5_STATE_W), h_im.reshape(rows, S5_STATE_W), ops["t"], ops["m_re"], ops["m_im"])
    return yt, f_re, f_im


def _s5_operator_kernel(btr_ref, bti_ref, cr_ref, ci_ref, pl_re, pl_im, pr_re, pr_im,
                        t_ref, wre_ref, wim_ref, mre_ref, mim_ref):
    btr, bti = btr_ref[0], bti_ref[0]
    cr, ci = cr_ref[0], ci_ref[0]
    hi_dot = lambda a, b: jnp.dot(a, b, precision=HI, preferred_element_type=F32)
    kerns = []
    for k in range(S5_CHUNK + 1):
        ar, ai = pr_re[0, :, k:k + 1], pr_im[0, :, k:k + 1]
        cpr = cr * ar - ci * ai
        cpi = cr * ai + ci * ar
        if k < S5_CHUNK:
            kerns.append((hi_dot(btr, cpr) - hi_dot(bti, cpi)).astype(BF16))
            s = S5_CHUNK - 1 - k
            lr, li = pl_re[0, k:k + 1, :], pl_im[0, k:k + 1, :]
            wre_ref[0, s * LANE:(s + 1) * LANE, :] = (btr * lr - bti * li).astype(BF16)
            wim_ref[0, s * LANE:(s + 1) * LANE, :] = (btr * li + bti * lr).astype(BF16)
        if k >= 1:
            t = k - 1
            mre_ref[0, :, t * LANE:(t + 1) * LANE] = cpr.astype(BF16)
            mim_ref[0, :, t * LANE:(t + 1) * LANE] = (-cpi).astype(BF16)
    zero = jnp.zeros((LANE, LANE), BF16)
    for s in range(S5_CHUNK):
        for t in range(S5_CHUNK):
            t_ref[0, s * LANE:(s + 1) * LANE, t * LANE:(t + 1) * LANE] = kerns[t - s] if t >= s else zero


def _s5_operators(a_re, a_im, log_dt, b_re, b_im, c_re, c_im):
    steps = S5_CHUNK
    dt = jnp.exp(log_dt)[:, None]
    k = jnp.arange(steps + 1, dtype=F32)[:, None, None]
    mag = jnp.exp(k * (dt * a_re)[None])
    pw_r = mag * jnp.cos(k * (dt * a_im)[None])
    pw_i = mag * jnp.sin(k * (dt * a_im)[None])
    abr, abi = pw_r[1], pw_i[1]
    num_re, num_im = abr - 1.0, abi
    den = a_re * a_re + a_im * a_im
    f_re = (num_re * a_re + num_im * a_im) / den
    f_im = (num_im * a_re - num_re * a_im) / den
    bbr = f_re[..., None] * b_re - f_im[..., None] * b_im
    bbi = f_re[..., None] * b_im + f_im[..., None] * b_re
    same_group = jnp.arange(S5_OCT)[:, None, None, None] == jnp.arange(S5_OCT)[None, None, :, None]
    ow = S5_OCT * S5_STATE

    def block_diag(a):
        a = a.reshape(S5_NOCT, S5_OCT, a.shape[1], 1, a.shape[2])
        return jnp.where(same_group, a, 0.0).reshape(S5_NOCT, S5_OCT * a.shape[2], S5_OCT * a.shape[4])

    pad_k = S5_POW_PAD - (steps + 1)
    lanes = lambda p: jnp.pad(p.reshape(steps + 1, S5_NOCT, ow).transpose(1, 0, 2), ((0, 0), (0, pad_k), (0, 0)))
    rows = lambda p: jnp.pad(p.reshape(steps + 1, S5_NOCT, ow).transpose(1, 2, 0), ((0, 0), (0, 0), (0, pad_k)))
    bt_re, bt_im = block_diag(bbr.transpose(0, 2, 1)), block_diag(bbi.transpose(0, 2, 1))
    cb_re, cb_im = block_diag(c_re.transpose(0, 2, 1)), block_diag(c_im.transpose(0, 2, 1))
    cw = steps * LANE
    oct_spec = lambda a, b: pl.BlockSpec((1, a, b), lambda q: (q, 0, 0))
    t_mat, w_re, w_im, m_re, m_im = pl.pallas_call(
        _s5_operator_kernel,
        grid=(S5_NOCT,),
        in_specs=[oct_spec(LANE, ow), oct_spec(LANE, ow), oct_spec(ow, LANE), oct_spec(ow, LANE),
                  oct_spec(S5_POW_PAD, ow), oct_spec(S5_POW_PAD, ow), oct_spec(ow, S5_POW_PAD),
                  oct_spec(ow, S5_POW_PAD)],
        out_specs=[oct_spec(cw, cw), oct_spec(cw, ow), oct_spec(cw, ow), oct_spec(ow, cw), oct_spec(ow, cw)],
        out_shape=[jax.ShapeDtypeStruct((S5_NOCT, cw, cw), BF16)]
                  + [jax.ShapeDtypeStruct((S5_NOCT, cw, ow), BF16)] * 2
                  + [jax.ShapeDtypeStruct((S5_NOCT, ow, cw), BF16)] * 2,
        compiler_params=_params("arbitrary"),
        name="s5_operators",
    )(bt_re, bt_im, cb_re, cb_im, lanes(pw_r), lanes(pw_i), rows(pw_r), rows(pw_i))
    return {
        "t": t_mat, "w_re": w_re, "w_im": w_im, "m_re": m_re, "m_im": m_im,
        "a_re": pw_r[steps].reshape(1, S5_STATE_W), "a_im": pw_i[steps].reshape(1, S5_STATE_W),
    }


def _softmax_update_t(s, vt, m_ref, l_ref, acc_ref):
    m_prev = m_ref[...]
    m_new = jnp.maximum(m_prev, jnp.max(s, axis=0, keepdims=True))
    alpha = jnp.exp2(m_prev - m_new)
    p = jnp.exp2(s - m_new)
    l_ref[...] = alpha * l_ref[...] + jnp.sum(p, axis=0, keepdims=True)
    acc_ref[...] = alpha * acc_ref[...] + _dot(vt, p.astype(BF16))
    m_ref[...] = m_new


def _attn_prompt_kernel(qt_ref, k_ref, vt_ref, kmeta_ref, vmeta_t_ref, wuvt_ref, o_ref, m_ref, l_ref, acc_ref, s_ref,
                        *, tq):
    qi = pl.program_id(1)
    qt = jnp.concatenate([qt_ref[h] for h in range(MLA_HEADS)], axis=1)

    def keys(j):
        return pl.ds(pl.multiple_of(j * tq, tq), tq)

    def scores(j):
        return _dot(k_ref[0, keys(j), :], qt)

    def update(s, j):
        _softmax_update_t(s, vt_ref[:, keys(j)], m_ref, l_ref, acc_ref)

    s = _dot(kmeta_ref[...], qt)
    m0 = jnp.max(s, axis=0, keepdims=True)
    p = jnp.exp2(s - m0)
    m_ref[...] = m0
    l_ref[...] = jnp.sum(p, axis=0, keepdims=True)
    acc_ref[...] = _dot(vmeta_t_ref[...], p.astype(BF16))

    s_ref[...] = scores(0)

    def pair_body(kp, carry):
        j = 2 * kp
        s_b = scores(j + 1)
        s_next = scores(j + 2)
        update(s_ref[...], j)
        update(s_b, j + 1)
        s_ref[...] = s_next
        return carry

    lax.fori_loop(0, qi // 2, pair_body, 0)

    @pl.when(qi % 2 == 1)
    def _():
        s_next = scores(qi)
        update(s_ref[...], qi - 1)
        s_ref[...] = s_next

    s = s_ref[...]
    t_k = lax.broadcasted_iota(jnp.int32, s.shape, 0)
    t_q = lax.broadcasted_iota(jnp.int32, s.shape, 1) % tq
    s = jnp.where(t_k <= t_q, s, NEG_INF)
    update(s, qi)

    ot = (acc_ref[...] / l_ref[...]).astype(BF16)
    for h in range(MLA_HEADS):
        yt = _dot(wuvt_ref[h], ot[:, h * tq:(h + 1) * tq])
        o_ref[:, h * V_HEAD:(h + 1) * V_HEAD] = yt.T.astype(o_ref.dtype)


def _attn_prompt(qt, kcat, vt, kmeta, vmeta_t, w_uv_t, nb, tq):
    t = kcat.shape[1]
    nq = t // tq
    cols = MLA_HEADS * tq
    return pl.pallas_call(
        functools.partial(_attn_prompt_kernel, tq=tq),
        grid=(nb, nq),
        in_specs=[
            pl.BlockSpec((MLA_HEADS, QK_CAT, tq), lambda b, i: (0, 0, b * nq + i)),
            pl.BlockSpec((1, t, QK_CAT), lambda b, i: (b, 0, 0)),
            pl.BlockSpec((KV_RANK, t), lambda b, i: (0, b)),
            _const_spec(kmeta.shape),
            _const_spec(vmeta_t.shape),
            _const_spec(w_uv_t.shape),
        ],
        out_specs=pl.BlockSpec((tq, MLA_HEADS * V_HEAD), lambda b, i: (b * nq + i, 0)),
        out_shape=jax.ShapeDtypeStruct((nb * t, MLA_HEADS * V_HEAD), BF16),
        scratch_shapes=[pltpu.VMEM((1, cols), F32), pltpu.VMEM((1, cols), F32), pltpu.VMEM((KV_RANK, cols), F32),
                        pltpu.VMEM((tq, cols), F32)],
        compiler_params=_params("arbitrary", "arbitrary"),
        name="attn_prompt",
    )(qt, kcat, vt, kmeta, vmeta_t, w_uv_t)


def _attn_meta_kernel(q_ref, kmeta_ref, wuv_ref, o_ref):
    rows = MLA_HEADS * N_META
    q = q_ref[...].reshape(rows, QK_CAT)
    km = kmeta_ref[...]
    s = _dot_nt(q, km)
    t_q = lax.broadcasted_iota(jnp.int32, s.shape, 0) % N_META
    t_k = lax.broadcasted_iota(jnp.int32, s.shape, 1)
    s = jnp.where(t_k <= t_q, s, NEG_INF)
    p = jnp.exp(s - jnp.max(s, axis=1, keepdims=True))
    o = _dot(p.astype(BF16), km[:, :KV_RANK]) / jnp.sum(p, axis=1, keepdims=True)
    o = o.astype(BF16)
    for h in range(MLA_HEADS):
        o_ref[:, h * V_HEAD:(h + 1) * V_HEAD] = _dot(o[h * N_META:(h + 1) * N_META], wuv_ref[h]).astype(o_ref.dtype)


def _attn_meta(q, kmeta, w_uv):
    return pl.pallas_call(
        _attn_meta_kernel,
        out_shape=jax.ShapeDtypeStruct((N_META, MLA_HEADS * V_HEAD), BF16),
        name="attn_meta",
    )(q, kmeta, w_uv)


def _attn_sample_kernel(pt_ref, q_ref, knew_ref, wuv_ref, ckv_hbm, kr_hbm, o_ref, ckv_buf, kr_buf, sem,
                        *, n_pages, steps):
    b = pl.program_id(0)
    slot = b % 2

    def page_copies(seq, into, j):
        page = pt_ref[seq, j]
        return (pltpu.make_async_copy(ckv_hbm.at[0, page], ckv_buf.at[into, j], sem.at[into, 0]),
                pltpu.make_async_copy(kr_hbm.at[0, page], kr_buf.at[into, j], sem.at[into, 1]))

    def request_pages(seq, into):
        def body(i, carry):
            for parity in range(2):
                ckv_copy, kr_copy = page_copies(seq, into, 2 * i + parity)
                ckv_copy.start(priority=parity)
                kr_copy.start(priority=1 - parity)
            return carry
        lax.fori_loop(0, n_pages // 2, body, 0, unroll=2)

    @pl.when(b == 0)
    def _():
        request_pages(0, 0)

    @pl.when(b + 1 < pl.num_programs(0))
    def _():
        request_pages(b + 1, 1 - slot)

    pltpu.make_async_copy(ckv_hbm.at[0, pl.ds(0, n_pages)], ckv_buf.at[slot], sem.at[slot, 0]).wait()
    pltpu.make_async_copy(kr_hbm.at[0, pl.ds(0, n_pages)], kr_buf.at[slot], sem.at[slot, 1]).wait()

    rows = MLA_HEADS * steps
    q = q_ref[...].reshape(rows, QK_CAT).astype(BF16)
    q_lat = q[:, :KV_RANK]
    q_rope = q[:, KV_RANK:KV_RANK + QK_ROPE]
    groups = range(0, n_pages, PAGE_GROUP)

    def latent_keys(j0):
        return jnp.concatenate([ckv_buf[slot, j].astype(BF16) for j in range(j0, j0 + PAGE_GROUP)], axis=0)

    def chunk_scores(c):
        parts = []
        for j0 in groups[c * CHUNK_GROUPS:(c + 1) * CHUNK_GROUPS]:
            kr_t = jnp.concatenate([kr_buf[slot, j].astype(BF16) for j in range(j0, j0 + PAGE_GROUP)], axis=1)
            parts.append(_dot_nt(q_lat, latent_keys(j0)) + _dot(q_rope, kr_t))
        return jnp.concatenate(parts, axis=1)

    kn = jnp.concatenate([knew_ref[...], jnp.zeros((PAGE_SIZE - steps, QK_CAT), F32)], axis=0).astype(BF16)
    s_new = _dot_nt(q, kn)
    t_q = lax.broadcasted_iota(jnp.int32, s_new.shape, 0) % steps
    t_k = lax.broadcasted_iota(jnp.int32, s_new.shape, 1)
    s_new = jnp.where(t_k <= t_q, s_new, NEG_INF)

    gk = PAGE_GROUP * PAGE_SIZE
    n_chunks = len(groups) // CHUNK_GROUPS
    m = jnp.full((rows, 1), NEG_INF, F32)
    l = jnp.zeros((rows, 1), F32)
    acc = jnp.zeros((rows, KV_RANK), F32)
    s_cur = chunk_scores(0)
    for c in range(n_chunks + 1):
        s_next = chunk_scores(c + 1) if c + 1 < n_chunks else s_new
        m_new = jnp.maximum(m, jnp.max(s_cur, axis=1, keepdims=True))
        alpha = jnp.exp(m - m_new)
        p = jnp.exp(s_cur - m_new)
        l = alpha * l + jnp.sum(p, axis=1, keepdims=True)
        p = p.astype(BF16)
        if c < n_chunks:
            chunk_groups = groups[c * CHUNK_GROUPS:(c + 1) * CHUNK_GROUPS]
            pv = _dot(p[:, :gk], latent_keys(chunk_groups[0]))
            for i, j0 in enumerate(chunk_groups[1:], start=1):
                pv = pv + _dot(p[:, i * gk:(i + 1) * gk], latent_keys(j0))
        else:
            pv = _dot(p, kn[:, :KV_RANK])
        acc = alpha * acc + pv
        m = m_new
        s_cur = s_next
    o = (acc / l).astype(BF16)
    for h in range(MLA_HEADS):
        o_ref[:, h * V_HEAD:(h + 1) * V_HEAD] = _dot(o[h * steps:(h + 1) * steps], wuv_ref[h])


def _attn_sample(page_table, q, knew, cache_ckv, cache_kr_t, w_uv, steps):
    nb, n_pages = page_table.shape
    assert n_pages % (PAGE_GROUP * CHUNK_GROUPS) == 0
    grid_spec = pltpu.PrefetchScalarGridSpec(
        num_scalar_prefetch=1,
        grid=(nb,),
        in_specs=[pl.BlockSpec((MLA_HEADS, steps, QK_CAT), lambda b, pt: (0, b, 0)),
                  pl.BlockSpec((steps, QK_CAT), lambda b, pt: (b, 0)),
                  pl.BlockSpec(w_uv.shape, lambda b, pt: (0, 0, 0)),
                  pl.BlockSpec(memory_space=pl.ANY),
                  pl.BlockSpec(memory_space=pl.ANY)],
        out_specs=pl.BlockSpec((steps, MLA_HEADS * V_HEAD), lambda b, pt: (b, 0)),
        scratch_shapes=[pltpu.VMEM((2, n_pages, PAGE_SIZE, KV_RANK), F32),
                        pltpu.VMEM((2, n_pages, QK_ROPE, PAGE_SIZE), F32),
                        pltpu.SemaphoreType.DMA((2, 2))],
    )
    return pl.pallas_call(
        functools.partial(_attn_sample_kernel, n_pages=n_pages, steps=steps),
        grid_spec=grid_spec,
        out_shape=jax.ShapeDtypeStruct((nb * steps, MLA_HEADS * V_HEAD), F32),
        compiler_params=_params("arbitrary"),
        name="attn_sample",
    )(page_table, q, knew, w_uv, cache_ckv, cache_kr_t)


def _gelu_tanh(x):
    return 0.5 * x * (1.0 + jnp.tanh(math.sqrt(2.0 / math.pi) * (x + 0.044715 * (x * x * x))))


def _sigmoid(x):
    return 1.0 / (1.0 + jnp.exp(-x))


def _post_kernel(x_ref, yt_ref, u_ref, ymla_ref, p1_ref, p2_ref, d_ref, wglu_ref, wout_ref, gffn_ref,
                 wgate_ref, wup_ref, convw_ref, convb_ref, wdown_ref, gfin_ref,
                 y_ref, gate_ref, ybuf, hbuf, carry_ref, *, tm, seq_steps):
    long_seq = seq_steps == 0
    for o in range(S5_NOCT):
        for t in range(S5_CHUNK):
            ybuf[o, pl.ds(t, tm // S5_CHUNK, stride=S5_CHUNK), :] = yt_ref[o, t]
    y_ssm = jnp.concatenate([ybuf[o] for o in range(S5_NOCT)], axis=1)
    y = y_ssm + d_ref[...] * u_ref[...]
    y = _gelu_tanh(y)
    y = y * _sigmoid(_dot(y.astype(BF16), wglu_ref[...]))
    mixed = _dot(y.astype(BF16), wout_ref[:S5_W, :]) + _dot(ymla_ref[...].astype(BF16), wout_ref[S5_W:, :])
    x1 = x_ref[...] + mixed
    xn = _rms(x1, gffn_ref[...]).astype(BF16)

    row = lax.broadcasted_iota(jnp.int32, (tm, FF_CHUNK), 0)
    if long_seq:
        @pl.when(pl.program_id(1) == 0)
        def _():
            carry_ref[...] = p1_ref[...]
        t_in = row
    else:
        t_in = row % seq_steps

    for c in range(D_FF // FF_CHUNK):
        sl = slice(c * FF_CHUNK, (c + 1) * FF_CHUNK)
        gate = _dot(xn, wgate_ref[:, sl])
        up = _dot(xn, wup_ref[:, sl])
        if long_seq:
            back1 = jnp.broadcast_to(carry_ref[7:8, sl], gate.shape)
            back2 = jnp.where(t_in == 0, carry_ref[6:7, sl], back1)
            carry_ref[:, sl] = gate[tm - 8:, :]
            gate_ref[:, sl] = gate[tm - 8:, :]
        else:
            back1 = p1_ref[:, sl]
            back2 = p2_ref[:, sl]
            gate_ref[:, sl] = gate
        prev1 = jnp.where(t_in >= 1, pltpu.roll(gate, 1, 0), back1)
        prev2 = jnp.where(t_in >= 2, pltpu.roll(gate, 2, 0), back2)
        conv = (convb_ref[:, sl] + convw_ref[0:1, sl] * prev2 + convw_ref[1:2, sl] * prev1
                + convw_ref[2:3, sl] * gate)
        hbuf[:, sl] = (conv * _sigmoid(conv) * up).astype(BF16)

    x2 = x1 + _dot(hbuf[...], wdown_ref[...])
    y_ref[...] = _rms(x2, gfin_ref[...])


def _post_call(x2d, yt, u, y_mla, p1, p2, w, nb, tm, seq_steps):
    rows = x2d.shape[0]
    nt = rows // (nb * tm)
    assert nb * nt * tm == rows and tm % S5_CHUNK == 0
    long_seq = seq_steps == 0
    row = lambda width: pl.BlockSpec((tm, width), lambda b, i: (b * nt + i, 0))
    yt_spec = pl.BlockSpec((S5_NOCT, S5_CHUNK, tm // S5_CHUNK, LANE), lambda b, i: (0, 0, b * nt + i, 0))
    if long_seq:
        hist = [_const_spec((8, D_FF)), _const_spec((8, D_FF))]
        gate_spec = pl.BlockSpec((None, 8, D_FF), lambda b, i: (b, 0, 0))
        gate_shape = jax.ShapeDtypeStruct((nb, 8, D_FF), F32)
    else:
        hist = [row(D_FF), row(D_FF)]
        gate_spec = row(D_FF)
        gate_shape = jax.ShapeDtypeStruct((rows, D_FF), F32)
    return pl.pallas_call(
        functools.partial(_post_kernel, tm=tm, seq_steps=seq_steps),
        grid=(nb, nt),
        in_specs=[row(D_MODEL), yt_spec, row(S5_W), row(MLA_HEADS * V_HEAD)] + hist + [
            _const_spec((1, S5_W)), _const_spec((S5_W, S5_W)), _const_spec((D_MODEL, D_MODEL)),
            _const_spec((1, D_MODEL)), _const_spec((D_MODEL, D_FF)), _const_spec((D_MODEL, D_FF)),
            _const_spec((CONV_W, D_FF)), _const_spec((1, D_FF)), _const_spec((D_FF, D_MODEL)),
            _const_spec((1, D_MODEL))],
        out_specs=[row(D_MODEL), gate_spec],
        out_shape=[jax.ShapeDtypeStruct((rows, D_MODEL), F32), gate_shape],
        scratch_shapes=[pltpu.VMEM((S5_NOCT, tm, LANE), F32), pltpu.VMEM((tm, D_FF), BF16),
                        pltpu.VMEM((8, D_FF), F32)],
        compiler_params=_params("arbitrary", "arbitrary"),
        name="post",
    )(x2d, yt, u, y_mla, p1, p2, w["s5_d"], w["w_glu"], w["w_out"], w["g_ffn"], w["w_gate"], w["w_up"],
      w["conv_w"], w["conv_b"], w["w_down"], w["g_final"])


def _rope_tables(pos):
    half = QK_ROPE // 2
    inv = ROPE_BASE ** (-jnp.arange(half, dtype=F32) / half)
    ang = pos.astype(F32)[:, None] * inv[None, :]
    cos, sin = jnp.cos(ang), jnp.sin(ang)
    pad = jnp.zeros((pos.shape[0], ROPE_PAD - QK_ROPE), F32)
    return jnp.concatenate([cos, cos, pad], axis=1), jnp.concatenate([-sin, sin, pad], axis=1)


def kernel(x_prompt, x_sample, cache_ckv, cache_kr, state_s5_re, state_s5_im, state_conv, page_table, meta_tokens, g_mix, w_in, g_q, w_uq, g_kv, w_uk, w_uv, s5_a_re, s5_a_im, s5_log_dt, s5_b_re, s5_b_im, s5_c_re, s5_c_im, s5_d, w_glu, w_out, g_ffn, w_gate, w_up, conv_w, conv_b, w_down, g_final):
    assert w_in.shape[0] == 1, "single-layer step"
    nb, seq, _ = x_prompt.shape
    db, steps, _ = x_sample.shape
    n_past = page_table.shape[1] * PAGE_SIZE
    assert seq % S5_CHUNK == 0 and N_META % S5_CHUNK == 0 and steps == S5_CHUNK

    w_in_pad = jnp.concatenate([w_in[0], jnp.zeros((D_MODEL, IN_PAD - w_in.shape[2]), F32)], axis=1)
    wq = w_uq[0].reshape(Q_RANK, MLA_HEADS, QK_NOPE + QK_ROPE)
    wq_rope = jnp.concatenate([wq[:, :, QK_NOPE:], jnp.zeros((Q_RANK, MLA_HEADS, ROPE_PAD - QK_ROPE), F32)], axis=2)
    wq_perm = jnp.concatenate([wq[:, :, :QK_NOPE].reshape(Q_RANK, -1), wq_rope.reshape(Q_RANK, -1)], axis=1)
    w = {
        "g_mix": g_mix[0].reshape(1, -1), "w_in": w_in_pad.astype(BF16),
        "g_q": g_q[0].reshape(1, -1), "w_uq": wq_perm.astype(BF16), "w_uq_t": wq_perm.T.astype(BF16),
        "w_uk_t": w_uk[0].transpose(1, 2, 0).astype(BF16),
        "w_uk": w_uk[0].transpose(1, 0, 2).astype(BF16),
        "g_kv": g_kv[0].reshape(1, -1),
        "s5_d": s5_d[0].reshape(1, -1), "w_glu": w_glu[0].astype(BF16), "w_out": w_out[0].astype(BF16),
        "g_ffn": g_ffn[0].reshape(1, -1), "w_gate": w_gate[0].astype(BF16), "w_up": w_up[0].astype(BF16),
        "conv_w": conv_w[0], "conv_b": conv_b[0].reshape(1, -1), "w_down": w_down[0].astype(BF16),
        "g_final": g_final.reshape(1, -1),
    }
    w_uv_h = w_uv[0].transpose(1, 0, 2).astype(BF16)
    ops = _s5_operators(s5_a_re[0], s5_a_im[0], s5_log_dt[0], s5_b_re[0], s5_b_im[0], s5_c_re[0], s5_c_im[0])

    cos_m, sin_m = _rope_tables(jnp.arange(N_META, dtype=jnp.int32))
    cos_p, sin_p = _rope_tables(N_META + jnp.arange(seq, dtype=jnp.int32))
    tm_p = 512
    tm_s = 512
    cos_s, sin_s = _rope_tables(n_past + jnp.arange(tm_s, dtype=jnp.int32) % steps)

    tm_pre = 1024
    tiles_per_seq = seq // tm_pre
    u_m, ut_m, q_m, kcat_m, ckv_m, kr_m = _pre_call(meta_tokens, cos_m, sin_m, lambda i: (0, 0), N_META, BF16, w)
    u_p, ut_p, qt_p, kcat_p, ckv_p, kr_p, vt_p = _pre_call(
        x_prompt.reshape(nb * seq, D_MODEL), cos_p, sin_p, lambda i: (i % tiles_per_seq, 0), tm_pre, BF16, w,
        q_transposed=True)
    u_s, ut_s, q_s, kcat_s, ckv_s, kr_s = _pre_call(x_sample.reshape(db * steps, D_MODEL), cos_s, sin_s,
                                                    lambda i: (0, 0), tm_s, F32, w)

    zero_state = jnp.zeros((1, 1, S5_STATE_W), F32)
    yt_m, fm_re, fm_im = _s5_mixer(ut_m, zero_state, zero_state, ops, 1)
    yt_p, s5_re_p, s5_im_p = _s5_mixer(ut_p, jnp.broadcast_to(fm_re, (nb, 1, S5_STATE_W)),
                                       jnp.broadcast_to(fm_im, (nb, 1, S5_STATE_W)), ops, nb)
    yt_s, s5_re_s, s5_im_s = _s5_mixer(ut_s, state_s5_re[0].reshape(db, 1, S5_STATE_W),
                                       state_s5_im[0].reshape(db, 1, S5_STATE_W), ops, db)

    kmeta = jnp.concatenate([kcat_m, jnp.zeros((PAGE_SIZE - N_META, QK_CAT), BF16)], axis=0)
    ymla_m = _attn_meta(q_m, kmeta, w_uv_h)
    ymla_p = _attn_prompt(qt_p, kcat_p.reshape(nb, seq, QK_CAT), vt_p, kcat_m, kcat_m[:, :KV_RANK].T,
                          w_uv[0].transpose(1, 2, 0).astype(BF16), nb, tq=256)
    ymla_s = _attn_sample(page_table, q_s, kcat_s, cache_ckv, jnp.swapaxes(cache_kr, 2, 3), w_uv_h, steps)

    zeros_hist = jnp.zeros((8, D_FF), F32)
    _, gate_m = _post_call(meta_tokens, yt_m, u_m, ymla_m, zeros_hist, zeros_hist, w, 1, N_META, 0)
    y_p, gate_p = _post_call(x_prompt.reshape(nb * seq, D_MODEL), yt_p, u_p, ymla_p, gate_m[0], gate_m[0],
                             w, nb, tm_p, 0)
    sc = state_conv[0]
    back1 = jnp.pad(sc[:, 1:2], ((0, 0), (0, steps - 1), (0, 0))).reshape(db * steps, D_FF)
    back2 = jnp.pad(sc, ((0, 0), (0, steps - 2), (0, 0))).reshape(db * steps, D_FF)
    y_s, gate_s = _post_call(x_sample.reshape(db * steps, D_MODEL), yt_s, u_s, ymla_s, back1, back2,
                             w, 1, tm_s // 2, steps)

    def with_meta(meta_rows, tok_rows, width):
        return jnp.concatenate([jnp.broadcast_to(meta_rows[None], (nb, N_META, width)),
                                tok_rows.reshape(nb, seq, width)], axis=1)[None]

    return (y_p.reshape(nb, seq, D_MODEL), y_s.reshape(db, steps, D_MODEL),
            with_meta(ckv_m, ckv_p, KV_RANK), with_meta(kr_m, kr_p, QK_ROPE),
            s5_re_p.reshape(1, nb, S5_GROUPS, S5_STATE), s5_im_p.reshape(1, nb, S5_GROUPS, S5_STATE),
            gate_p[:, 8 - (CONV_W - 1):][None],
            ckv_s.reshape(1, db, steps, KV_RANK), kr_s.reshape(1, db, steps, QK_ROPE),
            s5_re_s.reshape(1, db, S5_GROUPS, S5_STATE), s5_im_s.reshape(1, db, S5_GROUPS, S5_STATE),
            gate_s.reshape(db, steps, D_FF)[:, steps - (CONV_W - 1):][None])
```

```python
import functools
import math

import jax
import jax.numpy as jnp
from jax import lax
from jax.experimental import pallas as pl
from jax.experimental.pallas import tpu as pltpu

F32 = jnp.float32
BF16 = jnp.bfloat16

D_MODEL = 1024
N_META = 16
S5_W = 512
S5_GROUP = 16
S5_GROUPS = 32
S5_STATE = 64
MLA_HEADS = 4
QK_NOPE = 128
QK_ROPE = 64
V_HEAD = 128
Q_RANK = 384
KV_RANK = 256
D_FF = 2816
CONV_W = 3
ROPE_BASE = 10000.0
EPS = 1e-6
PAGE_SIZE = 128
ATTN_SCALE = 1.0 / math.sqrt(QK_NOPE + QK_ROPE)
LOG2_E = math.log2(math.e)

LANE = 128
ROPE_PAD = LANE
QK_CAT = KV_RANK + ROPE_PAD
IN_PAD = S5_W + Q_RANK + KV_RANK + ROPE_PAD
S5_CHUNK = 8
S5_OCT = LANE // S5_GROUP
S5_NOCT = S5_GROUPS // S5_OCT
S5_STATE_W = S5_GROUPS * S5_STATE
S5_SCAN_W = 512
S5_POW_PAD = 16
FF_CHUNK = 256
PAGE_GROUP = 4
CHUNK_GROUPS = 4
VMEM_LIMIT = 56 * 1024 * 1024
NEG_INF = float("-inf")
HI = lax.Precision.HIGHEST


def _const_spec(shape):
    nd = len(shape)
    return pl.BlockSpec(shape, lambda *_: (0,) * nd, pipeline_mode=pl.Buffered(1))


def _params(*semantics):
    return pltpu.CompilerParams(dimension_semantics=semantics, vmem_limit_bytes=VMEM_LIMIT)


def _rms(x, g):
    return x * lax.rsqrt(jnp.mean(x * x, axis=-1, keepdims=True) + EPS) * g


def _rope_slab(x, cos, sin):
    lane = lax.broadcasted_iota(jnp.int32, x.shape, 1)
    half = QK_ROPE // 2
    swapped = jnp.where(lane < half, pltpu.roll(x, LANE - half, 1), pltpu.roll(x, half, 1))
    return x * cos + swapped * sin


def _dot(a, b):
    return jnp.dot(a, b, preferred_element_type=F32)


def _dot_nt(a, b):
    return lax.dot_general(a, b, (((1,), (1,)), ((), ())), preferred_element_type=F32)


def _pre_kernel(x_ref, cos_ref, sin_ref, cost_ref, sint_ref, gmix_ref, win_ref, gq_ref, wuq_ref, wuk_ref, gkv_ref,
                u_ref, ut_ref, q_ref, kcat_ref, ckv_ref, kr_ref, *rest, q_transposed):
    x = x_ref[...]
    xn = _rms(x, gmix_ref[...]).astype(BF16)
    z = _dot(xn, win_ref[...])
    u_ref[...] = z[:, :S5_W]
    slab_ref = rest[-1]
    chunks = u_ref.shape[0] // S5_CHUNK
    for o in range(S5_NOCT):
        slab_ref[o] = z[:, o * LANE:(o + 1) * LANE]
        for s in range(S5_CHUNK):
            ut_ref[o, s] = slab_ref[o, pl.ds(s, chunks, stride=S5_CHUNK), :].astype(BF16)
    cq = z[:, S5_W:S5_W + Q_RANK]
    ckv_raw = z[:, S5_W + Q_RANK:S5_W + Q_RANK + KV_RANK]
    kr_raw = z[:, S5_W + Q_RANK + KV_RANK:]
    cqn = _rms(cq, gq_ref[...]).astype(BF16)
    nope_w = MLA_HEADS * QK_NOPE
    half = QK_ROPE // 2
    if q_transposed:
        qt = _dot_nt(wuq_ref[...], cqn)
        cost = cost_ref[...]
        sint = sint_ref[...]
        for h in range(MLA_HEADS):
            qn = qt[h * QK_NOPE:(h + 1) * QK_NOPE].astype(BF16)
            q_lat = _dot(wuk_ref[h], qn) * (ATTN_SCALE * LOG2_E)
            xr = qt[nope_w + h * ROPE_PAD:nope_w + (h + 1) * ROPE_PAD]
            swapped = jnp.concatenate([xr[half:QK_ROPE], xr[:half], xr[QK_ROPE:]], axis=0)
            qr = (xr * cost + swapped * sint) * (ATTN_SCALE * LOG2_E)
            q_ref[h, :KV_RANK, :] = q_lat.astype(q_ref.dtype)
            q_ref[h, KV_RANK:, :] = qr.astype(q_ref.dtype)
    else:
        q = _dot(cqn, wuq_ref[...])
        for h in range(MLA_HEADS):
            qn = q[:, h * QK_NOPE:(h + 1) * QK_NOPE].astype(BF16)
            q_lat = _dot(qn, wuk_ref[h]) * ATTN_SCALE
            qr = _rope_slab(q[:, nope_w + h * ROPE_PAD:nope_w + (h + 1) * ROPE_PAD],
                            cos_ref[...], sin_ref[...]) * ATTN_SCALE
            q_ref[h, :, :KV_RANK] = q_lat.astype(q_ref.dtype)
            q_ref[h, :, KV_RANK:] = qr.astype(q_ref.dtype)
    ckv = _rms(ckv_raw, gkv_ref[...])
    kr = _rope_slab(kr_raw, cos_ref[...], sin_ref[...])
    ckv_ref[...] = ckv
    kr_ref[...] = kr[:, :QK_ROPE]
    kcat_ref[:, :KV_RANK] = ckv.astype(kcat_ref.dtype)
    kcat_ref[:, KV_RANK:] = kr.astype(kcat_ref.dtype)
    if q_transposed:
        rest[0][...] = ckv.T.astype(BF16)


def _pre_call(x2d, cos, sin, tab_map, tm, qdtype, w, q_transposed=False):
    rows = x2d.shape[0]
    assert rows % tm == 0 and tm % S5_CHUNK == 0
    row = lambda width: pl.BlockSpec((tm, width), lambda i: (i, 0))
    tab_map_t = lambda i: tab_map(i)[::-1]
    if q_transposed:
        wuq, wuk = w["w_uq_t"], w["w_uk"]
        q_spec = pl.BlockSpec((MLA_HEADS, QK_CAT, tm), lambda i: (0, 0, i))
        q_shape = jax.ShapeDtypeStruct((MLA_HEADS, QK_CAT, rows), qdtype)
        extra_specs = [pl.BlockSpec((KV_RANK, tm), lambda i: (0, i))]
        extra_shapes = [jax.ShapeDtypeStruct((KV_RANK, rows), BF16)]
    else:
        wuq, wuk = w["w_uq"], w["w_uk_t"]
        q_spec = pl.BlockSpec((MLA_HEADS, tm, QK_CAT), lambda i: (0, i, 0))
        q_shape = jax.ShapeDtypeStruct((MLA_HEADS, rows, QK_CAT), qdtype)
        extra_specs, extra_shapes = [], []
    return pl.pallas_call(
        functools.partial(_pre_kernel, q_transposed=q_transposed),
        grid=(rows // tm,),
        in_specs=[
            row(D_MODEL),
            pl.BlockSpec((tm, ROPE_PAD), tab_map),
            pl.BlockSpec((tm, ROPE_PAD), tab_map),
            pl.BlockSpec((ROPE_PAD, tm), tab_map_t),
            pl.BlockSpec((ROPE_PAD, tm), tab_map_t),
            _const_spec((1, D_MODEL)),
            _const_spec((D_MODEL, IN_PAD)),
            _const_spec((1, Q_RANK)),
            _const_spec(wuq.shape),
            _const_spec(wuk.shape),
            _const_spec((1, KV_RANK)),
        ],
        out_specs=[row(S5_W), pl.BlockSpec((S5_NOCT, S5_CHUNK, tm // S5_CHUNK, LANE), lambda i: (0, 0, i, 0)),
                   q_spec, row(QK_CAT), row(KV_RANK), row(QK_ROPE)] + extra_specs,
        out_shape=[
            jax.ShapeDtypeStruct((rows, S5_W), F32),
            jax.ShapeDtypeStruct((S5_NOCT, S5_CHUNK, rows // S5_CHUNK, LANE), BF16),
            q_shape,
            jax.ShapeDtypeStruct((rows, QK_CAT), qdtype),
            jax.ShapeDtypeStruct((rows, KV_RANK), F32),
            jax.ShapeDtypeStruct((rows, QK_ROPE), F32),
        ] + extra_shapes,
        scratch_shapes=[pltpu.VMEM((S5_NOCT, tm, LANE), F32)],
        compiler_params=_params("arbitrary"),
        name="pre",
    )(x2d, cos, sin, cos.T, sin.T, w["g_mix"], w["w_in"], w["g_q"], wuq, wuk, w["g_kv"])


def _chunk_inputs(ut_ref):
    return jnp.concatenate([ut_ref[s] for s in range(S5_CHUNK)], axis=1)


def _s5_state_in_kernel(ut_ref, wre_ref, wim_ref, sre_ref, sim_ref):
    u = _chunk_inputs(ut_ref)
    sre_ref[...] = _dot(u, wre_ref[0])
    sim_ref[...] = _dot(u, wim_ref[0])


def _s5_scan_kernel(sre_ref, sim_ref, h0re_ref, h0im_ref, are_ref, aim_ref, hre_ref, him_ref, fre_ref, fim_ref):
    nb, nc, width = sre_ref.shape
    ar = are_ref[...].reshape(1, 1, width)
    ai = aim_ref[...].reshape(1, 1, width)

    def body(c, carry):
        hr, hi = carry
        hre_ref[:, pl.ds(c, 1), :] = hr
        him_ref[:, pl.ds(c, 1), :] = hi
        sr = sre_ref[:, pl.ds(c, 1), :]
        si = sim_ref[:, pl.ds(c, 1), :]
        return ar * hr - ai * hi + sr, ar * hi + ai * hr + si

    hr, hi = lax.fori_loop(0, nc, body, (h0re_ref[...], h0im_ref[...]))
    fre_ref[...] = hr
    fim_ref[...] = hi


def _s5_out_kernel(ut_ref, hre_ref, him_ref, t_ref, mre_ref, mim_ref, yt_ref):
    y = (_dot(_chunk_inputs(ut_ref), t_ref[0]) + _dot(hre_ref[...].astype(BF16), mre_ref[0])
         + _dot(him_ref[...].astype(BF16), mim_ref[0]))
    for t in range(S5_CHUNK):
        yt_ref[t] = y[:, t * LANE:(t + 1) * LANE]


def _s5_mixer(ut, h0_re, h0_im, ops, nb):
    rows = ut.shape[2]
    nc = rows // nb
    tr = min(rows, 1024)
    assert rows % tr == 0
    ow = S5_OCT * S5_STATE
    cw = S5_CHUNK * LANE
    ut_spec = pl.BlockSpec((None, S5_CHUNK, tr, LANE), lambda q, r: (q, 0, r, 0))
    st_spec = pl.BlockSpec((tr, ow), lambda q, r: (r, q))
    oct_spec = lambda a, b: pl.BlockSpec((1, a, b), lambda q, r: (q, 0, 0))
    s_re, s_im = pl.pallas_call(
        _s5_state_in_kernel,
        grid=(S5_NOCT, rows // tr),
        in_specs=[ut_spec, oct_spec(cw, ow), oct_spec(cw, ow)],
        out_specs=[st_spec, st_spec],
        out_shape=[jax.ShapeDtypeStruct((rows, S5_STATE_W), F32)] * 2,
        compiler_params=_params("arbitrary", "arbitrary"),
        name="s5_state_in",
    )(ut, ops["w_re"], ops["w_im"])

    blk = pl.BlockSpec((nb, nc, S5_SCAN_W), lambda j: (0, 0, j))
    one = pl.BlockSpec((nb, 1, S5_SCAN_W), lambda j: (0, 0, j))
    vec = pl.BlockSpec((1, S5_SCAN_W), lambda j: (0, j))
    h_re, h_im, f_re, f_im = pl.pallas_call(
        _s5_scan_kernel,
        grid=(S5_STATE_W // S5_SCAN_W,),
        in_specs=[blk, blk, one, one, vec, vec],
        out_specs=[blk, blk, one, one],
        out_shape=[jax.ShapeDtypeStruct((nb, nc, S5_STATE_W), F32)] * 2
                  + [jax.ShapeDtypeStruct((nb, 1, S5_STATE_W), F32)] * 2,
        compiler_params=_params("arbitrary"),
        name="s5_scan",
    )(s_re.reshape(nb, nc, S5_STATE_W), s_im.reshape(nb, nc, S5_STATE_W), h0_re, h0_im, ops["a_re"], ops["a_im"])

    yt = pl.pallas_call(
        _s5_out_kernel,
        grid=(S5_NOCT, rows // tr),
        in_specs=[ut_spec, st_spec, st_spec, oct_spec(cw, cw), oct_spec(ow, cw), oct_spec(ow, cw)],
        out_specs=ut_spec,
        out_shape=jax.ShapeDtypeStruct(ut.shape, F32),
        compiler_params=_params("arbitrary", "arbitrary"),
        name="s5_out",
    )(ut, h_re.reshape(rows, S5_STATE_W), h_im.reshape(rows, S5_STATE_W), ops["t"], ops["m_re"], ops["m_im"])
    return yt, f_re, f_im


def _s5_operator_kernel(btr_ref, bti_ref, cr_ref, ci_ref, pl_re, pl_im, pr_re, pr_im,
                        t_ref, wre_ref, wim_ref, mre_ref, mim_ref):
    btr, bti = btr_ref[0], bti_ref[0]
    cr, ci = cr_ref[0], ci_ref[0]
    hi_dot = lambda a, b: jnp.dot(a, b, precision=HI, preferred_element_type=F32)
    kerns = []
    for k in range(S5_CHUNK + 1):
        ar, ai = pr_re[0, :, k:k + 1], pr_im[0, :, k:k + 1]
        cpr = cr * ar - ci * ai
        cpi = cr * ai + ci * ar
        if k < S5_CHUNK:
            kerns.append((hi_dot(btr, cpr) - hi_dot(bti, cpi)).astype(BF16))
            s = S5_CHUNK - 1 - k
            lr, li = pl_re[0, k:k + 1, :], pl_im[0, k:k + 1, :]
            wre_ref[0, s * LANE:(s + 1) * LANE, :] = (btr * lr - bti * li).astype(BF16)
            wim_ref[0, s * LANE:(s + 1) * LANE, :] = (btr * li + bti * lr).astype(BF16)
        if k >= 1:
            t = k - 1
            mre_ref[0, :, t * LANE:(t + 1) * LANE] = cpr.astype(BF16)
            mim_ref[0, :, t * LANE:(t + 1) * LANE] = (-cpi).astype(BF16)
    zero = jnp.zeros((LANE, LANE), BF16)
    for s in range(S5_CHUNK):
        for t in range(S5_CHUNK):
            t_ref[0, s * LANE:(s + 1) * LANE, t * LANE:(t + 1) * LANE] = kerns[t - s] if t >= s else zero


def _s5_operators(a_re, a_im, log_dt, b_re, b_im, c_re, c_im):
    steps = S5_CHUNK
    dt = jnp.exp(log_dt)[:, None]
    k = jnp.arange(steps + 1, dtype=F32)[:, None, None]
    mag = jnp.exp(k * (dt * a_re)[None])
    pw_r = mag * jnp.cos(k * (dt * a_im)[None])
    pw_i = mag * jnp.sin(k * (dt * a_im)[None])
    abr, abi = pw_r[1], pw_i[1]
    num_re, num_im = abr - 1.0, abi
    den = a_re * a_re + a_im * a_im
    f_re = (num_re * a_re + num_im * a_im) / den
    f_im = (num_im * a_re - num_re * a_im) / den
    bbr = f_re[..., None] * b_re - f_im[..., None] * b_im
    bbi = f_re[..., None] * b_im + f_im[..., None] * b_re
    same_group = jnp.arange(S5_OCT)[:, None, None, None] == jnp.arange(S5_OCT)[None, None, :, None]
    ow = S5_OCT * S5_STATE

    def block_diag(a):
        a = a.reshape(S5_NOCT, S5_OCT, a.shape[1], 1, a.shape[2])
        return jnp.where(same_group, a, 0.0).reshape(S5_NOCT, S5_OCT * a.shape[2], S5_OCT * a.shape[4])

    pad_k = S5_POW_PAD - (steps + 1)
    lanes = lambda p: jnp.pad(p.reshape(steps + 1, S5_NOCT, ow).transpose(1, 0, 2), ((0, 0), (0, pad_k), (0, 0)))
    rows = lambda p: jnp.pad(p.reshape(steps + 1, S5_NOCT, ow).transpose(1, 2, 0), ((0, 0), (0, 0), (0, pad_k)))
    bt_re, bt_im = block_diag(bbr.transpose(0, 2, 1)), block_diag(bbi.transpose(0, 2, 1))
    cb_re, cb_im = block_diag(c_re.transpose(0, 2, 1)), block_diag(c_im.transpose(0, 2, 1))
    cw = steps * LANE
    oct_spec = lambda a, b: pl.BlockSpec((1, a, b), lambda q: (q, 0, 0))
    t_mat, w_re, w_im, m_re, m_im = pl.pallas_call(
        _s5_operator_kernel,
        grid=(S5_NOCT,),
        in_specs=[oct_spec(LANE, ow), oct_spec(LANE, ow), oct_spec(ow, LANE), oct_spec(ow, LANE),
                  oct_spec(S5_POW_PAD, ow), oct_spec(S5_POW_PAD, ow), oct_spec(ow, S5_POW_PAD),
                  oct_spec(ow, S5_POW_PAD)],
        out_specs=[oct_spec(cw, cw), oct_spec(cw, ow), oct_spec(cw, ow), oct_spec(ow, cw), oct_spec(ow, cw)],
        out_shape=[jax.ShapeDtypeStruct((S5_NOCT, cw, cw), BF16)]
                  + [jax.ShapeDtypeStruct((S5_NOCT, cw, ow), BF16)] * 2
                  + [jax.ShapeDtypeStruct((S5_NOCT, ow, cw), BF16)] * 2,
        compiler_params=_params("arbitrary"),
        name="s5_operators",
    )(bt_re, bt_im, cb_re, cb_im, lanes(pw_r), lanes(pw_i), rows(pw_r), rows(pw_i))
    return {
        "t": t_mat, "w_re": w_re, "w_im": w_im, "m_re": m_re, "m_im": m_im,
        "a_re": pw_r[steps].reshape(1, S5_STATE_W), "a_im": pw_i[steps].reshape(1, S5_STATE_W),
    }


def _softmax_update_t(s, vt, m_ref, l_ref, acc_ref):
    m_prev = m_ref[...]
    m_new = jnp.maximum(m_prev, jnp.max(s, axis=0, keepdims=True))
    alpha = jnp.exp2(m_prev - m_new)
    p = jnp.exp2(s - m_new)
    l_ref[...] = alpha * l_ref[...] + jnp.sum(p, axis=0, keepdims=True)
    acc_ref[...] = alpha * acc_ref[...] + _dot(vt, p.astype(BF16))
    m_ref[...] = m_new


def _attn_prompt_kernel(qt_ref, k_ref, vt_ref, kmeta_ref, vmeta_t_ref, wuvt_ref, o_ref, m_ref, l_ref, acc_ref, s_ref,
                        *, tq):
    qi = pl.program_id(1)
    qt = jnp.concatenate([qt_ref[h] for h in range(MLA_HEADS)], axis=1)

    def keys(j):
        return pl.ds(pl.multiple_of(j * tq, tq), tq)

    def scores(j):
        return _dot(k_ref[0, keys(j), :], qt)

    def update(s, j):
        _softmax_update_t(s, vt_ref[:, keys(j)], m_ref, l_ref, acc_ref)

    s = _dot(kmeta_ref[...], qt)
    m0 = jnp.max(s, axis=0, keepdims=True)
    p = jnp.exp2(s - m0)
    m_ref[...] = m0
    l_ref[...] = jnp.sum(p, axis=0, keepdims=True)
    acc_ref[...] = _dot(vmeta_t_ref[...], p.astype(BF16))

    s_ref[...] = scores(0)

    def pair_body(kp, carry):
        j = 2 * kp
        s_b = scores(j + 1)
        s_next = scores(j + 2)
        update(s_ref[...], j)
        update(s_b, j + 1)
        s_ref[...] = s_next
        return carry

    lax.fori_loop(0, qi // 2, pair_body, 0)

    @pl.when(qi % 2 == 1)
    def _():
        s_next = scores(qi)
        update(s_ref[...], qi - 1)
        s_ref[...] = s_next

    s = s_ref[...]
    t_k = lax.broadcasted_iota(jnp.int32, s.shape, 0)
    t_q = lax.broadcasted_iota(jnp.int32, s.shape, 1) % tq
    s = jnp.where(t_k <= t_q, s, NEG_INF)
    update(s, qi)

    ot = (acc_ref[...] / l_ref[...]).astype(BF16)
    for h in range(MLA_HEADS):
        yt = _dot(wuvt_ref[h], ot[:, h * tq:(h + 1) * tq])
        o_ref[:, h * V_HEAD:(h + 1) * V_HEAD] = yt.T.astype(o_ref.dtype)


def _attn_prompt(qt, kcat, vt, kmeta, vmeta_t, w_uv_t, nb, tq):
    t = kcat.shape[1]
    nq = t // tq
    cols = MLA_HEADS * tq
    return pl.pallas_call(
        functools.partial(_attn_prompt_kernel, tq=tq),
        grid=(nb, nq),
        in_specs=[
            pl.BlockSpec((MLA_HEADS, QK_CAT, tq), lambda b, i: (0, 0, b * nq + i)),
            pl.BlockSpec((1, t, QK_CAT), lambda b, i: (b, 0, 0)),
            pl.BlockSpec((KV_RANK, t), lambda b, i: (0, b)),
            _const_spec(kmeta.shape),
            _const_spec(vmeta_t.shape),
            _const_spec(w_uv_t.shape),
        ],
        out_specs=pl.BlockSpec((tq, MLA_HEADS * V_HEAD), lambda b, i: (b * nq + i, 0)),
        out_shape=jax.ShapeDtypeStruct((nb * t, MLA_HEADS * V_HEAD), BF16),
        scratch_shapes=[pltpu.VMEM((1, cols), F32), pltpu.VMEM((1, cols), F32), pltpu.VMEM((KV_RANK, cols), F32),
                        pltpu.VMEM((tq, cols), F32)],
        compiler_params=_params("arbitrary", "arbitrary"),
        name="attn_prompt",
    )(qt, kcat, vt, kmeta, vmeta_t, w_uv_t)


def _attn_meta_kernel(q_ref, kmeta_ref, wuv_ref, o_ref):
    rows = MLA_HEADS * N_META
    q = q_ref[...].reshape(rows, QK_CAT)
    km = kmeta_ref[...]
    s = _dot_nt(q, km)
    t_q = lax.broadcasted_iota(jnp.int32, s.shape, 0) % N_META
    t_k = lax.broadcasted_iota(jnp.int32, s.shape, 1)
    s = jnp.where(t_k <= t_q, s, NEG_INF)
    p = jnp.exp(s - jnp.max(s, axis=1, keepdims=True))
    o = _dot(p.astype(BF16), km[:, :KV_RANK]) / jnp.sum(p, axis=1, keepdims=True)
    o = o.astype(BF16)
    for h in range(MLA_HEADS):
        o_ref[:, h * V_HEAD:(h + 1) * V_HEAD] = _dot(o[h * N_META:(h + 1) * N_META], wuv_ref[h]).astype(o_ref.dtype)


def _attn_meta(q, kmeta, w_uv):
    return pl.pallas_call(
        _attn_meta_kernel,
        out_shape=jax.ShapeDtypeStruct((N_META, MLA_HEADS * V_HEAD), BF16),
        name="attn_meta",
    )(q, kmeta, w_uv)


def _attn_sample_kernel(pt_ref, q_ref, knew_ref, wuv_ref, ckv_hbm, kr_hbm, o_ref, ckv_buf, kr_buf, sem,
                        *, n_pages, steps):
    b = pl.program_id(0)
    slot = b % 2

    def page_copies(seq, into, j):
        page = pt_ref[seq, j]
        return (pltpu.make_async_copy(ckv_hbm.at[0, page], ckv_buf.at[into, j], sem.at[into, 0]),
                pltpu.make_async_copy(kr_hbm.at[0, page], kr_buf.at[into, j], sem.at[into, 1]))

    def for_each_page(seq, into, act):
        def body(j, carry):
            for copy in page_copies(seq, into, j):
                act(copy)
            return carry
        lax.fori_loop(0, n_pages, body, 0, unroll=4)

    @pl.when(b == 0)
    def _():
        for_each_page(0, 0, lambda copy: copy.start())

    @pl.when(b + 1 < pl.num_programs(0))
    def _():
        for_each_page(b + 1, 1 - slot, lambda copy: copy.start())

    pltpu.make_async_copy(ckv_hbm.at[0, pl.ds(0, n_pages)], ckv_buf.at[slot], sem.at[slot, 0]).wait()
    pltpu.make_async_copy(kr_hbm.at[0, pl.ds(0, n_pages)], kr_buf.at[slot], sem.at[slot, 1]).wait()

    rows = MLA_HEADS * steps
    q = q_ref[...].reshape(rows, QK_CAT).astype(BF16)
    q_lat = q[:, :KV_RANK]
    q_rope = q[:, KV_RANK:KV_RANK + QK_ROPE]
    groups = range(0, n_pages, PAGE_GROUP)

    def latent_keys(j0):
        return jnp.concatenate([ckv_buf[slot, j].astype(BF16) for j in range(j0, j0 + PAGE_GROUP)], axis=0)

    def chunk_scores(c):
        parts = []
        for j0 in groups[c * CHUNK_GROUPS:(c + 1) * CHUNK_GROUPS]:
            kr_t = jnp.concatenate([kr_buf[slot, j].astype(BF16) for j in range(j0, j0 + PAGE_GROUP)], axis=1)
            parts.append(_dot_nt(q_lat, latent_keys(j0)) + _dot(q_rope, kr_t))
        return jnp.concatenate(parts, axis=1)

    kn = jnp.concatenate([knew_ref[...], jnp.zeros((PAGE_SIZE - steps, QK_CAT), F32)], axis=0).astype(BF16)
    s_new = _dot_nt(q, kn)
    t_q = lax.broadcasted_iota(jnp.int32, s_new.shape, 0) % steps
    t_k = lax.broadcasted_iota(jnp.int32, s_new.shape, 1)
    s_new = jnp.where(t_k <= t_q, s_new, NEG_INF)

    gk = PAGE_GROUP * PAGE_SIZE
    n_chunks = len(groups) // CHUNK_GROUPS
    m = jnp.full((rows, 1), NEG_INF, F32)
    l = jnp.zeros((rows, 1), F32)
    acc = jnp.zeros((rows, KV_RANK), F32)
    s_cur = chunk_scores(0)
    for c in range(n_chunks + 1):
        s_next = chunk_scores(c + 1) if c + 1 < n_chunks else s_new
        m_new = jnp.maximum(m, jnp.max(s_cur, axis=1, keepdims=True))
        alpha = jnp.exp(m - m_new)
        p = jnp.exp(s_cur - m_new)
        l = alpha * l + jnp.sum(p, axis=1, keepdims=True)
        p = p.astype(BF16)
        if c < n_chunks:
            chunk_groups = groups[c * CHUNK_GROUPS:(c + 1) * CHUNK_GROUPS]
            pv = _dot(p[:, :gk], latent_keys(chunk_groups[0]))
            for i, j0 in enumerate(chunk_groups[1:], start=1):
                pv = pv + _dot(p[:, i * gk:(i + 1) * gk], latent_keys(j0))
        else:
            pv = _dot(p, kn[:, :KV_RANK])
        acc = alpha * acc + pv
        m = m_new
        s_cur = s_next
    o = (acc / l).astype(BF16)
    for h in range(MLA_HEADS):
        o_ref[:, h * V_HEAD:(h + 1) * V_HEAD] = _dot(o[h * steps:(h + 1) * steps], wuv_ref[h])


def _attn_sample(page_table, q, knew, cache_ckv, cache_kr_t, w_uv, steps):
    nb, n_pages = page_table.shape
    assert n_pages % (PAGE_GROUP * CHUNK_GROUPS) == 0
    grid_spec = pltpu.PrefetchScalarGridSpec(
        num_scalar_prefetch=1,
        grid=(nb,),
        in_specs=[pl.BlockSpec((MLA_HEADS, steps, QK_CAT), lambda b, pt: (0, b, 0)),
                  pl.BlockSpec((steps, QK_CAT), lambda b, pt: (b, 0)),
                  pl.BlockSpec(w_uv.shape, lambda b, pt: (0, 0, 0)),
                  pl.BlockSpec(memory_space=pl.ANY),
                  pl.BlockSpec(memory_space=pl.ANY)],
        out_specs=pl.BlockSpec((steps, MLA_HEADS * V_HEAD), lambda b, pt: (b, 0)),
        scratch_shapes=[pltpu.VMEM((2, n_pages, PAGE_SIZE, KV_RANK), F32),
                        pltpu.VMEM((2, n_pages, QK_ROPE, PAGE_SIZE), F32),
                        pltpu.SemaphoreType.DMA((2, 2))],
    )
    return pl.pallas_call(
        functools.partial(_attn_sample_kernel, n_pages=n_pages, steps=steps),
        grid_spec=grid_spec,
        out_shape=jax.ShapeDtypeStruct((nb * steps, MLA_HEADS * V_HEAD), F32),
        compiler_params=_params("arbitrary"),
        name="attn_sample",
    )(page_table, q, knew, w_uv, cache_ckv, cache_kr_t)


def _gelu_tanh(x):
    return 0.5 * x * (1.0 + jnp.tanh(math.sqrt(2.0 / math.pi) * (x + 0.044715 * (x * x * x))))


def _sigmoid(x):
    return 1.0 / (1.0 + jnp.exp(-x))


def _post_kernel(x_ref, yt_ref, u_ref, ymla_ref, p1_ref, p2_ref, d_ref, wglu_ref, wout_ref, gffn_ref,
                 wgate_ref, wup_ref, convw_ref, convb_ref, wdown_ref, gfin_ref,
                 y_ref, gate_ref, ybuf, hbuf, carry_ref, xnbuf, *, tm, seq_steps):
    long_seq = seq_steps == 0
    for o in range(S5_NOCT):
        for t in range(S5_CHUNK):
            ybuf[o, pl.ds(t, tm // S5_CHUNK, stride=S5_CHUNK), :] = yt_ref[o, t]
    y_ssm = jnp.concatenate([ybuf[o] for o in range(S5_NOCT)], axis=1)
    y = y_ssm + d_ref[...] * u_ref[...]
    y = _gelu_tanh(y)
    y = y * _sigmoid(_dot(y.astype(BF16), wglu_ref[...]))
    mixed = _dot(y.astype(BF16), wout_ref[:S5_W, :]) + _dot(ymla_ref[...].astype(BF16), wout_ref[S5_W:, :])
    x1 = x_ref[...] + mixed
    xnbuf[...] = _rms(x1, gffn_ref[...]).astype(BF16)

    row = lax.broadcasted_iota(jnp.int32, (tm, FF_CHUNK), 0)
    if long_seq:
        @pl.when(pl.program_id(1) == 0)
        def _():
            carry_ref[...] = p1_ref[...]
        t_in = row
    else:
        t_in = row % seq_steps

    for c in range(D_FF // FF_CHUNK):
        sl = slice(c * FF_CHUNK, (c + 1) * FF_CHUNK)
        gate = _dot(xnbuf[...], wgate_ref[:, sl])
        up = _dot(xnbuf[...], wup_ref[:, sl])
        if long_seq:
            back1 = jnp.broadcast_to(carry_ref[7:8, sl], gate.shape)
            back2 = jnp.where(t_in == 0, carry_ref[6:7, sl], back1)
            carry_ref[:, sl] = gate[tm - 8:, :]
            gate_ref[:, sl] = gate[tm - 8:, :]
        else:
            back1 = p1_ref[:, sl]
            back2 = p2_ref[:, sl]
            gate_ref[:, sl] = gate
        prev1 = jnp.where(t_in >= 1, pltpu.roll(gate, 1, 0), back1)
        prev2 = jnp.where(t_in >= 2, pltpu.roll(gate, 2, 0), back2)
        conv = (convb_ref[:, sl] + convw_ref[0:1, sl] * prev2 + convw_ref[1:2, sl] * prev1
                + convw_ref[2:3, sl] * gate)
        hbuf[:, sl] = (conv * _sigmoid(conv) * up).astype(BF16)

    x2 = x1 + _dot(hbuf[...], wdown_ref[...])
    y_ref[...] = _rms(x2, gfin_ref[...])


def _post_call(x2d, yt, u, y_mla, p1, p2, w, nb, tm, seq_steps):
    rows = x2d.shape[0]
    nt = rows // (nb * tm)
    assert nb * nt * tm == rows and tm % S5_CHUNK == 0
    long_seq = seq_steps == 0
    row = lambda width: pl.BlockSpec((tm, width), lambda b, i: (b * nt + i, 0))
    yt_spec = pl.BlockSpec((S5_NOCT, S5_CHUNK, tm // S5_CHUNK, LANE), lambda b, i: (0, 0, b * nt + i, 0))
    if long_seq:
        hist = [_const_spec((8, D_FF)), _const_spec((8, D_FF))]
        gate_spec = pl.BlockSpec((None, 8, D_FF), lambda b, i: (b, 0, 0))
        gate_shape = jax.ShapeDtypeStruct((nb, 8, D_FF), F32)
    else:
        hist = [row(D_FF), row(D_FF)]
        gate_spec = row(D_FF)
        gate_shape = jax.ShapeDtypeStruct((rows, D_FF), F32)
    return pl.pallas_call(
        functools.partial(_post_kernel, tm=tm, seq_steps=seq_steps),
        grid=(nb, nt),
        in_specs=[row(D_MODEL), yt_spec, row(S5_W), row(MLA_HEADS * V_HEAD)] + hist + [
            _const_spec((1, S5_W)), _const_spec((S5_W, S5_W)), _const_spec((D_MODEL, D_MODEL)),
            _const_spec((1, D_MODEL)), _const_spec((D_MODEL, D_FF)), _const_spec((D_MODEL, D_FF)),
            _const_spec((CONV_W, D_FF)), _const_spec((1, D_FF)), _const_spec((D_FF, D_MODEL)),
            _const_spec((1, D_MODEL))],
        out_specs=[row(D_MODEL), gate_spec],
        out_shape=[jax.ShapeDtypeStruct((rows, D_MODEL), F32), gate_shape],
        scratch_shapes=[pltpu.VMEM((S5_NOCT, tm, LANE), F32), pltpu.VMEM((tm, D_FF), BF16),
                        pltpu.VMEM((8, D_FF), F32), pltpu.VMEM((tm, D_MODEL), BF16)],
        compiler_params=_params("arbitrary", "arbitrary"),
        name="post",
    )(x2d, yt, u, y_mla, p1, p2, w["s5_d"], w["w_glu"], w["w_out"], w["g_ffn"], w["w_gate"], w["w_up"],
      w["conv_w"], w["conv_b"], w["w_down"], w["g_final"])


def _rope_tables(pos):
    half = QK_ROPE // 2
    inv = ROPE_BASE ** (-jnp.arange(half, dtype=F32) / half)
    ang = pos.astype(F32)[:, None] * inv[None, :]
    cos, sin = jnp.cos(ang), jnp.sin(ang)
    pad = jnp.zeros((pos.shape[0], ROPE_PAD - QK_ROPE), F32)
    return jnp.concatenate([cos, cos, pad], axis=1), jnp.concatenate([-sin, sin, pad], axis=1)


def kernel(x_prompt, x_sample, cache_ckv, cache_kr, state_s5_re, state_s5_im, state_conv, page_table, meta_tokens, g_mix, w_in, g_q, w_uq, g_kv, w_uk, w_uv, s5_a_re, s5_a_im, s5_log_dt, s5_b_re, s5_b_im, s5_c_re, s5_c_im, s5_d, w_glu, w_out, g_ffn, w_gate, w_up, conv_w, conv_b, w_down, g_final):
    assert w_in.shape[0] == 1, "single-layer step"
    nb, seq, _ = x_prompt.shape
    db, steps, _ = x_sample.shape
    n_past = page_table.shape[1] * PAGE_SIZE
    assert seq % S5_CHUNK == 0 and N_META % S5_CHUNK == 0 and steps == S5_CHUNK

    w_in_pad = jnp.concatenate([w_in[0], jnp.zeros((D_MODEL, IN_PAD - w_in.shape[2]), F32)], axis=1)
    wq = w_uq[0].reshape(Q_RANK, MLA_HEADS, QK_NOPE + QK_ROPE)
    wq_rope = jnp.concatenate([wq[:, :, QK_NOPE:], jnp.zeros((Q_RANK, MLA_HEADS, ROPE_PAD - QK_ROPE), F32)], axis=2)
    wq_perm = jnp.concatenate([wq[:, :, :QK_NOPE].reshape(Q_RANK, -1), wq_rope.reshape(Q_RANK, -1)], axis=1)
    w = {
        "g_mix": g_mix[0].reshape(1, -1), "w_in": w_in_pad.astype(BF16),
        "g_q": g_q[0].reshape(1, -1), "w_uq": wq_perm.astype(BF16), "w_uq_t": wq_perm.T.astype(BF16),
        "w_uk_t": w_uk[0].transpose(1, 2, 0).astype(BF16),
        "w_uk": w_uk[0].transpose(1, 0, 2).astype(BF16),
        "g_kv": g_kv[0].reshape(1, -1),
        "s5_d": s5_d[0].reshape(1, -1), "w_glu": w_glu[0].astype(BF16), "w_out": w_out[0].astype(BF16),
        "g_ffn": g_ffn[0].reshape(1, -1), "w_gate": w_gate[0].astype(BF16), "w_up": w_up[0].astype(BF16),
        "conv_w": conv_w[0], "conv_b": conv_b[0].reshape(1, -1), "w_down": w_down[0].astype(BF16),
        "g_final": g_final.reshape(1, -1),
    }
    w_uv_h = w_uv[0].transpose(1, 0, 2).astype(BF16)
    ops = _s5_operators(s5_a_re[0], s5_a_im[0], s5_log_dt[0], s5_b_re[0], s5_b_im[0], s5_c_re[0], s5_c_im[0])

    cos_m, sin_m = _rope_tables(jnp.arange(N_META, dtype=jnp.int32))
    cos_p, sin_p = _rope_tables(N_META + jnp.arange(seq, dtype=jnp.int32))
    tm_p = 512
    tm_s = 512
    cos_s, sin_s = _rope_tables(n_past + jnp.arange(tm_s, dtype=jnp.int32) % steps)

    tm_pre = 1024
    tiles_per_seq = seq // tm_pre
    u_m, ut_m, q_m, kcat_m, ckv_m, kr_m = _pre_call(meta_tokens, cos_m, sin_m, lambda i: (0, 0), N_META, BF16, w)
    u_p, ut_p, qt_p, kcat_p, ckv_p, kr_p, vt_p = _pre_call(
        x_prompt.reshape(nb * seq, D_MODEL), cos_p, sin_p, lambda i: (i % tiles_per_seq, 0), tm_pre, BF16, w,
        q_transposed=True)
    u_s, ut_s, q_s, kcat_s, ckv_s, kr_s = _pre_call(x_sample.reshape(db * steps, D_MODEL), cos_s, sin_s,
                                                    lambda i: (0, 0), tm_s, F32, w)

    zero_state = jnp.zeros((1, 1, S5_STATE_W), F32)
    yt_m, fm_re, fm_im = _s5_mixer(ut_m, zero_state, zero_state, ops, 1)
    yt_p, s5_re_p, s5_im_p = _s5_mixer(ut_p, jnp.broadcast_to(fm_re, (nb, 1, S5_STATE_W)),
                                       jnp.broadcast_to(fm_im, (nb, 1, S5_STATE_W)), ops, nb)
    yt_s, s5_re_s, s5_im_s = _s5_mixer(ut_s, state_s5_re[0].reshape(db, 1, S5_STATE_W),
                                       state_s5_im[0].reshape(db, 1, S5_STATE_W), ops, db)

    kmeta = jnp.concatenate([kcat_m, jnp.zeros((PAGE_SIZE - N_META, QK_CAT), BF16)], axis=0)
    ymla_m = _attn_meta(q_m, kmeta, w_uv_h)
    ymla_p = _attn_prompt(qt_p, kcat_p.reshape(nb, seq, QK_CAT), vt_p, kcat_m, kcat_m[:, :KV_RANK].T,
                          w_uv[0].transpose(1, 2, 0).astype(BF16), nb, tq=256)
    ymla_s = _attn_sample(page_table, q_s, kcat_s, cache_ckv, jnp.swapaxes(cache_kr, 2, 3), w_uv_h, steps)

    zeros_hist = jnp.zeros((8, D_FF), F32)
    _, gate_m = _post_call(meta_tokens, yt_m, u_m, ymla_m, zeros_hist, zeros_hist, w, 1, N_META, 0)
    y_p, gate_p = _post_call(x_prompt.reshape(nb * seq, D_MODEL), yt_p, u_p, ymla_p, gate_m[0], gate_m[0],
                             w, nb, tm_p, 0)
    sc = state_conv[0]
    back1 = jnp.pad(sc[:, 1:2], ((0, 0), (0, steps - 1), (0, 0))).reshape(db * steps, D_FF)
    back2 = jnp.pad(sc, ((0, 0), (0, steps - 2), (0, 0))).reshape(db * steps, D_FF)
    y_s, gate_s = _post_call(x_sample.reshape(db * steps, D_MODEL), yt_s, u_s, ymla_s, back1, back2,
                             w, 1, tm_s // 2, steps)

    def with_meta(meta_rows, tok_rows, width):
        return jnp.concatenate([jnp.broadcast_to(meta_rows[None], (nb, N_META, width)),
                                tok_rows.reshape(nb, seq, width)], axis=1)[None]

    return (y_p.reshape(nb, seq, D_MODEL), y_s.reshape(db, steps, D_MODEL),
            with_meta(ckv_m, ckv_p, KV_RANK), with_meta(kr_m, kr_p, QK_ROPE),
            s5_re_p.reshape(1, nb, S5_GROUPS, S5_STATE), s5_im_p.reshape(1, nb, S5_GROUPS, S5_STATE),
            gate_p[:, 8 - (CONV_W - 1):][None],
            ckv_s.reshape(1, db, steps, KV_RANK), kr_s.reshape(1, db, steps, QK_ROPE),
            s5_re_s.reshape(1, db, S5_GROUPS, S5_STATE), s5_im_s.reshape(1, db, S5_GROUPS, S5_STATE),
            gate_s.reshape(db, steps, D_FF)[:, steps - (CONV_W - 1):][None])
```
